```python
import math
import jax, jax.numpy as jnp
from jax import lax
import numpy as np

D_MODEL = 1024
BATCH = 2
SEQ = 8192
DEPTH = 2

N_EVEN = (DEPTH + 1) // 2
N_ODD = DEPTH // 2

ALPHA = (2 * DEPTH) ** 0.25
BETA = (8 * DEPTH) ** -0.25
LN_EPS = 1e-5

MLSTM_HEADS = 4
RET_HEADS = 4
HEAD_DIM = D_MODEL // 8
MLSTM_W = MLSTM_HEADS * HEAD_DIM
RET_W = RET_HEADS * HEAD_DIM
CHUNK = 128
CONV_WIDTH = 4
IN0_SPLITS = (MLSTM_W, 2 * MLSTM_W, 3 * MLSTM_W, 4 * MLSTM_W,
              4 * MLSTM_W + MLSTM_HEADS, 4 * MLSTM_W + 2 * MLSTM_HEADS,
              4 * MLSTM_W + 2 * MLSTM_HEADS + RET_W,
              4 * MLSTM_W + 2 * MLSTM_HEADS + 2 * RET_W,
              4 * MLSTM_W + 2 * MLSTM_HEADS + 3 * RET_W)
IN0_COLS = 4 * MLSTM_W + 2 * MLSTM_HEADS + 4 * RET_W

DIL_HEADS = 8
DIL_DH = D_MODEL // 16
DIL_W = DIL_HEADS * DIL_DH
DIL_CONFIGS = ((128, 1), (512, 4), (2048, 16))
IN1_COLS = len(DIL_CONFIGS) * 3 * DIL_W

N_EXPERTS = 32
N_GROUPS = 4
EXPERTS_PER_GROUP = N_EXPERTS // N_GROUPS
TOP_K = 2
D_EXPERT = D_MODEL // 4
MOE_BLOCK = 128

kernel_name = "hybrid_mlstm_retention_dilated_moe_deepnorm"


def layer_norm(x, g, b):
    xf = x.astype(jnp.float32)
    mu = jnp.mean(xf, -1, keepdims=True)
    var = jnp.mean(jnp.square(xf - mu), -1, keepdims=True)
    return ((xf - mu) * lax.rsqrt(var + LN_EPS)).astype(x.dtype) * g + b


def head_norm(h, g):
    B, T, H, dh = h.shape
    mu = jnp.mean(h, -1, keepdims=True)
    var = jnp.mean(jnp.square(h - mu), -1, keepdims=True)
    y = (h - mu) * lax.rsqrt(var + LN_EPS)
    return y.reshape(B, T, H * dh) * g.astype(jnp.float32)


def causal_depthwise_conv(x, w, b):
    K, C = w.shape
    y = lax.conv_general_dilated(x, w[:, None, :], window_strides=(1,),
                                 padding=[(K - 1, 0)],
                                 dimension_numbers=('NWC', 'WIO', 'NWC'),
                                 feature_group_count=C)
    return y + b


def to_chunks(a):
    n = a.shape[2] // CHUNK
    a = a.reshape(a.shape[:2] + (n, CHUNK) + a.shape[3:])
    return jnp.moveaxis(a, 2, 0)


def from_chunks(a):
    a = jnp.moveaxis(a, 0, 2)
    return a.reshape(a.shape[:2] + (a.shape[2] * a.shape[3],) + a.shape[4:])


def mlstm_chunkwise(q, k, v, i_pre, log_f):
    B, H, T, dh = q.shape
    causal = jnp.tril(jnp.ones((CHUNK, CHUNK), bool))

    def step(carry, inp):
        C, n, m = carry
        qb, kb, vb, ib, fb = inp
        b = jnp.cumsum(fb, axis=-1)
        g = b[..., -1]
        log_d = jnp.where(causal, b[..., :, None] - b[..., None, :] + ib[..., None, :], -jnp.inf)
        a = b + m[..., None]
        m_t = jnp.maximum(a, jnp.max(log_d, -1))
        d = jnp.exp(log_d - m_t[..., None])
        inter = jnp.exp(a - m_t)
        s = jnp.einsum('bhld,bhsd->bhls', qb, kb) * d
        num = jnp.einsum('bhls,bhse->bhle', s, vb) + inter[..., None] * jnp.einsum('bhld,bhde->bhle', qb, C)
        den = jnp.sum(s, -1) + inter * jnp.einsum('bhld,bhd->bhl', qb, n)
        h = num / jnp.maximum(jnp.abs(den), jnp.exp(-m_t))[..., None]
        log_w = g[..., None] - b + ib
        m_new = jnp.maximum(g + m, jnp.max(log_w, -1))
        w = jnp.exp(log_w - m_new[..., None])
        decay = jnp.exp(g + m - m_new)
        C_new = decay[..., None, None] * C + jnp.einsum('bhs,bhsd,bhse->bhde', w, kb, vb)
        n_new = decay[..., None] * n + jnp.einsum('bhs,bhsd->bhd', w, kb)
        return (C_new, n_new, m_new), h

    init = (jnp.zeros((B, H, dh, dh), jnp.float32), jnp.zeros((B, H, dh), jnp.float32),
            jnp.zeros((B, H), jnp.float32))
    _, hs = lax.scan(step, init, (to_chunks(q), to_chunks(k), to_chunks(v),
                                   to_chunks(i_pre), to_chunks(log_f)))
    return from_chunks(hs)


def retention_chunkwise(q, k, v, log_gamma):
    B, H, T, dh = q.shape
    idx = jnp.arange(CHUNK)
    rel = idx[:, None] - idx[None, :]
    decay_in = jnp.where(rel >= 0, jnp.exp(log_gamma[:, None, None] * jnp.maximum(rel, 0)), 0.0)
    xi = jnp.exp(log_gamma[:, None] * (idx + 1))
    zeta = jnp.exp(log_gamma[:, None] * (CHUNK - 1 - idx))
    g_chunk = jnp.exp(log_gamma * CHUNK)

    def step(R, inp):
        qb, kb, vb = inp
        inner = jnp.einsum('bhld,bhsd->bhls', qb, kb) * decay_in
        y = jnp.einsum('bhls,bhse->bhle', inner, vb) + \
            xi[None, :, :, None] * jnp.einsum('bhld,bhde->bhle', qb, R)
        R_new = g_chunk[None, :, None, None] * R + jnp.einsum('bhsd,hs,bhse->bhde', kb, zeta, vb)
        return R_new, y

    _, ys = lax.scan(step, jnp.zeros((B, H, dh, dh), jnp.float32),
                     (to_chunks(q), to_chunks(k), to_chunks(v)))
    return from_chunks(ys)


def alibi_slopes(n_heads):
    return jnp.exp2(-8.0 * jnp.arange(1, n_heads + 1, dtype=jnp.float32) / n_heads)


def dilated_band_attention(q, k, v, window, dilation, slopes):
    B, T, H, dh = q.shape
    L = window // dilation
    span = L * dilation
    T_pad = -(-T // span) * span
    NB = T_pad // span

    def blocks(a):
        a = jnp.pad(a.astype(jnp.float32), ((0, 0), (0, T_pad - T), (0, 0), (0, 0)))
        return a.reshape(B, NB, L, dilation, H, dh).transpose(0, 3, 4, 1, 2, 5)

    def with_prev(a):
        prev = jnp.pad(a, ((0, 0), (0, 0), (0, 0), (1, 0), (0, 0), (0, 0)))[:, :, :, :-1]
        return jnp.concatenate([prev, a], axis=4)

    qb = blocks(q)
    kk, vv = with_prev(blocks(k)), with_prev(blocks(v))
    qi = jnp.arange(L)[:, None]
    kr = jnp.arange(2 * L)[None, :]
    steps = qi + L - kr
    band = (steps >= 0) & (steps <= L)
    first = (jnp.arange(NB) == 0)[:, None, None]
    valid = band[None] & ~(first & (kr < L)[None])
    bias = -slopes[:, None, None] * (steps * dilation).astype(jnp.float32)
    s = jnp.einsum('brhnqd,brhnkd->brhnqk', qb, kk) * (dh ** -0.5) + bias[:, None]
    s = jnp.where(valid, s, -jnp.inf)
    m = jnp.max(s, -1, keepdims=True)
    p = jnp.exp(s - m)
    den = jnp.sum(p, -1)
    o = jnp.einsum('brhnqk,brhnkd->brhnqd', p, vv) / den[..., None]
    lse = m[..., 0] + jnp.log(den)
    o = o.transpose(0, 3, 4, 1, 2, 5).reshape(B, T_pad, H, dh)[:, :T]
    lse = lse.transpose(0, 3, 4, 1, 2).reshape(B, T_pad, H)[:, :T]
    return o, lse


def even_mixer(x, w_in, i_bias, f_bias, conv_w, conv_b, mlstm_norm_g, ret_norm_g, w_out):
    B, T, _ = x.shape
    f32 = jnp.float32
    p = x @ w_in
    q_m, k_m, v_m, o_m, i_m, f_m, q_r, k_r, v_r, g_r = jnp.split(p, IN0_SPLITS, axis=-1)
    qk_m = jax.nn.silu(causal_depthwise_conv(jnp.concatenate([q_m, k_m], -1), conv_w, conv_b))
    q_m, k_m = jnp.split(qk_m, 2, axis=-1)

    def heads(a, n):
        return a.astype(f32).reshape(B, T, n, HEAD_DIM).transpose(0, 2, 1, 3)

    scale = HEAD_DIM ** -0.5
    i_pre = (i_m + i_bias).astype(f32).transpose(0, 2, 1)
    log_f = jax.nn.log_sigmoid((f_m + f_bias).astype(f32)).transpose(0, 2, 1)
    h_m = mlstm_chunkwise(heads(q_m, MLSTM_HEADS), heads(k_m, MLSTM_HEADS) * scale,
                          heads(v_m, MLSTM_HEADS), i_pre, log_f).transpose(0, 2, 1, 3)
    h_m = jax.nn.sigmoid(o_m.astype(f32)).reshape(B, T, MLSTM_HEADS, HEAD_DIM) * h_m
    y_m = head_norm(h_m, mlstm_norm_g)
    log_gamma = jnp.log1p(-jnp.exp2(-5.0 - jnp.arange(RET_HEADS, dtype=f32)))
    h_r = retention_chunkwise(heads(q_r, RET_HEADS), heads(k_r, RET_HEADS) * scale,
                              heads(v_r, RET_HEADS), log_gamma).transpose(0, 2, 1, 3)
    y_r = head_norm(h_r, ret_norm_g) * jax.nn.silu(g_r.astype(f32))
    y = jnp.concatenate([y_m, y_r], -1).astype(x.dtype)
    return y @ w_out


def odd_mixer(x, w_in, w_out):
    B, T, _ = x.shape
    p = (x @ w_in).reshape(B, T, len(DIL_CONFIGS), 3, DIL_HEADS, DIL_DH)
    slopes = alibi_slopes(DIL_HEADS)
    outs, lses = [], []
    for g, (window, dilation) in enumerate(DIL_CONFIGS):
        o, l = dilated_band_attention(p[:, :, g, 0], p[:, :, g, 1], p[:, :, g, 2],
                                      window, dilation, slopes)
        outs.append(o)
        lses.append(l)
    wts = jax.nn.softmax(jnp.stack(lses, 0), axis=0)
    y = jnp.einsum('gbth,gbthd->bthd', wts, jnp.stack(outs, 0))
    return y.reshape(B, T, DIL_W).astype(x.dtype) @ w_out


def route(h, router_w, router_bias):
    N = h.shape[0]
    aff = jax.nn.sigmoid((h @ router_w).astype(jnp.float32))
    sel = (aff + router_bias.astype(jnp.float32)).reshape(N, N_GROUPS, EXPERTS_PER_GROUP)
    grp_score = jnp.sum(lax.top_k(sel, 2)[0], -1)
    g_best = jnp.argmax(grp_score, -1)
    in_grp = jnp.take_along_axis(sel, g_best[:, None, None], axis=1)[:, 0]
    _, local = lax.top_k(in_grp, TOP_K)
    expert_idx = g_best[:, None] * EXPERTS_PER_GROUP + local
    gate = jnp.take_along_axis(aff, expert_idx, axis=1)
    return expert_idx, gate / jnp.sum(gate, -1, keepdims=True)


def moe_ffn(h, router_w, router_bias, w_gate, w_up, w_down):
    B, T, D = h.shape
    N = B * T
    hf = h.reshape(N, D)
    expert_idx, gate = route(hf, router_w, router_bias)
    M = N * TOP_K
    e_flat = expert_idx.reshape(M)
    tok_flat = jnp.repeat(jnp.arange(N, dtype=jnp.int32), TOP_K)
    g_flat = gate.reshape(M)
    order = jnp.argsort(e_flat)
    e_s, tok_s, g_s = e_flat[order], tok_flat[order], g_flat[order]
    counts = jnp.bincount(e_flat, length=N_EXPERTS)
    padded = (counts + MOE_BLOCK - 1) // MOE_BLOCK * MOE_BLOCK
    start = jnp.cumsum(counts) - counts
    pend = jnp.cumsum(padded)
    pstart = pend - padded
    dest = pstart[e_s] + (jnp.arange(M) - start[e_s])
    P = M + N_EXPERTS * MOE_BLOCK
    slot_tok = jnp.zeros((P,), jnp.int32).at[dest].set(tok_s)
    slot_gate = jnp.zeros((P,), jnp.float32).at[dest].set(g_s)
    n_blocks = P // MOE_BLOCK
    block_e = jnp.minimum(jnp.searchsorted(pend, jnp.arange(n_blocks) * MOE_BLOCK, side='right'),
                          N_EXPERTS - 1)
    xs = hf[slot_tok].reshape(n_blocks, MOE_BLOCK, D)

    def expert_block(args):
        xb, e = args
        return (jax.nn.silu(xb @ w_gate[e]) * (xb @ w_up[e])) @ w_down[e]

    ys = lax.map(expert_block, (xs, block_e)).reshape(P, D)
    out = jnp.zeros((N, D), ys.dtype).at[slot_tok].add(ys * slot_gate[:, None].astype(ys.dtype))
    return out.reshape(B, T, D)


def setup_inputs(seed: int = 0) -> dict:
    key = jax.random.key(seed)
    ks = jax.random.split(key, 18)
    f32 = jnp.float32

    def nrm(k, shape, scale):
        return jax.random.normal(k, shape, f32) * scale

    return {
        "x": nrm(ks[0], (BATCH, SEQ, D_MODEL), 1.0),
        "w_in_even": nrm(ks[1], (N_EVEN, D_MODEL, IN0_COLS), D_MODEL ** -0.5),
        "i_bias": nrm(ks[2], (N_EVEN, MLSTM_HEADS), 0.1),
        "f_bias": jnp.linspace(3.0, 6.0, MLSTM_HEADS)[None] + nrm(ks[3], (N_EVEN, MLSTM_HEADS), 0.1),
        "conv_w": nrm(ks[4], (N_EVEN, CONV_WIDTH, 2 * MLSTM_W), CONV_WIDTH ** -0.5),
        "conv_b": nrm(ks[5], (N_EVEN, 2 * MLSTM_W), 0.02),
        "mlstm_norm_g": 1.0 + nrm(ks[6], (N_EVEN, MLSTM_W), 0.02),
        "ret_norm_g": 1.0 + nrm(ks[7], (N_EVEN, RET_W), 0.02),
        "w_out_even": nrm(ks[8], (N_EVEN, MLSTM_W + RET_W, D_MODEL), (MLSTM_W + RET_W) ** -0.5 * BETA),
        "w_in_odd": nrm(ks[9], (N_ODD, D_MODEL, IN1_COLS), D_MODEL ** -0.5),
        "w_out_odd": nrm(ks[10], (N_ODD, DIL_W, D_MODEL), DIL_W ** -0.5 * BETA),
        "router_w": nrm(ks[11], (D_MODEL, N_EXPERTS), D_MODEL ** -0.5),
        "router_bias": nrm(ks[12], (N_EXPERTS,), 0.01),
        "w_gate": nrm(ks[13], (DEPTH, N_EXPERTS, D_MODEL, D_EXPERT), D_MODEL ** -0.5),
        "w_up": nrm(ks[14], (DEPTH, N_EXPERTS, D_MODEL, D_EXPERT), D_MODEL ** -0.5),
        "w_down": nrm(ks[15], (DEPTH, N_EXPERTS, D_EXPERT, D_MODEL), D_EXPERT ** -0.5 * BETA),
        "ln_g": 1.0 + nrm(ks[16], (DEPTH, 2, D_MODEL), 0.02),
        "ln_b": nrm(ks[17], (DEPTH, 2, D_MODEL), 0.02),
    }


def reference(x, w_in_even, i_bias, f_bias, conv_w, conv_b, mlstm_norm_g, ret_norm_g,
              w_out_even, w_in_odd, w_out_odd, router_w, router_bias, w_gate, w_up,
              w_down, ln_g, ln_b):
    h = x
    for layer in range(DEPTH):
        j = layer // 2
        if layer % 2 == 0:
            mix = even_mixer(h, w_in_even[j], i_bias[j], f_bias[j], conv_w[j], conv_b[j],
                             mlstm_norm_g[j], ret_norm_g[j], w_out_even[j])
        else:
            mix = odd_mixer(h, w_in_odd[j], w_out_odd[j])
        h = layer_norm(ALPHA * h + mix, ln_g[layer, 0], ln_b[layer, 0])
        ffn = moe_ffn(h, router_w, router_bias, w_gate[layer], w_up[layer], w_down[layer])
        h = layer_norm(ALPHA * h + ffn, ln_g[layer, 1], ln_b[layer, 1])
    return h
```

```python
import functools

import jax
import jax.numpy as jnp
from jax import lax
from jax.experimental import pallas as pl
from jax.experimental.pallas import tpu as pltpu

F32 = jnp.float32
BF16 = jnp.bfloat16

D_MODEL = 1024
DEPTH = 2
ALPHA = (2 * DEPTH) ** 0.25
LN_EPS = 1e-5
MLSTM_HEADS = 4
RET_HEADS = 4
HEAD_DIM = 128
MLSTM_W = MLSTM_HEADS * HEAD_DIM
RET_W = RET_HEADS * HEAD_DIM
CHUNK = 128
CONV_WIDTH = 4
DIL_HEADS = 8
DIL_DH = 64
DIL_W = DIL_HEADS * DIL_DH
DIL_CONFIGS = ((128, 1), (512, 4), (2048, 16))
N_EXPERTS = 32
N_GROUPS = 4
EXPERTS_PER_GROUP = N_EXPERTS // N_GROUPS
TOP_K = 2
D_EXPERT = 256

LANES = 128
SUBLANES = 8
VMEM_LIMIT_BYTES = 56 * 1024 * 1024

PROJ_TM = 512
POST_TM = 512
MOE_BLK = 256
COMB_TM = 256
CONV_TAIL = SUBLANES

EV_QM, EV_KM, EV_VM, EV_OM = 0, MLSTM_W, 2 * MLSTM_W, 3 * MLSTM_W
EV_QR, EV_KR, EV_VR, EV_GR = (4 * MLSTM_W, 4 * MLSTM_W + RET_W, 4 * MLSTM_W + 2 * RET_W,
                              4 * MLSTM_W + 3 * RET_W)
EV_GATE = 4 * MLSTM_W + 4 * RET_W
EV_COLS = EV_GATE + LANES

NEG_INF = float("-inf")


def _cparams(sem):
    return pltpu.CompilerParams(dimension_semantics=sem, vmem_limit_bytes=VMEM_LIMIT_BYTES)


def _proj_kernel(x_ref, w_ref, o_ref):
    o_ref[...] = jnp.dot(x_ref[...].astype(BF16), w_ref[...],
                         preferred_element_type=F32).astype(o_ref.dtype)


def _project(x, w, tn, name):
    n, k = x.shape
    m = w.shape[1]
    return pl.pallas_call(
        _proj_kernel,
        grid=(n // PROJ_TM, m // tn),
        in_specs=[pl.BlockSpec((PROJ_TM, k), lambda i, j: (i, 0)),
                  pl.BlockSpec((k, tn), lambda i, j: (0, j))],
        out_specs=pl.BlockSpec((PROJ_TM, tn), lambda i, j: (i, j)),
        out_shape=jax.ShapeDtypeStruct((n, m), F32),
        compiler_params=_cparams(("parallel", "arbitrary")),
        name=name,
    )(x, w)


def _silu(x):
    return x * jax.nn.sigmoid(x)


def _head_norm(h):
    mu = jnp.mean(h, -1, keepdims=True)
    c = h - mu
    var = jnp.mean(c * c, -1, keepdims=True)
    return c * lax.rsqrt(var + LN_EPS)


def _dot_nt(a, b):
    return lax.dot_general(a, b, (((1,), (1,)), ((), ())), preferred_element_type=F32)


def _dot_tn(a_f32, b):
    return jnp.dot(a_f32.T.astype(BF16), b, preferred_element_type=F32)


def _even_mixer_kernel(p_ref, convw_ref, convb_ref, gbias_ref, dec_ref, rc_ref, ng_ref, y_ref,
                       qkbuf, c_ref, n_ref, m_ref, r_ref):
    L = CHUNK
    scale = HEAD_DIM ** -0.5

    @pl.when(pl.program_id(1) == 0)
    def _():
        qkbuf[0:CONV_TAIL, :] = jnp.zeros((CONV_TAIL, 2 * MLSTM_W), F32)
        c_ref[...] = jnp.zeros_like(c_ref)
        n_ref[...] = jnp.zeros_like(n_ref)
        m_ref[...] = jnp.zeros_like(m_ref)
        r_ref[...] = jnp.zeros_like(r_ref)

    qkbuf[CONV_TAIL:CONV_TAIL + L, :] = p_ref[:, EV_QM:EV_QM + 2 * MLSTM_W]
    acc = jnp.broadcast_to(convb_ref[...], (L, 2 * MLSTM_W))
    for k in range(CONV_WIDTH):
        off = CONV_TAIL - (CONV_WIDTH - 1) + k
        acc = acc + convw_ref[k:k + 1, :] * qkbuf[off:off + L, :]
    qk = _silu(acc)
    qkbuf[0:CONV_TAIL, :] = qkbuf[L:L + CONV_TAIL, :]

    pre = p_ref[:, EV_GATE:EV_GATE + LANES] + gbias_ref[...]
    logf = jnp.minimum(pre, 0.0) - jnp.log1p(jnp.exp(-jnp.abs(pre)))
    row = lax.broadcasted_iota(jnp.int32, (L, L), 0)
    col = lax.broadcasted_iota(jnp.int32, (L, L), 1)
    causal = row >= col
    bcs = jnp.dot(causal.astype(F32), logf, preferred_element_type=F32,
                  precision=lax.Precision.HIGHEST)
    u_t = (pre - pltpu.roll(bcs, LANES - MLSTM_HEADS, 1)).T

    for h in range(MLSTM_HEADS):
        lo = h * HEAD_DIM
        b_col = bcs[:, MLSTM_HEADS + h:MLSTM_HEADS + h + 1]
        i_col = pre[:, h:h + 1]
        r_row = u_t[h:h + 1, :]
        m_prev = m_ref[h:h + 1, 0:1]
        log_d = jnp.where(causal, b_col + r_row, NEG_INF)
        a = b_col + m_prev
        m_t = jnp.maximum(a, jnp.max(log_d, -1, keepdims=True))
        d = jnp.exp(log_d - m_t)
        inter = jnp.exp(a - m_t)
        q = qk[:, lo:lo + HEAD_DIM]
        k = qk[:, MLSTM_W + lo:MLSTM_W + lo + HEAD_DIM] * scale
        qb = q.astype(BF16)
        vb = p_ref[:, EV_VM + lo:EV_VM + lo + HEAD_DIM].astype(BF16)
        s = _dot_nt(qb, k.astype(BF16)) * d
        c_old = c_ref[h]
        n_old = n_ref[h:h + 1, :]
        num = jnp.dot(s.astype(BF16), vb, preferred_element_type=F32) + \
            inter * jnp.dot(qb, c_old.astype(BF16), preferred_element_type=F32)
        den = jnp.sum(s, -1, keepdims=True) + inter * jnp.sum(q * n_old, -1, keepdims=True)
        hh = num / jnp.maximum(jnp.abs(den), jnp.exp(-m_t))
        g = b_col[L - 1:L, :]
        log_w = g - b_col + i_col
        m_new = jnp.maximum(g + m_prev, jnp.max(log_w, 0, keepdims=True))
        kw = k * jnp.exp(log_w - m_new)
        decay = jnp.exp(g + m_prev - m_new)
        c_ref[h] = decay * c_old + _dot_tn(kw, vb)
        n_ref[h:h + 1, :] = decay * n_old + jnp.sum(kw, 0, keepdims=True)
        m_ref[h:h + 1, :] = jnp.broadcast_to(m_new, (1, LANES))
        o_gate = jax.nn.sigmoid(p_ref[:, EV_OM + lo:EV_OM + lo + HEAD_DIM])
        y_ref[:, lo:lo + HEAD_DIM] = (_head_norm(o_gate * hh) * ng_ref[:, lo:lo + HEAD_DIM]).astype(y_ref.dtype)

    for h in range(RET_HEADS):
        lo = h * HEAD_DIM
        qb = p_ref[:, EV_QR + lo:EV_QR + lo + HEAD_DIM].astype(BF16)
        k = p_ref[:, EV_KR + lo:EV_KR + lo + HEAD_DIM] * scale
        vb = p_ref[:, EV_VR + lo:EV_VR + lo + HEAD_DIM].astype(BF16)
        xi = rc_ref[:, h:h + 1]
        zeta = rc_ref[:, RET_HEADS + h:RET_HEADS + h + 1]
        g_chunk = rc_ref[0:1, 2 * RET_HEADS + h:2 * RET_HEADS + h + 1]
        r_old = r_ref[h]
        inner = _dot_nt(qb, k.astype(BF16)) * dec_ref[h]
        y = jnp.dot(inner.astype(BF16), vb, preferred_element_type=F32) + \
            xi * jnp.dot(qb, r_old.astype(BF16), preferred_element_type=F32)
        r_ref[h] = g_chunk * r_old + _dot_tn(k * zeta, vb)
        gr = p_ref[:, EV_GR + lo:EV_GR + lo + HEAD_DIM]
        out = _head_norm(y) * ng_ref[:, MLSTM_W + lo:MLSTM_W + lo + HEAD_DIM] * _silu(gr)
        y_ref[:, MLSTM_W + lo:MLSTM_W + lo + HEAD_DIM] = out.astype(y_ref.dtype)


def _even_mixer(p, batch, seq, conv_w, conv_b, gate_bias, decay_in, ret_consts, norm_g):
    nchunks = seq // CHUNK
    const2 = lambda b, c: (0, 0)
    return pl.pallas_call(
        _even_mixer_kernel,
        grid=(batch, nchunks),
        in_specs=[pl.BlockSpec((CHUNK, EV_COLS), lambda b, c: (b * nchunks + c, 0)),
                  pl.BlockSpec((CONV_WIDTH, 2 * MLSTM_W), const2),
                  pl.BlockSpec((1, 2 * MLSTM_W), const2),
                  pl.BlockSpec((1, LANES), const2),
                  pl.BlockSpec((RET_HEADS, CHUNK, CHUNK), lambda b, c: (0, 0, 0)),
                  pl.BlockSpec((CHUNK, LANES), const2),
                  pl.BlockSpec((1, MLSTM_W + RET_W), const2)],
        out_specs=pl.BlockSpec((CHUNK, MLSTM_W + RET_W), lambda b, c: (b * nchunks + c, 0)),
        out_shape=jax.ShapeDtypeStruct((batch * seq, MLSTM_W + RET_W), BF16),
        scratch_shapes=[pltpu.VMEM((CONV_TAIL + CHUNK, 2 * MLSTM_W), F32),
                        pltpu.VMEM((MLSTM_HEADS, HEAD_DIM, HEAD_DIM), F32),
                        pltpu.VMEM((SUBLANES, HEAD_DIM), F32),
                        pltpu.VMEM((SUBLANES, LANES), F32),
                        pltpu.VMEM((RET_HEADS, HEAD_DIM, HEAD_DIM), F32)],
        compiler_params=_cparams(("arbitrary", "arbitrary")),
        name="even_mixer",
    )(p, conv_w, conv_b, gate_bias, decay_in, ret_consts, norm_g)


def _layer_norm(z, g, b):
    mu = jnp.mean(z, -1, keepdims=True)
    c = z - mu
    var = jnp.mean(c * c, -1, keepdims=True)
    return c * lax.rsqrt(var + LN_EPS) * g + b


def _route(h, rwt_ref, rb_ref, eidx_ref, gate_ref):
    logits = lax.dot_general(rwt_ref[...], h, (((1,), (1,)), ((), ())),
                             preferred_element_type=F32, precision=lax.Precision.HIGHEST)
    aff = jax.nn.sigmoid(logits)
    sel = aff + rb_ref[...]
    tm = h.shape[0]
    sub = lax.broadcasted_iota(jnp.int32, (EXPERTS_PER_GROUP, tm), 0)
    best = None
    for g in range(N_GROUPS):
        lo = g * EXPERTS_PER_GROUP
        sg = sel[lo:lo + EXPERTS_PER_GROUP, :]
        ag = aff[lo:lo + EXPERTS_PER_GROUP, :]
        v1 = jnp.max(sg, 0, keepdims=True)
        i1 = jnp.min(jnp.where(sg == v1, sub, EXPERTS_PER_GROUP), 0, keepdims=True)
        rest = jnp.where(sub == i1, NEG_INF, sg)
        v2 = jnp.max(rest, 0, keepdims=True)
        i2 = jnp.min(jnp.where(rest == v2, sub, EXPERTS_PER_GROUP), 0, keepdims=True)
        a1 = jnp.sum(jnp.where(sub == i1, ag, 0.0), 0, keepdims=True)
        a2 = jnp.sum(jnp.where(sub == i2, ag, 0.0), 0, keepdims=True)
        cand = (v1 + v2, i1 + lo, i2 + lo, a1, a2)
        if best is None:
            best = cand
        else:
            take = cand[0] > best[0]
            best = tuple(jnp.where(take, c, b) for c, b in zip(cand, best))
    _, e1, e2, a1, a2 = best
    tot = a1 + a2
    eidx_ref[0:1, :] = e1
    eidx_ref[1:2, :] = e2
    gate_ref[0:1, :] = a1 / tot
    gate_ref[1:2, :] = a2 / tot


def _post_even_kernel(x_ref, y_ref, wo_ref, lng_ref, lnb_ref, rwt_ref, rb_ref,
                      h_ref, eidx_ref, gate_ref):
    mix = jnp.dot(y_ref[...], wo_ref[...], preferred_element_type=F32)
    h = _layer_norm(ALPHA * x_ref[...] + mix, lng_ref[...], lnb_ref[...])
    h_ref[...] = h
    _route(h, rwt_ref, rb_ref, eidx_ref, gate_ref)


def _post_odd_kernel(x_ref, o0_ref, o1_ref, o2_ref, l0_ref, l1_ref, l2_ref, wo_ref, lng_ref, lnb_ref,
                     rwt_ref, rb_ref, h_ref, eidx_ref, gate_ref):
    l0, l1, l2 = l0_ref[...], l1_ref[...], l2_ref[...]
    m = jnp.maximum(jnp.maximum(l0, l1), l2)
    e0, e1, e2 = jnp.exp(l0 - m), jnp.exp(l1 - m), jnp.exp(l2 - m)
    y = (e0 * o0_ref[...] + e1 * o1_ref[...] + e2 * o2_ref[...]) / (e0 + e1 + e2)
    mix = jnp.dot(y.astype(BF16), wo_ref[...], preferred_element_type=F32)
    h = _layer_norm(ALPHA * x_ref[...] + mix, lng_ref[...], lnb_ref[...])
    h_ref[...] = h
    _route(h, rwt_ref, rb_ref, eidx_ref, gate_ref)


def _post_mixer(kernel_fn, x, mixer_inputs, w_out, ln_g, ln_b, router_wt, router_b, name):
    n = x.shape[0]
    tm = POST_TM
    const2 = lambda i: (0, 0)
    row_spec = lambda a: pl.BlockSpec((tm, a.shape[1]), lambda i: (i, 0))
    return pl.pallas_call(
        kernel_fn,
        grid=(n // tm,),
        in_specs=[row_spec(x)] + [row_spec(a) for a in mixer_inputs] + [
            pl.BlockSpec(w_out.shape, const2),
            pl.BlockSpec((1, D_MODEL), const2),
            pl.BlockSpec((1, D_MODEL), const2),
            pl.BlockSpec((N_EXPERTS, D_MODEL), const2),
            pl.BlockSpec((N_EXPERTS, 1), const2)],
        out_specs=[pl.BlockSpec((tm, D_MODEL), lambda i: (i, 0)),
                   pl.BlockSpec((TOP_K, tm), lambda i: (0, i)),
                   pl.BlockSpec((TOP_K, tm), lambda i: (0, i))],
        out_shape=[jax.ShapeDtypeStruct((n, D_MODEL), F32),
                   jax.ShapeDtypeStruct((TOP_K, n), jnp.int32),
                   jax.ShapeDtypeStruct((TOP_K, n), F32)],
        compiler_params=_cparams(("parallel",)),
        name=name,
    )(x, *mixer_inputs, w_out, ln_g, ln_b, router_wt, router_b)


def _dil_attn_kernel(q_ref, kc_ref, kp_ref, vc_ref, vp_ref, o_ref, lse_ref, *, dilation):
    L = CHUNK
    first = pl.program_id(2) == 0
    qi = lax.broadcasted_iota(jnp.int32, (L, L), 0)
    kr = lax.broadcasted_iota(jnp.int32, (L, L), 1)
    steps_c = (qi - kr).astype(F32)
    steps_p = steps_c + float(L)
    valid_c = qi >= kr
    valid_p = jnp.logical_and(kr >= qi, jnp.logical_not(first))
    scale = DIL_DH ** -0.5
    for h in range(DIL_HEADS):
        slope = 2.0 ** (-8.0 * (h + 1) / DIL_HEADS) * dilation
        lo = h * DIL_DH
        qb = q_ref[:, lo:lo + DIL_DH].astype(BF16)
        s_c = _dot_nt(qb, kc_ref[:, lo:lo + DIL_DH].astype(BF16)) * scale - slope * steps_c
        s_p = _dot_nt(qb, kp_ref[:, lo:lo + DIL_DH].astype(BF16)) * scale - slope * steps_p
        s_c = jnp.where(valid_c, s_c, NEG_INF)
        s_p = jnp.where(valid_p, s_p, NEG_INF)
        m = jnp.maximum(jnp.max(s_c, -1, keepdims=True), jnp.max(s_p, -1, keepdims=True))
        p_c = jnp.exp(s_c - m)
        p_p = jnp.exp(s_p - m)
        den = jnp.sum(p_c, -1, keepdims=True) + jnp.sum(p_p, -1, keepdims=True)
        o = jnp.dot(p_c.astype(BF16), vc_ref[:, lo:lo + DIL_DH].astype(BF16), preferred_element_type=F32) + \
            jnp.dot(p_p.astype(BF16), vp_ref[:, lo:lo + DIL_DH].astype(BF16), preferred_element_type=F32)
        o_ref[:, lo:lo + DIL_DH] = o / den
        lse_ref[:, lo:lo + DIL_DH] = jnp.broadcast_to(m + jnp.log(den), (L, DIL_DH))


def _dilated_attention(p, batch, seq, group, dilation):
    n, cols = p.shape
    pv = p.reshape(n // dilation, dilation * cols)
    nbs = seq // dilation // CHUNK
    cb = cols // DIL_W
    base = group * 3

    def cur(c):
        return pl.BlockSpec((CHUNK, DIL_W), lambda b, r, j: (b * nbs + j, r * cb + base + c))

    def prev(c):
        return pl.BlockSpec((CHUNK, DIL_W),
                            lambda b, r, j: (b * nbs + jnp.maximum(j - 1, 0), r * cb + base + c))

    out_spec = pl.BlockSpec((CHUNK, DIL_W), lambda b, r, j: (b * nbs + j, r))
    out_sds = jax.ShapeDtypeStruct((n // dilation, dilation * DIL_W), F32)
    o, lse = pl.pallas_call(
        functools.partial(_dil_attn_kernel, dilation=dilation),
        grid=(batch, dilation, nbs),
        in_specs=[cur(0), cur(1), prev(1), cur(2), prev(2)],
        out_specs=[out_spec, out_spec],
        out_shape=[out_sds, out_sds],
        compiler_params=_cparams(("parallel", "parallel", "arbitrary")),
        name=f"dil_attn_d{dilation}",
    )(pv, pv, pv, pv, pv)
    return o.reshape(n, DIL_W), lse.reshape(n, DIL_W)


def _row_copy(src_hbm, row, dst, i, sem):
    return pltpu.make_async_copy(src_hbm.at[pl.ds(row, 1), :], dst.at[pl.ds(i, 1), :], sem)


def _experts_kernel(slot_tok_ref, block_e_ref, nused_ref, h_hbm, gate_ref, wg_ref, wu_ref, wd_ref,
                    ys_ref, xbuf, sems):
    j = pl.program_id(0)
    nused = nused_ref[0]

    def issue(blk, slot):
        def body(i, _):
            _row_copy(h_hbm, slot_tok_ref[blk * MOE_BLK + i], xbuf.at[slot], i, sems.at[slot]).start()
            return 0
        lax.fori_loop(0, MOE_BLK, body, 0)

    def wait(slot):
        def body(i, _):
            _row_copy(h_hbm, 0, xbuf.at[slot], i, sems.at[slot]).wait()
            return 0
        lax.fori_loop(0, MOE_BLK, body, 0)

    slot = j % 2

    @pl.when(jnp.logical_and(j == 0, nused > 0))
    def _():
        issue(0, 0)

    @pl.when(j + 1 < nused)
    def _():
        issue(j + 1, 1 - slot)

    @pl.when(j < nused)
    def _():
        wait(slot)
        x = xbuf[slot].astype(BF16)
        a = jnp.dot(x, wg_ref[0], preferred_element_type=F32)
        u = jnp.dot(x, wu_ref[0], preferred_element_type=F32)
        hm = (_silu(a) * u).astype(BF16)
        y = jnp.dot(hm, wd_ref[0], preferred_element_type=F32)
        ys_ref[...] = y * gate_ref[:, 0:1]

    @pl.when(j >= nused)
    def _():
        ys_ref[...] = jnp.zeros_like(ys_ref)


def _experts(h, slot_tok, block_e, nused, slot_gate, wg, wu, wd):
    p_slots = slot_tok.shape[0]
    n_blocks = p_slots // MOE_BLK
    grid_spec = pltpu.PrefetchScalarGridSpec(
        num_scalar_prefetch=3,
        grid=(n_blocks,),
        in_specs=[pl.BlockSpec(memory_space=pl.ANY),
                  pl.BlockSpec((MOE_BLK, LANES), lambda j, st, be, nu: (j, 0)),
                  pl.BlockSpec((1, D_MODEL, D_EXPERT), lambda j, st, be, nu: (be[j], 0, 0)),
                  pl.BlockSpec((1, D_MODEL, D_EXPERT), lambda j, st, be, nu: (be[j], 0, 0)),
                  pl.BlockSpec((1, D_EXPERT, D_MODEL), lambda j, st, be, nu: (be[j], 0, 0))],
        out_specs=pl.BlockSpec((MOE_BLK, D_MODEL), lambda j, st, be, nu: (j, 0)),
        scratch_shapes=[pltpu.VMEM((2, MOE_BLK, D_MODEL), F32),
                        pltpu.SemaphoreType.DMA((2,))],
    )
    return pl.pallas_call(
        _experts_kernel,
        grid_spec=grid_spec,
        out_shape=jax.ShapeDtypeStruct((p_slots, D_MODEL), F32),
        compiler_params=_cparams(("arbitrary",)),
        name="moe_experts",
    )(slot_tok, block_e, nused, h, slot_gate, wg, wu, wd)


def _combine_kernel(dest_ref, h_ref, ys_hbm, lng_ref, lnb_ref, o_ref, buf, sems):
    i = pl.program_id(0)
    nt = pl.num_programs(0)
    n_tok = nt * COMB_TM

    def issue(tile, slot):
        def body(t, _):
            for k in range(TOP_K):
                row = dest_ref[k * n_tok + tile * COMB_TM + t]
                _row_copy(ys_hbm, row, buf.at[slot, k], t, sems.at[slot]).start()
            return 0
        lax.fori_loop(0, COMB_TM, body, 0)

    def wait(slot):
        def body(t, _):
            for k in range(TOP_K):
                _row_copy(ys_hbm, 0, buf.at[slot, k], t, sems.at[slot]).wait()
            return 0
        lax.fori_loop(0, COMB_TM, body, 0)

    slot = i % 2

    @pl.when(i == 0)
    def _():
        issue(0, 0)

    @pl.when(i + 1 < nt)
    def _():
        issue(i + 1, 1 - slot)

    wait(slot)
    z = ALPHA * h_ref[...] + (buf[slot, 0] + buf[slot, 1])
    o_ref[...] = _layer_norm(z, lng_ref[...], lnb_ref[...])


def _combine(h, ys, dest, ln_g, ln_b):
    n = h.shape[0]
    grid_spec = pltpu.PrefetchScalarGridSpec(
        num_scalar_prefetch=1,
        grid=(n // COMB_TM,),
        in_specs=[pl.BlockSpec((COMB_TM, D_MODEL), lambda i, d: (i, 0)),
                  pl.BlockSpec(memory_space=pl.ANY),
                  pl.BlockSpec((1, D_MODEL), lambda i, d: (0, 0)),
                  pl.BlockSpec((1, D_MODEL), lambda i, d: (0, 0))],
        out_specs=pl.BlockSpec((COMB_TM, D_MODEL), lambda i, d: (i, 0)),
        scratch_shapes=[pltpu.VMEM((2, TOP_K, COMB_TM, D_MODEL), F32),
                        pltpu.SemaphoreType.DMA((2,))],
    )
    return pl.pallas_call(
        _combine_kernel,
        grid_spec=grid_spec,
        out_shape=jax.ShapeDtypeStruct((n, D_MODEL), F32),
        compiler_params=_cparams(("arbitrary",)),
        name="moe_combine",
    )(dest, h, ys, ln_g, ln_b)


def _slot_tables(eidx, gate):
    n = eidx.shape[1]
    chosen = (eidx[0][:, None] == jnp.arange(N_EXPERTS)[None, :]) | \
             (eidx[1][:, None] == jnp.arange(N_EXPERTS)[None, :])
    chosen = chosen.astype(jnp.int32)
    incl = jnp.cumsum(chosen, axis=0)
    counts = incl[-1]
    rank = incl - chosen
    padded = (counts + MOE_BLK - 1) // MOE_BLK * MOE_BLK
    pend = jnp.cumsum(padded)
    pstart = pend - padded
    pos = pstart[None, :] + rank
    dest = jnp.take_along_axis(pos, eidx.T, axis=1).T
    p_slots = n * TOP_K + N_EXPERTS * MOE_BLK
    tok = jnp.broadcast_to(jnp.arange(n, dtype=jnp.int32)[None, :], (TOP_K, n))
    slot_tok = jnp.zeros((p_slots,), jnp.int32).at[dest.reshape(-1)].set(tok.reshape(-1))
    slot_gate = jnp.zeros((p_slots,), F32).at[dest.reshape(-1)].set(gate.reshape(-1))
    n_blocks = p_slots // MOE_BLK
    block_e = jnp.minimum(jnp.searchsorted(pend, jnp.arange(n_blocks) * MOE_BLK, side='right'),
                          N_EXPERTS - 1).astype(jnp.int32)
    nused = (pend[-1:] // MOE_BLK).astype(jnp.int32)
    slot_gate_b = jnp.broadcast_to(slot_gate[:, None], (p_slots, LANES))
    return slot_tok, block_e, nused, slot_gate_b, dest.reshape(-1).astype(jnp.int32)


def _moe_and_norm(h, eidx, gate, wg, wu, wd, ln_g, ln_b):
    slot_tok, block_e, nused, slot_gate_b, dest = _slot_tables(eidx, gate)
    ys = _experts(h, slot_tok, block_e, nused, slot_gate_b, wg, wu, wd)
    return _combine(h, ys, dest, ln_g, ln_b)


def _retention_constants():
    log_gamma = jnp.log1p(-jnp.exp2(-5.0 - jnp.arange(RET_HEADS, dtype=F32)))
    idx = jnp.arange(CHUNK)
    rel = idx[:, None] - idx[None, :]
    decay_in = jnp.where(rel >= 0, jnp.exp(log_gamma[:, None, None] * jnp.maximum(rel, 0)), 0.0)
    xi = jnp.exp(log_gamma[:, None] * (idx + 1))
    zeta = jnp.exp(log_gamma[:, None] * (CHUNK - 1 - idx))
    g_chunk = jnp.exp(log_gamma * CHUNK)
    rc = jnp.zeros((CHUNK, LANES), F32)
    rc = rc.at[:, 0:RET_HEADS].set(xi.T)
    rc = rc.at[:, RET_HEADS:2 * RET_HEADS].set(zeta.T)
    rc = rc.at[:, 2 * RET_HEADS:3 * RET_HEADS].set(jnp.broadcast_to(g_chunk[None, :], (CHUNK, RET_HEADS)))
    return decay_in.astype(F32), rc


def kernel(x, w_in_even, i_bias, f_bias, conv_w, conv_b, mlstm_norm_g, ret_norm_g, w_out_even, w_in_odd,
           w_out_odd, router_w, router_bias, w_gate, w_up, w_down, ln_g, ln_b):
    batch, seq, _ = x.shape
    n = batch * seq
    h = x.reshape(n, D_MODEL)
    router_wt = router_w.T.astype(F32)
    router_b = router_bias.astype(F32).reshape(N_EXPERTS, 1)
    decay_in, ret_consts = _retention_constants()
    gate_col0 = 4 * MLSTM_W
    gate_col1 = gate_col0 + 2 * MLSTM_HEADS

    for layer in range(DEPTH):
        j = layer // 2
        lg = lambda s: ln_g[layer, s].reshape(1, D_MODEL)
        lb = lambda s: ln_b[layer, s].reshape(1, D_MODEL)
        if layer % 2 == 0:
            w = w_in_even[j]
            w_re = jnp.concatenate(
                [w[:, :gate_col0], w[:, gate_col1:], w[:, gate_col0:gate_col1],
                 jnp.zeros((D_MODEL, LANES - 2 * MLSTM_HEADS), w.dtype)], axis=1).astype(BF16)
            p = _project(h, w_re, EV_COLS // 3, "proj_even")
            gate_bias = jnp.zeros((1, LANES), F32)
            gate_bias = gate_bias.at[0, 0:MLSTM_HEADS].set(i_bias[j])
            gate_bias = gate_bias.at[0, MLSTM_HEADS:2 * MLSTM_HEADS].set(f_bias[j])
            norm_g = jnp.concatenate([mlstm_norm_g[j], ret_norm_g[j]]).reshape(1, MLSTM_W + RET_W)
            y = _even_mixer(p, batch, seq, conv_w[j], conv_b[j].reshape(1, 2 * MLSTM_W), gate_bias,
                            decay_in, ret_consts, norm_g.astype(F32))
            h, eidx, gate = _post_mixer(_post_even_kernel, h, [y], w_out_even[j].astype(BF16),
                                        lg(0), lb(0), router_wt, router_b, "post_even")
        else:
            p = _project(h, w_in_odd[j].astype(BF16), 3 * DIL_W, "proj_odd")
            outs, lses = [], []
            for g, (_, dilation) in enumerate(DIL_CONFIGS):
                o, l = _dilated_attention(p, batch, seq, g, dilation)
                outs.append(o)
                lses.append(l)
            h, eidx, gate = _post_mixer(_post_odd_kernel, h, outs + lses, w_out_odd[j].astype(BF16),
                                        lg(0), lb(0), router_wt, router_b, "post_odd")
        h = _moe_and_norm(h, eidx, gate, w_gate[layer].astype(BF16), w_up[layer].astype(BF16),
                          w_down[layer].astype(BF16), lg(1), lb(1))
    return h.reshape(batch, seq, D_MODEL)
```

```python
import functools

import jax
import jax.numpy as jnp
from jax import lax
from jax.experimental import pallas as pl
from jax.experimental.pallas import tpu as pltpu

F32 = jnp.float32
BF16 = jnp.bfloat16

D_MODEL = 1024
DEPTH = 2
ALPHA = (2 * DEPTH) ** 0.25
LN_EPS = 1e-5
MLSTM_HEADS = 4
RET_HEADS = 4
HEAD_DIM = 128
MLSTM_W = MLSTM_HEADS * HEAD_DIM
RET_W = RET_HEADS * HEAD_DIM
CHUNK = 128
CONV_WIDTH = 4
DIL_HEADS = 8
DIL_DH = 64
DIL_W = DIL_HEADS * DIL_DH
DIL_CONFIGS = ((128, 1), (512, 4), (2048, 16))
N_EXPERTS = 32
N_GROUPS = 4
EXPERTS_PER_GROUP = N_EXPERTS // N_GROUPS
TOP_K = 2
D_EXPERT = 256

LANES = 128
SUBLANES = 8
VMEM_LIMIT_BYTES = 56 * 1024 * 1024

PROJ_TM = 512
POST_TM = 512
MOE_BLK = 256
COMB_TM = 256
CONV_TAIL = SUBLANES

EV_QM, EV_KM, EV_VM, EV_OM = 0, MLSTM_W, 2 * MLSTM_W, 3 * MLSTM_W
EV_QR, EV_KR, EV_VR, EV_GR = (4 * MLSTM_W, 4 * MLSTM_W + RET_W, 4 * MLSTM_W + 2 * RET_W,
                              4 * MLSTM_W + 3 * RET_W)
EV_GATE = 4 * MLSTM_W + 4 * RET_W
EV_COLS = EV_GATE + LANES

NEG_INF = float("-inf")


def _cparams(sem):
    return pltpu.CompilerParams(dimension_semantics=sem, vmem_limit_bytes=VMEM_LIMIT_BYTES)


def _proj_kernel(x_ref, w_ref, o_ref):
    o_ref[...] = jnp.dot(x_ref[...].astype(BF16), w_ref[...],
                         preferred_element_type=F32).astype(o_ref.dtype)


def _project(x, w, tn, name):
    n, k = x.shape
    m = w.shape[1]
    return pl.pallas_call(
        _proj_kernel,
        grid=(n // PROJ_TM, m // tn),
        in_specs=[pl.BlockSpec((PROJ_TM, k), lambda i, j: (i, 0)),
                  pl.BlockSpec((k, tn), lambda i, j: (0, j))],
        out_specs=pl.BlockSpec((PROJ_TM, tn), lambda i, j: (i, j)),
        out_shape=jax.ShapeDtypeStruct((n, m), F32),
        compiler_params=_cparams(("parallel", "arbitrary")),
        name=name,
    )(x, w)


def _silu(x):
    return x * jax.nn.sigmoid(x)


def _head_norm(h):
    mu = jnp.mean(h, -1, keepdims=True)
    c = h - mu
    var = jnp.mean(c * c, -1, keepdims=True)
    return c * lax.rsqrt(var + LN_EPS)


def _dot_nt(a, b):
    return lax.dot_general(a, b, (((1,), (1,)), ((), ())), preferred_element_type=F32)


def _dot_tn(a_f32, b):
    return jnp.dot(a_f32.T.astype(BF16), b, preferred_element_type=F32)


def _even_mixer_kernel(p_ref, convw_ref, convb_ref, gbias_ref, dec_ref, rc_ref, ng_ref, y_ref,
                       qkbuf, c_ref, n_ref, m_ref, r_ref):
    L = CHUNK
    scale = HEAD_DIM ** -0.5

    @pl.when(pl.program_id(1) == 0)
    def _():
        qkbuf[0:CONV_TAIL, :] = jnp.zeros((CONV_TAIL, 2 * MLSTM_W), F32)
        c_ref[...] = jnp.zeros_like(c_ref)
        n_ref[...] = jnp.zeros_like(n_ref)
        m_ref[...] = jnp.zeros_like(m_ref)
        r_ref[...] = jnp.zeros_like(r_ref)

    qkbuf[CONV_TAIL:CONV_TAIL + L, :] = p_ref[:, EV_QM:EV_QM + 2 * MLSTM_W]
    acc = jnp.broadcast_to(convb_ref[...], (L, 2 * MLSTM_W))
    for k in range(CONV_WIDTH):
        off = CONV_TAIL - (CONV_WIDTH - 1) + k
        acc = acc + convw_ref[k:k + 1, :] * qkbuf[off:off + L, :]
    qk = _silu(acc)
    qkbuf[0:CONV_TAIL, :] = qkbuf[L:L + CONV_TAIL, :]

    pre = p_ref[:, EV_GATE:EV_GATE + LANES] + gbias_ref[...]
    logf = jnp.minimum(pre, 0.0) - jnp.log1p(jnp.exp(-jnp.abs(pre)))
    row = lax.broadcasted_iota(jnp.int32, (L, L), 0)
    col = lax.broadcasted_iota(jnp.int32, (L, L), 1)
    causal = row >= col
    bcs = jnp.dot(causal.astype(F32), logf, preferred_element_type=F32,
                  precision=lax.Precision.HIGHEST)
    u_t = (pre - pltpu.roll(bcs, LANES - MLSTM_HEADS, 1)).T

    for h in range(MLSTM_HEADS):
        lo = h * HEAD_DIM
        b_col = bcs[:, MLSTM_HEADS + h:MLSTM_HEADS + h + 1]
        i_col = pre[:, h:h + 1]
        r_row = u_t[h:h + 1, :]
        m_prev = m_ref[h:h + 1, 0:1]
        log_d = jnp.where(causal, b_col + r_row, NEG_INF)
        a = b_col + m_prev
        m_t = jnp.maximum(a, jnp.max(log_d, -1, keepdims=True))
        d = jnp.exp(log_d - m_t)
        inter = jnp.exp(a - m_t)
        q = qk[:, lo:lo + HEAD_DIM]
        k = qk[:, MLSTM_W + lo:MLSTM_W + lo + HEAD_DIM] * scale
        qb = q.astype(BF16)
        vb = p_ref[:, EV_VM + lo:EV_VM + lo + HEAD_DIM].astype(BF16)
        s = _dot_nt(qb, k.astype(BF16)) * d
        c_old = c_ref[h]
        n_old = n_ref[h:h + 1, :]
        num = jnp.dot(s.astype(BF16), vb, preferred_element_type=F32) + \
            inter * jnp.dot(qb, c_old.astype(BF16), preferred_element_type=F32)
        den = jnp.sum(s, -1, keepdims=True) + inter * jnp.sum(q * n_old, -1, keepdims=True)
        hh = num / jnp.maximum(jnp.abs(den), jnp.exp(-m_t))
        g = b_col[L - 1:L, :]
        log_w = g - b_col + i_col
        m_new = jnp.maximum(g + m_prev, jnp.max(log_w, 0, keepdims=True))
        kw = k * jnp.exp(log_w - m_new)
        decay = jnp.exp(g + m_prev - m_new)
        c_ref[h] = decay * c_old + _dot_tn(kw, vb)
        n_ref[h:h + 1, :] = decay * n_old + jnp.sum(kw, 0, keepdims=True)
        m_ref[h:h + 1, :] = jnp.broadcast_to(m_new, (1, LANES))
        o_gate = jax.nn.sigmoid(p_ref[:, EV_OM + lo:EV_OM + lo + HEAD_DIM])
        y_ref[:, lo:lo + HEAD_DIM] = (_head_norm(o_gate * hh) * ng_ref[:, lo:lo + HEAD_DIM]).astype(y_ref.dtype)

    for h in range(RET_HEADS):
        lo = h * HEAD_DIM
        qb = p_ref[:, EV_QR + lo:EV_QR + lo + HEAD_DIM].astype(BF16)
        k = p_ref[:, EV_KR + lo:EV_KR + lo + HEAD_DIM] * scale
        vb = p_ref[:, EV_VR + lo:EV_VR + lo + HEAD_DIM].astype(BF16)
        xi = rc_ref[:, h:h + 1]
        zeta = rc_ref[:, RET_HEADS + h:RET_HEADS + h + 1]
        g_chunk = rc_ref[0:1, 2 * RET_HEADS + h:2 * RET_HEADS + h + 1]
        r_old = r_ref[h]
        inner = _dot_nt(qb, k.astype(BF16)) * dec_ref[h]
        y = jnp.dot(inner.astype(BF16), vb, preferred_element_type=F32) + \
            xi * jnp.dot(qb, r_old.astype(BF16), preferred_element_type=F32)
        r_ref[h] = g_chunk * r_old + _dot_tn(k * zeta, vb)
        gr = p_ref[:, EV_GR + lo:EV_GR + lo + HEAD_DIM]
        out = _head_norm(y) * ng_ref[:, MLSTM_W + lo:MLSTM_W + lo + HEAD_DIM] * _silu(gr)
        y_ref[:, MLSTM_W + lo:MLSTM_W + lo + HEAD_DIM] = out.astype(y_ref.dtype)


def _even_mixer(p, batch, seq, conv_w, conv_b, gate_bias, decay_in, ret_consts, norm_g):
    nchunks = seq // CHUNK
    const2 = lambda b, c: (0, 0)
    return pl.pallas_call(
        _even_mixer_kernel,
        grid=(batch, nchunks),
        in_specs=[pl.BlockSpec((CHUNK, EV_COLS), lambda b, c: (b * nchunks + c, 0)),
                  pl.BlockSpec((CONV_WIDTH, 2 * MLSTM_W), const2),
                  pl.BlockSpec((1, 2 * MLSTM_W), const2),
                  pl.BlockSpec((1, LANES), const2),
                  pl.BlockSpec((RET_HEADS, CHUNK, CHUNK), lambda b, c: (0, 0, 0)),
                  pl.BlockSpec((CHUNK, LANES), const2),
                  pl.BlockSpec((1, MLSTM_W + RET_W), const2)],
        out_specs=pl.BlockSpec((CHUNK, MLSTM_W + RET_W), lambda b, c: (b * nchunks + c, 0)),
        out_shape=jax.ShapeDtypeStruct((batch * seq, MLSTM_W + RET_W), BF16),
        scratch_shapes=[pltpu.VMEM((CONV_TAIL + CHUNK, 2 * MLSTM_W), F32),
                        pltpu.VMEM((MLSTM_HEADS, HEAD_DIM, HEAD_DIM), F32),
                        pltpu.VMEM((SUBLANES, HEAD_DIM), F32),
                        pltpu.VMEM((SUBLANES, LANES), F32),
                        pltpu.VMEM((RET_HEADS, HEAD_DIM, HEAD_DIM), F32)],
        compiler_params=_cparams(("arbitrary", "arbitrary")),
        name="even_mixer",
    )(p, conv_w, conv_b, gate_bias, decay_in, ret_consts, norm_g)


TOK_TILES = D_MODEL // LANES


def _store_token_tiles(ref, val, rows):
    for c in range(TOK_TILES):
        ref[pl.ds(c, rows, stride=TOK_TILES), :] = val[:, c * LANES:(c + 1) * LANES]


def _load_token_tiles(ref, rows):
    return jnp.concatenate([ref[pl.ds(c, rows, stride=TOK_TILES), :] for c in range(TOK_TILES)], axis=1)


def _layer_norm(z, g, b):
    mu = jnp.mean(z, -1, keepdims=True)
    c = z - mu
    var = jnp.mean(c * c, -1, keepdims=True)
    return c * lax.rsqrt(var + LN_EPS) * g + b


def _route(h, rwt_ref, rb_ref):
    logits = lax.dot_general(rwt_ref[...], h, (((1,), (1,)), ((), ())),
                             preferred_element_type=F32, precision=lax.Precision.HIGHEST)
    aff = jax.nn.sigmoid(logits)
    sel = aff + rb_ref[...]
    tm = h.shape[0]
    sub = lax.broadcasted_iota(jnp.int32, (EXPERTS_PER_GROUP, tm), 0)
    best = None
    for g in range(N_GROUPS):
        lo = g * EXPERTS_PER_GROUP
        sg = sel[lo:lo + EXPERTS_PER_GROUP, :]
        ag = aff[lo:lo + EXPERTS_PER_GROUP, :]
        v1 = jnp.max(sg, 0, keepdims=True)
        i1 = jnp.min(jnp.where(sg == v1, sub, EXPERTS_PER_GROUP), 0, keepdims=True)
        rest = jnp.where(sub == i1, NEG_INF, sg)
        v2 = jnp.max(rest, 0, keepdims=True)
        i2 = jnp.min(jnp.where(rest == v2, sub, EXPERTS_PER_GROUP), 0, keepdims=True)
        a1 = jnp.sum(jnp.where(sub == i1, ag, 0.0), 0, keepdims=True)
        a2 = jnp.sum(jnp.where(sub == i2, ag, 0.0), 0, keepdims=True)
        cand = (v1 + v2, i1 + lo, i2 + lo, a1, a2)
        if best is None:
            best = cand
        else:
            take = cand[0] > best[0]
            best = tuple(jnp.where(take, c, b) for c, b in zip(cand, best))
    _, e1, e2, a1, a2 = best
    tot = a1 + a2
    return e1, e2, a1 / tot, a2 / tot


def _post_kernel(x_ref, y_ref, wo_ref, lng_ref, lnb_ref, rwt_ref, rb_ref, tri_ref,
                 h8_ref, eidx_ref, rank_ref, gcol_ref, cnt_ref, run_ref):
    tm = x_ref.shape[0]

    @pl.when(pl.program_id(0) == 0)
    def _():
        run_ref[...] = jnp.zeros_like(run_ref)

    mix = jnp.dot(y_ref[...], wo_ref[...], preferred_element_type=F32)
    h = _layer_norm(ALPHA * x_ref[...] + mix, lng_ref[...], lnb_ref[...])
    _store_token_tiles(h8_ref, h, tm)

    e1, e2, g1, g2 = _route(h, rwt_ref, rb_ref)
    eidx_ref[0:1, :] = e1
    eidx_ref[1:2, :] = e2
    gates = jnp.concatenate([g1, g2, jnp.zeros((LANES - TOP_K, tm), F32)], axis=0)
    gcol_ref[...] = gates.T

    sub = lax.broadcasted_iota(jnp.int32, (N_EXPERTS, tm), 0)
    oh1 = sub == e1
    oh2 = sub == e2
    chosen = jnp.logical_or(oh1, oh2).astype(F32)
    incl = jnp.dot(chosen.astype(BF16), tri_ref[...], preferred_element_type=F32)
    before = run_ref[:, 0:1] + incl - chosen
    rank_ref[0:1, :] = jnp.sum(jnp.where(oh1, before, 0.0), 0, keepdims=True).astype(jnp.int32)
    rank_ref[1:2, :] = jnp.sum(jnp.where(oh2, before, 0.0), 0, keepdims=True).astype(jnp.int32)
    run_new = run_ref[...] + incl[:, tm - 1:tm]
    run_ref[...] = run_new
    cnt_ref[...] = run_new.astype(jnp.int32)


def _post_mixer(x, y, w_out, ln_g, ln_b, router_wt, router_b, tri, name):
    n = x.shape[0]
    tm = POST_TM
    const2 = lambda i: (0, 0)
    return pl.pallas_call(
        _post_kernel,
        grid=(n // tm,),
        in_specs=[pl.BlockSpec((tm, D_MODEL), lambda i: (i, 0)),
                  pl.BlockSpec((tm, y.shape[1]), lambda i: (i, 0)),
                  pl.BlockSpec(w_out.shape, const2),
                  pl.BlockSpec((1, D_MODEL), const2),
                  pl.BlockSpec((1, D_MODEL), const2),
                  pl.BlockSpec((N_EXPERTS, D_MODEL), const2),
                  pl.BlockSpec((N_EXPERTS, 1), const2),
                  pl.BlockSpec((tm, tm), const2)],
        out_specs=[pl.BlockSpec((tm * TOK_TILES, LANES), lambda i: (i, 0)),
                   pl.BlockSpec((TOP_K, tm), lambda i: (0, i)),
                   pl.BlockSpec((TOP_K, tm), lambda i: (0, i)),
                   pl.BlockSpec((tm, LANES), lambda i: (i, 0)),
                   pl.BlockSpec((N_EXPERTS, LANES), const2)],
        out_shape=[jax.ShapeDtypeStruct((n * TOK_TILES, LANES), F32),
                   jax.ShapeDtypeStruct((TOP_K, n), jnp.int32),
                   jax.ShapeDtypeStruct((TOP_K, n), jnp.int32),
                   jax.ShapeDtypeStruct((n, LANES), F32),
                   jax.ShapeDtypeStruct((N_EXPERTS, LANES), jnp.int32)],
        scratch_shapes=[pltpu.VMEM((N_EXPERTS, LANES), F32)],
        compiler_params=_cparams(("arbitrary",)),
        name=name,
    )(x, y, w_out, ln_g, ln_b, router_wt, router_b, tri)


ATT_SPAN = max(w for w, _ in DIL_CONFIGS)
ATT_PAIR = 2
ATT_PW = ATT_PAIR * DIL_DH
N_DIL = len(DIL_CONFIGS)


def _attn_unit(q_ref, kbuf, vbuf, bias, g, start, stride, noprev, o_scr, m_scr, l_scr):
    L = CHUNK
    if stride == 1:
        start = pl.multiple_of(start, L)
        qrows = pl.ds(start, L)
        kvrows = pl.ds(start, 2 * L)
    else:
        qrows = pl.ds(start, L, stride=stride)
        kvrows = pl.ds(start, 2 * L, stride=stride)
    q2 = q_ref[qrows, :].astype(BF16)
    kb = kbuf[kvrows, :].astype(BF16)
    vb = vbuf[kvrows, :].astype(BF16)
    head0 = lax.broadcasted_iota(jnp.int32, (2 * L, ATT_PW), 1) < DIL_DH
    zero = jnp.zeros_like(kb)
    k2 = jnp.concatenate([jnp.where(head0, kb, zero), jnp.where(head0, zero, kb)], axis=0)
    v2 = jnp.concatenate([jnp.where(head0, vb, zero), jnp.where(head0, zero, vb)], axis=0)
    s = _dot_nt(q2, k2) + bias
    if noprev is not None:
        col = lax.broadcasted_iota(jnp.int32, (L, 4 * L), 1)
        prev_col = (col % (2 * L)) < L
        s = jnp.where(jnp.logical_and(prev_col, noprev), NEG_INF, s)
    s0, s1 = s[:, 0:2 * L], s[:, 2 * L:4 * L]
    m0 = jnp.max(s0, -1, keepdims=True)
    m1 = jnp.max(s1, -1, keepdims=True)
    p0 = jnp.exp(s0 - m0)
    p1 = jnp.exp(s1 - m1)
    l0 = jnp.sum(p0, -1, keepdims=True)
    l1 = jnp.sum(p1, -1, keepdims=True)
    o = jnp.dot(jnp.concatenate([p0, p1], axis=1).astype(BF16), v2, preferred_element_type=F32)
    left = lax.broadcasted_iota(jnp.int32, (L, ATT_PW), 1) < DIL_DH
    o_scr[g, qrows, :] = o
    m_scr[g, qrows, :] = jnp.where(left, m0, m1)
    l_scr[g, qrows, :] = jnp.where(left, l0, l1)


def _attn_kernel(q0_ref, q1_ref, q2_ref, kvc_ref, kvp0_ref, kvp1_ref, kvp2_ref, bias_ref, y_ref,
                 k0, k1, k2, v0, v1, v2, o_scr, m_scr, l_scr):
    first = pl.program_id(2) == 0
    q_refs = (q0_ref, q1_ref, q2_ref)
    kbufs = (k0, k1, k2)
    vbufs = (v0, v1, v2)
    prevs = (kvp0_ref, kvp1_ref, kvp2_ref)
    kvw = 2 * ATT_PW
    for g, (window, _) in enumerate(DIL_CONFIGS):
        kbufs[g][0:window, :] = prevs[g][:, 0:ATT_PW]
        vbufs[g][0:window, :] = prevs[g][:, ATT_PW:kvw]
        kbufs[g][window:window + ATT_SPAN, :] = kvc_ref[:, g * kvw:g * kvw + ATT_PW]
        vbufs[g][window:window + ATT_SPAN, :] = kvc_ref[:, g * kvw + ATT_PW:(g + 1) * kvw]

    for g, (window, dilation) in enumerate(DIL_CONFIGS):
        bias = bias_ref[0, g]
        nsub = ATT_SPAN // window

        def first_body(r, _, g=g, bias=bias, dilation=dilation):
            _attn_unit(q_refs[g], kbufs[g], vbufs[g], bias, g, r, dilation, first, o_scr, m_scr, l_scr)
            return 0
        lax.fori_loop(0, dilation, first_body, 0)

        if nsub > 1:
            def rest_body(idx, _, g=g, bias=bias, window=window, dilation=dilation):
                start = (1 + idx // dilation) * window + idx % dilation
                _attn_unit(q_refs[g], kbufs[g], vbufs[g], bias, g, start, dilation, None, o_scr, m_scr, l_scr)
                return 0
            lax.fori_loop(0, (nsub - 1) * dilation, rest_body, 0)

    rows = 256

    def merge(i, _):
        sl = pl.ds(pl.multiple_of(i * rows, rows), rows)
        ms = [m_scr[g, sl, :] for g in range(N_DIL)]
        mx = functools.reduce(jnp.maximum, ms)
        es = [jnp.exp(m - mx) for m in ms]
        num = sum(e * o_scr[g, sl, :] for g, e in enumerate(es))
        den = sum(e * l_scr[g, sl, :] for g, e in enumerate(es))
        y_ref[sl, :] = (num / den).astype(y_ref.dtype)
        return 0
    lax.fori_loop(0, ATT_SPAN // rows, merge, 0)


def _dilated_attention(q, kv, bias, batch, seq):
    n = q.shape[0]
    nspan = seq // ATT_SPAN
    npair = DIL_HEADS // ATT_PAIR
    kvw = 2 * ATT_PW

    def prev_spec(g, window):
        per_span = ATT_SPAN // window
        rows_per_batch = seq // window
        return pl.BlockSpec(
            (window, kvw),
            lambda b, p, s: (jnp.maximum(b * rows_per_batch + s * per_span - 1, 0), p * N_DIL + g))

    return pl.pallas_call(
        _attn_kernel,
        grid=(batch, npair, nspan),
        in_specs=[pl.BlockSpec((ATT_SPAN, ATT_PW), lambda b, p, s, g=g: (b * nspan + s, p * N_DIL + g))
                  for g in range(N_DIL)] +
                 [pl.BlockSpec((ATT_SPAN, N_DIL * kvw), lambda b, p, s: (b * nspan + s, p))] +
                 [prev_spec(g, w) for g, (w, _) in enumerate(DIL_CONFIGS)] +
                 [pl.BlockSpec((1, N_DIL, CHUNK, 4 * CHUNK), lambda b, p, s: (p, 0, 0, 0))],
        out_specs=pl.BlockSpec((ATT_SPAN, ATT_PW), lambda b, p, s: (b * nspan + s, p)),
        out_shape=jax.ShapeDtypeStruct((n, DIL_W), BF16),
        scratch_shapes=[pltpu.VMEM((w + ATT_SPAN, ATT_PW), F32) for w, _ in DIL_CONFIGS] * 2 +
                       [pltpu.VMEM((N_DIL, ATT_SPAN, ATT_PW), F32) for _ in range(3)],
        compiler_params=_cparams(("parallel", "parallel", "arbitrary")),
        name="dilated_attention",
    )(q, q, q, kv, kv, kv, kv, bias)


def _attention_bias():
    L = CHUNK
    qi = jnp.arange(L)[:, None]
    kr = jnp.arange(2 * L)[None, :]
    steps = qi + L - kr
    band = (steps >= 0) & (steps <= L)
    slopes = jnp.exp2(-8.0 * jnp.arange(1, DIL_HEADS + 1, dtype=F32) / DIL_HEADS)
    per_group = []
    for _, dilation in DIL_CONFIGS:
        b = -slopes[:, None, None] * (steps * dilation).astype(F32)[None]
        b = jnp.where(band[None], b, NEG_INF)
        per_group.append(b.reshape(DIL_HEADS // ATT_PAIR, ATT_PAIR, L, 2 * L)
                         .transpose(0, 2, 1, 3).reshape(DIL_HEADS // ATT_PAIR, L, 4 * L))
    return jnp.stack(per_group, axis=1)


def _odd_weights(w):
    w = w.reshape(D_MODEL, N_DIL, 3, DIL_HEADS // ATT_PAIR, ATT_PAIR, DIL_DH)
    wq = (w[:, :, 0] * (DIL_DH ** -0.5)).transpose(0, 2, 1, 3, 4).reshape(D_MODEL, N_DIL * DIL_W)
    wkv = w[:, :, 1:3].transpose(0, 3, 1, 2, 4, 5).reshape(D_MODEL, 2 * N_DIL * DIL_W)
    return wq.astype(BF16), wkv.astype(BF16)


def _tile_copy(src_hbm, row, dst, i, sem):
    src = src_hbm.at[pl.ds(pl.multiple_of(row * TOK_TILES, TOK_TILES), TOK_TILES), :]
    return pltpu.make_async_copy(src, dst.at[pl.ds(i * TOK_TILES, TOK_TILES), :], sem)


def _experts_kernel(eidx_ref, rank_ref, pstart_ref, count_ref, block_e_ref, nused_ref,
                    h8_hbm, wg_ref, wu_ref, wd_ref, ys8_ref, slot_tok, xbuf, sems):
    j = pl.program_id(0)
    nused = nused_ref[0]
    n_tok = eidx_ref.shape[0] // TOP_K

    @pl.when(j == 0)
    def _():
        def fill(t, _):
            for k in range(TOP_K):
                a = k * n_tok + t
                slot_tok[pstart_ref[eidx_ref[a]] + rank_ref[a]] = t
            return 0
        lax.fori_loop(0, n_tok, fill, 0, unroll=8)

        def pad_expert(e, _):
            cnt = count_ref[e]
            padded = (cnt + MOE_BLK - 1) // MOE_BLK * MOE_BLK

            def pad(i, _):
                slot_tok[pstart_ref[e] + i] = 0
                return 0
            lax.fori_loop(cnt, padded, pad, 0)
            return 0
        lax.fori_loop(0, N_EXPERTS, pad_expert, 0)

    def issue(blk, slot):
        for i in range(MOE_BLK):
            _tile_copy(h8_hbm, slot_tok[blk * MOE_BLK + i], xbuf.at[slot], i, sems.at[slot]).start()

    def wait(slot):
        for i in range(MOE_BLK):
            _tile_copy(h8_hbm, 0, xbuf.at[slot], i, sems.at[slot]).wait()

    slot = j % 2

    @pl.when(jnp.logical_and(j == 0, nused > 0))
    def _():
        issue(0, 0)

    @pl.when(j < nused)
    def _():
        issue(jnp.minimum(j + 1, nused - 1), 1 - slot)
        wait(slot)
        x = _load_token_tiles(xbuf.at[slot], MOE_BLK).astype(BF16)
        a = jnp.dot(x, wg_ref[0], preferred_element_type=F32)
        u = jnp.dot(x, wu_ref[0], preferred_element_type=F32)
        hm = (_silu(a) * u).astype(BF16)
        y = jnp.dot(hm, wd_ref[0], preferred_element_type=F32)
        _store_token_tiles(ys8_ref, y, MOE_BLK)

    @pl.when(j == nused - 1)
    def _():
        wait(1 - slot)

    @pl.when(j >= nused)
    def _():
        ys8_ref[...] = jnp.zeros_like(ys8_ref)


def _experts(h8, eidx, rank, pstart, counts, block_e, nused, wg, wu, wd, p_slots):
    n_blocks = p_slots // MOE_BLK
    wmap = lambda j, ei, rk, ps, ct, be, nu: (be[j], 0, 0)
    grid_spec = pltpu.PrefetchScalarGridSpec(
        num_scalar_prefetch=6,
        grid=(n_blocks,),
        in_specs=[pl.BlockSpec(memory_space=pl.ANY),
                  pl.BlockSpec((1, D_MODEL, D_EXPERT), wmap),
                  pl.BlockSpec((1, D_MODEL, D_EXPERT), wmap),
                  pl.BlockSpec((1, D_EXPERT, D_MODEL), wmap)],
        out_specs=pl.BlockSpec((MOE_BLK * TOK_TILES, LANES), lambda j, *_: (j, 0)),
        scratch_shapes=[pltpu.SMEM((p_slots,), jnp.int32),
                        pltpu.VMEM((2, MOE_BLK * TOK_TILES, LANES), F32),
                        pltpu.SemaphoreType.DMA((2,))],
    )
    return pl.pallas_call(
        _experts_kernel,
        grid_spec=grid_spec,
        out_shape=jax.ShapeDtypeStruct((p_slots * TOK_TILES, LANES), F32),
        compiler_params=_cparams(("arbitrary",)),
        name="moe_experts",
    )(eidx, rank, pstart, counts, block_e, nused, h8, wg, wu, wd)


def _combine_kernel(eidx_ref, rank_ref, pstart_ref, h8_ref, gcol_ref, ys8_hbm, lng_ref, lnb_ref,
                    o_ref, buf, sems):
    i = pl.program_id(0)
    nt = pl.num_programs(0)
    n_tok = nt * COMB_TM

    def issue(tile, slot):
        for t in range(COMB_TM):
            for k in range(TOP_K):
                a = k * n_tok + tile * COMB_TM + t
                row = pstart_ref[eidx_ref[a]] + rank_ref[a]
                _tile_copy(ys8_hbm, row, buf.at[slot, k], t, sems.at[slot]).start()

    def wait(slot):
        for t in range(COMB_TM):
            for k in range(TOP_K):
                _tile_copy(ys8_hbm, 0, buf.at[slot, k], t, sems.at[slot]).wait()

    slot = i % 2

    @pl.when(i == 0)
    def _():
        issue(0, 0)

    issue(jnp.minimum(i + 1, nt - 1), 1 - slot)
    wait(slot)
    g = gcol_ref[...]
    ffn = g[:, 0:1] * _load_token_tiles(buf.at[slot, 0], COMB_TM) + \
        g[:, 1:2] * _load_token_tiles(buf.at[slot, 1], COMB_TM)
    z = ALPHA * _load_token_tiles(h8_ref, COMB_TM) + ffn
    o_ref[...] = _layer_norm(z, lng_ref[...], lnb_ref[...])

    @pl.when(i == nt - 1)
    def _():
        wait(1 - slot)


def _combine(h8, gcol, ys8, eidx, rank, pstart, ln_g, ln_b):
    n = gcol.shape[0]
    const2 = lambda i, *_: (0, 0)
    grid_spec = pltpu.PrefetchScalarGridSpec(
        num_scalar_prefetch=3,
        grid=(n // COMB_TM,),
        in_specs=[pl.BlockSpec((COMB_TM * TOK_TILES, LANES), lambda i, *_: (i, 0)),
                  pl.BlockSpec((COMB_TM, LANES), lambda i, *_: (i, 0)),
                  pl.BlockSpec(memory_space=pl.ANY),
                  pl.BlockSpec((1, D_MODEL), const2),
                  pl.BlockSpec((1, D_MODEL), const2)],
        out_specs=pl.BlockSpec((COMB_TM, D_MODEL), lambda i, *_: (i, 0)),
        scratch_shapes=[pltpu.VMEM((2, TOP_K, COMB_TM * TOK_TILES, LANES), F32),
                        pltpu.SemaphoreType.DMA((2,))],
    )
    return pl.pallas_call(
        _combine_kernel,
        grid_spec=grid_spec,
        out_shape=jax.ShapeDtypeStruct((n, D_MODEL), F32),
        compiler_params=_cparams(("arbitrary",)),
        name="moe_combine",
    )(eidx, rank, pstart, h8, gcol, ys8, ln_g, ln_b)


def _moe_and_norm(h8, eidx, rank, gcol, counts, wg, wu, wd, ln_g, ln_b):
    n = gcol.shape[0]
    p_slots = n * TOP_K + N_EXPERTS * MOE_BLK
    n_blocks = p_slots // MOE_BLK
    cnt = counts[:, 0]
    padded = (cnt + MOE_BLK - 1) // MOE_BLK * MOE_BLK
    pend = jnp.cumsum(padded)
    pstart = (pend - padded).astype(jnp.int32)
    block_start = jnp.arange(n_blocks, dtype=jnp.int32) * MOE_BLK
    block_e = jnp.minimum(jnp.sum(pend[None, :] <= block_start[:, None], axis=1), N_EXPERTS - 1).astype(jnp.int32)
    nused = (pend[-1:] // MOE_BLK).astype(jnp.int32)
    eflat, rflat = eidx.reshape(-1), rank.reshape(-1)
    ys8 = _experts(h8, eflat, rflat, pstart, cnt, block_e, nused, wg, wu, wd, p_slots)
    return _combine(h8, gcol, ys8, eflat, rflat, pstart, ln_g, ln_b)


def _retention_constants():
    log_gamma = jnp.log1p(-jnp.exp2(-5.0 - jnp.arange(RET_HEADS, dtype=F32)))
    idx = jnp.arange(CHUNK)
    rel = idx[:, None] - idx[None, :]
    decay_in = jnp.where(rel >= 0, jnp.exp(log_gamma[:, None, None] * jnp.maximum(rel, 0)), 0.0)
    xi = jnp.exp(log_gamma[:, None] * (idx + 1))
    zeta = jnp.exp(log_gamma[:, None] * (CHUNK - 1 - idx))
    g_chunk = jnp.exp(log_gamma * CHUNK)
    rc = jnp.zeros((CHUNK, LANES), F32)
    rc = rc.at[:, 0:RET_HEADS].set(xi.T)
    rc = rc.at[:, RET_HEADS:2 * RET_HEADS].set(zeta.T)
    rc = rc.at[:, 2 * RET_HEADS:3 * RET_HEADS].set(jnp.broadcast_to(g_chunk[None, :], (CHUNK, RET_HEADS)))
    return decay_in.astype(F32), rc


def kernel(x, w_in_even, i_bias, f_bias, conv_w, conv_b, mlstm_norm_g, ret_norm_g, w_out_even, w_in_odd,
           w_out_odd, router_w, router_bias, w_gate, w_up, w_down, ln_g, ln_b):
    batch, seq, _ = x.shape
    n = batch * seq
    assert all(w // d == CHUNK for w, d in DIL_CONFIGS) and seq % ATT_SPAN == 0
    h = x.reshape(n, D_MODEL)
    router_wt = router_w.T.astype(F32)
    router_b = router_bias.astype(F32).reshape(N_EXPERTS, 1)
    decay_in, ret_consts = _retention_constants()
    tri = (jnp.arange(POST_TM)[:, None] <= jnp.arange(POST_TM)[None, :]).astype(BF16)
    att_bias = _attention_bias()
    gate_col0 = 4 * MLSTM_W
    gate_col1 = gate_col0 + 2 * MLSTM_HEADS

    for layer in range(DEPTH):
        j = layer // 2
        lg = lambda s: ln_g[layer, s].reshape(1, D_MODEL)
        lb = lambda s: ln_b[layer, s].reshape(1, D_MODEL)
        if layer % 2 == 0:
            w = w_in_even[j]
            w_re = jnp.concatenate(
                [w[:, :gate_col0], w[:, gate_col1:], w[:, gate_col0:gate_col1],
                 jnp.zeros((D_MODEL, LANES - 2 * MLSTM_HEADS), w.dtype)], axis=1).astype(BF16)
            p = _project(h, w_re, EV_COLS // 3, "proj_even")
            gate_bias = jnp.zeros((1, LANES), F32)
            gate_bias = gate_bias.at[0, 0:MLSTM_HEADS].set(i_bias[j])
            gate_bias = gate_bias.at[0, MLSTM_HEADS:2 * MLSTM_HEADS].set(f_bias[j])
            norm_g = jnp.concatenate([mlstm_norm_g[j], ret_norm_g[j]]).reshape(1, MLSTM_W + RET_W)
            y = _even_mixer(p, batch, seq, conv_w[j], conv_b[j].reshape(1, 2 * MLSTM_W), gate_bias,
                            decay_in, ret_consts, norm_g.astype(F32))
            w_out = w_out_even[j]
        else:
            wq, wkv = _odd_weights(w_in_odd[j])
            q = _project(h, wq, N_DIL * DIL_W, "proj_odd_q")
            kv = _project(h, wkv, N_DIL * DIL_W, "proj_odd_kv")
            y = _dilated_attention(q, kv, att_bias, batch, seq)
            w_out = w_out_odd[j]
        h8, eidx, rank, gcol, counts = _post_mixer(h, y, w_out.astype(BF16), lg(0), lb(0),
                                                   router_wt, router_b, tri, f"post_{layer}")
        h = _moe_and_norm(h8, eidx, rank, gcol, counts, w_gate[layer].astype(BF16),
                          w_up[layer].astype(BF16), w_down[layer].astype(BF16), lg(1), lb(1))
    return h.reshape(batch, seq, D_MODEL)
```

```python
import functools

import jax
import jax.numpy as jnp
from jax import lax
from jax.experimental import pallas as pl
from jax.experimental.pallas import tpu as pltpu

F32 = jnp.float32
BF16 = jnp.bfloat16

D_MODEL = 1024
DEPTH = 2
ALPHA = (2 * DEPTH) ** 0.25
LN_EPS = 1e-5
MLSTM_HEADS = 4
RET_HEADS = 4
HEAD_DIM = 128
MLSTM_W = MLSTM_HEADS * HEAD_DIM
RET_W = RET_HEADS * HEAD_DIM
CHUNK = 128
CONV_WIDTH = 4
DIL_HEADS = 8
DIL_DH = 64
DIL_W = DIL_HEADS * DIL_DH
DIL_CONFIGS = ((128, 1), (512, 4), (2048, 16))
N_EXPERTS = 32
N_GROUPS = 4
EXPERTS_PER_GROUP = N_EXPERTS // N_GROUPS
TOP_K = 2
D_EXPERT = 256

LANES = 128
SUBLANES = 8
VMEM_LIMIT_BYTES = 56 * 1024 * 1024

PROJ_TM = 512
POST_TM = 512
MOE_BLK = 256
MOE_NBUF = 3
COMB_TM = 256
CONV_TAIL = SUBLANES

EV_QM, EV_KM, EV_VM, EV_OM = 0, MLSTM_W, 2 * MLSTM_W, 3 * MLSTM_W
EV_QR, EV_KR, EV_VR, EV_GR = (4 * MLSTM_W, 4 * MLSTM_W + RET_W, 4 * MLSTM_W + 2 * RET_W,
                              4 * MLSTM_W + 3 * RET_W)
EV_GATE = 4 * MLSTM_W + 4 * RET_W
EV_COLS = EV_GATE + LANES

NEG_INF = float("-inf")


def _cparams(sem):
    return pltpu.CompilerParams(dimension_semantics=sem, vmem_limit_bytes=VMEM_LIMIT_BYTES)


def _proj_kernel(x_ref, w_ref, o_ref):
    o_ref[...] = jnp.dot(x_ref[...].astype(BF16), w_ref[...],
                         preferred_element_type=F32).astype(o_ref.dtype)


def _project(x, w, tn, name):
    n, k = x.shape
    m = w.shape[1]
    return pl.pallas_call(
        _proj_kernel,
        grid=(n // PROJ_TM, m // tn),
        in_specs=[pl.BlockSpec((PROJ_TM, k), lambda i, j: (i, 0)),
                  pl.BlockSpec((k, tn), lambda i, j: (0, j))],
        out_specs=pl.BlockSpec((PROJ_TM, tn), lambda i, j: (i, j)),
        out_shape=jax.ShapeDtypeStruct((n, m), F32),
        compiler_params=_cparams(("parallel", "arbitrary")),
        name=name,
    )(x, w)


EV_STEP_CHUNKS = 2
EV_PROJ_PIECE = 2 * LANES


def _silu(x):
    return x * jax.nn.sigmoid(x)


def _head_norm(h):
    mu = jnp.mean(h, -1, keepdims=True)
    c = h - mu
    var = jnp.mean(c * c, -1, keepdims=True)
    return c * lax.rsqrt(var + LN_EPS)


def _dot_nt(a, b):
    return lax.dot_general(a, b, (((1,), (1,)), ((), ())), preferred_element_type=F32)


def _dot_tn(a_f32, b):
    return jnp.dot(a_f32.T.astype(BF16), b, preferred_element_type=F32)


def _even_kernel(x0_ref, x_ref, w_ref, convw_ref, convb_ref, gbias_ref, dec_ref, rc_ref, ng_ref, y_ref,
                 p_scr, xb_scr, qkbuf, c_ref, n_ref, m_ref, r_ref, *, steps_per_batch):
    i = pl.program_id(0)

    @pl.when(i == 0)
    def _():
        p_scr[0] = jnp.dot(x0_ref[...].astype(BF16), w_ref[...], preferred_element_type=F32)

    @pl.when(i % steps_per_batch == 0)
    def _():
        qkbuf[0:CONV_TAIL, :] = jnp.zeros((CONV_TAIL, 2 * MLSTM_W), F32)
        c_ref[...] = jnp.zeros_like(c_ref)
        n_ref[...] = jnp.zeros_like(n_ref)
        m_ref[...] = jnp.zeros_like(m_ref)
        r_ref[...] = jnp.zeros_like(r_ref)

    xb_scr[...] = x_ref[...].astype(BF16)
    nxt = (i + 1) % 2
    pieces = iter(range(0, EV_COLS, EV_PROJ_PIECE))

    def project_piece():
        lo = next(pieces, None)
        if lo is not None:
            hi = min(lo + EV_PROJ_PIECE, EV_COLS)
            p_scr[nxt, :, lo:hi] = jnp.dot(xb_scr[...], w_ref[:, lo:hi], preferred_element_type=F32)

    cur = p_scr.at[i % 2]
    for c in range(EV_STEP_CHUNKS):
        rows = pl.ds(c * CHUNK, CHUNK)
        _mixer_chunk(cur.at[rows], y_ref.at[rows], convw_ref, convb_ref, gbias_ref, dec_ref, rc_ref, ng_ref,
                     qkbuf, c_ref, n_ref, m_ref, r_ref, project_piece)
    for _ in pieces:
        raise AssertionError("projection pieces left over: EV_PROJ_PIECE too small for the number of heads")


def _mixer_chunk(p_ref, y_ref, convw_ref, convb_ref, gbias_ref, dec_ref, rc_ref, ng_ref,
                 qkbuf, c_ref, n_ref, m_ref, r_ref, between):
    L = CHUNK
    scale = HEAD_DIM ** -0.5
    between()

    qkbuf[CONV_TAIL:CONV_TAIL + L, :] = p_ref[:, EV_QM:EV_QM + 2 * MLSTM_W]
    acc = jnp.broadcast_to(convb_ref[...], (L, 2 * MLSTM_W))
    for k in range(CONV_WIDTH):
        off = CONV_TAIL - (CONV_WIDTH - 1) + k
        acc = acc + convw_ref[k:k + 1, :] * qkbuf[off:off + L, :]
    qk = _silu(acc)
    qkbuf[0:CONV_TAIL, :] = qkbuf[L:L + CONV_TAIL, :]

    pre = p_ref[:, EV_GATE:EV_GATE + LANES] + gbias_ref[...]
    logf = jnp.minimum(pre, 0.0) - jnp.log1p(jnp.exp(-jnp.abs(pre)))
    row = lax.broadcasted_iota(jnp.int32, (L, L), 0)
    col = lax.broadcasted_iota(jnp.int32, (L, L), 1)
    causal = row >= col
    bcs = jnp.dot(causal.astype(F32), logf, preferred_element_type=F32,
                  precision=lax.Precision.HIGHEST)
    u_t = (pre - pltpu.roll(bcs, LANES - MLSTM_HEADS, 1)).T

    for h in range(MLSTM_HEADS):
        between()
        lo = h * HEAD_DIM
        b_col = bcs[:, MLSTM_HEADS + h:MLSTM_HEADS + h + 1]
        i_col = pre[:, h:h + 1]
        r_row = u_t[h:h + 1, :]
        m_prev = m_ref[h:h + 1, 0:1]
        log_d = jnp.where(causal, b_col + r_row, NEG_INF)
        a = b_col + m_prev
        m_t = jnp.maximum(a, jnp.max(log_d, -1, keepdims=True))
        d = jnp.exp(log_d - m_t)
        inter = jnp.exp(a - m_t)
        q = qk[:, lo:lo + HEAD_DIM]
        k = qk[:, MLSTM_W + lo:MLSTM_W + lo + HEAD_DIM] * scale
        qb = q.astype(BF16)
        vb = p_ref[:, EV_VM + lo:EV_VM + lo + HEAD_DIM].astype(BF16)
        s = _dot_nt(qb, k.astype(BF16)) * d
        c_old = c_ref[h]
        n_old = n_ref[h:h + 1, :]
        num = jnp.dot(s.astype(BF16), vb, preferred_element_type=F32) + \
            inter * jnp.dot(qb, c_old.astype(BF16), preferred_element_type=F32)
        den = jnp.sum(s, -1, keepdims=True) + inter * jnp.sum(q * n_old, -1, keepdims=True)
        hh = num / jnp.maximum(jnp.abs(den), jnp.exp(-m_t))
        g = b_col[L - 1:L, :]
        log_w = g - b_col + i_col
        m_new = jnp.maximum(g + m_prev, jnp.max(log_w, 0, keepdims=True))
        kw = k * jnp.exp(log_w - m_new)
        decay = jnp.exp(g + m_prev - m_new)
        c_ref[h] = decay * c_old + _dot_tn(kw, vb)
        n_ref[h:h + 1, :] = decay * n_old + jnp.sum(kw, 0, keepdims=True)
        m_ref[h:h + 1, :] = jnp.broadcast_to(m_new, (1, LANES))
        o_gate = jax.nn.sigmoid(p_ref[:, EV_OM + lo:EV_OM + lo + HEAD_DIM])
        y_ref[:, lo:lo + HEAD_DIM] = (_head_norm(o_gate * hh) * ng_ref[:, lo:lo + HEAD_DIM]).astype(y_ref.dtype)

    for h in range(RET_HEADS):
        between()
        lo = h * HEAD_DIM
        qb = p_ref[:, EV_QR + lo:EV_QR + lo + HEAD_DIM].astype(BF16)
        k = p_ref[:, EV_KR + lo:EV_KR + lo + HEAD_DIM] * scale
        vb = p_ref[:, EV_VR + lo:EV_VR + lo + HEAD_DIM].astype(BF16)
        xi = rc_ref[:, h:h + 1]
        zeta = rc_ref[:, RET_HEADS + h:RET_HEADS + h + 1]
        g_chunk = rc_ref[0:1, 2 * RET_HEADS + h:2 * RET_HEADS + h + 1]
        r_old = r_ref[h]
        inner = _dot_nt(qb, k.astype(BF16)) * dec_ref[h]
        y = jnp.dot(inner.astype(BF16), vb, preferred_element_type=F32) + \
            xi * jnp.dot(qb, r_old.astype(BF16), preferred_element_type=F32)
        r_ref[h] = g_chunk * r_old + _dot_tn(k * zeta, vb)
        gr = p_ref[:, EV_GR + lo:EV_GR + lo + HEAD_DIM]
        out = _head_norm(y) * ng_ref[:, MLSTM_W + lo:MLSTM_W + lo + HEAD_DIM] * _silu(gr)
        y_ref[:, MLSTM_W + lo:MLSTM_W + lo + HEAD_DIM] = out.astype(y_ref.dtype)


def _even_mixer(x, w, batch, seq, conv_w, conv_b, gate_bias, decay_in, ret_consts, norm_g):
    tm = EV_STEP_CHUNKS * CHUNK
    steps_per_batch = seq // tm
    nsteps = batch * steps_per_batch
    const2 = lambda i: (0, 0)
    return pl.pallas_call(
        functools.partial(_even_kernel, steps_per_batch=steps_per_batch),
        grid=(nsteps,),
        in_specs=[pl.BlockSpec((tm, D_MODEL), const2),
                  pl.BlockSpec((tm, D_MODEL), lambda i: (jnp.minimum(i + 1, nsteps - 1), 0)),
                  pl.BlockSpec((D_MODEL, EV_COLS), const2),
                  pl.BlockSpec((CONV_WIDTH, 2 * MLSTM_W), const2),
                  pl.BlockSpec((1, 2 * MLSTM_W), const2),
                  pl.BlockSpec((1, LANES), const2),
                  pl.BlockSpec((RET_HEADS, CHUNK, CHUNK), lambda i: (0, 0, 0)),
                  pl.BlockSpec((CHUNK, LANES), const2),
                  pl.BlockSpec((1, MLSTM_W + RET_W), const2)],
        out_specs=pl.BlockSpec((tm, MLSTM_W + RET_W), lambda i: (i, 0)),
        out_shape=jax.ShapeDtypeStruct((batch * seq, MLSTM_W + RET_W), BF16),
        scratch_shapes=[pltpu.VMEM((2, tm, EV_COLS), F32),
                        pltpu.VMEM((tm, D_MODEL), BF16),
                        pltpu.VMEM((CONV_TAIL + CHUNK, 2 * MLSTM_W), F32),
                        pltpu.VMEM((MLSTM_HEADS, HEAD_DIM, HEAD_DIM), F32),
                        pltpu.VMEM((SUBLANES, HEAD_DIM), F32),
                        pltpu.VMEM((SUBLANES, LANES), F32),
                        pltpu.VMEM((RET_HEADS, HEAD_DIM, HEAD_DIM), F32)],
        compiler_params=_cparams(("arbitrary",)),
        name="even_mixer",
    )(x, x, w, conv_w, conv_b, gate_bias, decay_in, ret_consts, norm_g)


TOK_TILES = D_MODEL // LANES


def _store_token_tiles(ref, val, rows):
    for c in range(TOK_TILES):
        ref[pl.ds(c, rows, stride=TOK_TILES), :] = val[:, c * LANES:(c + 1) * LANES]


def _load_token_tiles(ref, rows):
    return jnp.concatenate([ref[pl.ds(c, rows, stride=TOK_TILES), :] for c in range(TOK_TILES)], axis=1)


def _layer_norm(z, g, b):
    mu = jnp.mean(z, -1, keepdims=True)
    c = z - mu
    var = jnp.mean(c * c, -1, keepdims=True)
    return c * lax.rsqrt(var + LN_EPS) * g + b


def _route(h, rwt_ref, rb_ref):
    logits = lax.dot_general(rwt_ref[...], h, (((1,), (1,)), ((), ())),
                             preferred_element_type=F32, precision=lax.Precision.HIGHEST)
    aff = jax.nn.sigmoid(logits)
    sel = aff + rb_ref[...]
    tm = h.shape[0]
    sub = lax.broadcasted_iota(jnp.int32, (EXPERTS_PER_GROUP, tm), 0)
    best = None
    for g in range(N_GROUPS):
        lo = g * EXPERTS_PER_GROUP
        sg = sel[lo:lo + EXPERTS_PER_GROUP, :]
        ag = aff[lo:lo + EXPERTS_PER_GROUP, :]
        v1 = jnp.max(sg, 0, keepdims=True)
        i1 = jnp.min(jnp.where(sg == v1, sub, EXPERTS_PER_GROUP), 0, keepdims=True)
        rest = jnp.where(sub == i1, NEG_INF, sg)
        v2 = jnp.max(rest, 0, keepdims=True)
        i2 = jnp.min(jnp.where(rest == v2, sub, EXPERTS_PER_GROUP), 0, keepdims=True)
        a1 = jnp.sum(jnp.where(sub == i1, ag, 0.0), 0, keepdims=True)
        a2 = jnp.sum(jnp.where(sub == i2, ag, 0.0), 0, keepdims=True)
        cand = (v1 + v2, i1 + lo, i2 + lo, a1, a2)
        if best is None:
            best = cand
        else:
            take = cand[0] > best[0]
            best = tuple(jnp.where(take, c, b) for c, b in zip(cand, best))
    _, e1, e2, a1, a2 = best
    tot = a1 + a2
    return e1, e2, a1 / tot, a2 / tot


def _post_kernel(x_ref, y_ref, wo_ref, lng_ref, lnb_ref, rwt_ref, rb_ref, tri_ref,
                 h8_ref, eidx_ref, rank_ref, gcol_ref, cnt_ref, run_ref, wo_scr):
    tm = x_ref.shape[0]

    @pl.when(pl.program_id(0) == 0)
    def _():
        run_ref[...] = jnp.zeros_like(run_ref)
        wo_scr[...] = wo_ref[0].astype(BF16)

    mix = jnp.dot(y_ref[...], wo_scr[...], preferred_element_type=F32)
    h = _layer_norm(ALPHA * x_ref[...] + mix, lng_ref[...], lnb_ref[...])
    _store_token_tiles(h8_ref, h, tm)

    e1, e2, g1, g2 = _route(h, rwt_ref, rb_ref)
    eidx_ref[0:1, :] = e1
    eidx_ref[1:2, :] = e2
    gates = jnp.concatenate([g1, g2, jnp.zeros((LANES - TOP_K, tm), F32)], axis=0)
    gcol_ref[...] = gates.T

    sub = lax.broadcasted_iota(jnp.int32, (N_EXPERTS, tm), 0)
    oh1 = sub == e1
    oh2 = sub == e2
    chosen = jnp.logical_or(oh1, oh2).astype(F32)
    incl = jnp.dot(chosen.astype(BF16), tri_ref[...], preferred_element_type=F32)
    before = run_ref[:, 0:1] + incl - chosen
    rank_ref[0:1, :] = jnp.sum(jnp.where(oh1, before, 0.0), 0, keepdims=True).astype(jnp.int32)
    rank_ref[1:2, :] = jnp.sum(jnp.where(oh2, before, 0.0), 0, keepdims=True).astype(jnp.int32)
    run_new = run_ref[...] + incl[:, tm - 1:tm]
    run_ref[...] = run_new
    cnt_ref[...] = run_new.astype(jnp.int32)


def _post_mixer(x, y, w_out_all, j, ln_g, ln_b, router_wt, router_b, tri, name):
    n = x.shape[0]
    tm = POST_TM
    kdim = w_out_all.shape[1]
    const2 = lambda i: (0, 0)
    return pl.pallas_call(
        _post_kernel,
        grid=(n // tm,),
        in_specs=[pl.BlockSpec((tm, D_MODEL), lambda i: (i, 0)),
                  pl.BlockSpec((tm, y.shape[1]), lambda i: (i, 0)),
                  pl.BlockSpec((1, kdim, D_MODEL), lambda i: (j, 0, 0)),
                  pl.BlockSpec((1, D_MODEL), const2),
                  pl.BlockSpec((1, D_MODEL), const2),
                  pl.BlockSpec((N_EXPERTS, D_MODEL), const2),
                  pl.BlockSpec((N_EXPERTS, 1), const2),
                  pl.BlockSpec((tm, tm), const2)],
        out_specs=[pl.BlockSpec((tm * TOK_TILES, LANES), lambda i: (i, 0)),
                   pl.BlockSpec((TOP_K, tm), lambda i: (0, i)),
                   pl.BlockSpec((TOP_K, tm), lambda i: (0, i)),
                   pl.BlockSpec((tm, LANES), lambda i: (i, 0)),
                   pl.BlockSpec((N_EXPERTS, LANES), const2)],
        out_shape=[jax.ShapeDtypeStruct((n * TOK_TILES, LANES), F32),
                   jax.ShapeDtypeStruct((TOP_K, n), jnp.int32),
                   jax.ShapeDtypeStruct((TOP_K, n), jnp.int32),
                   jax.ShapeDtypeStruct((n, LANES), F32),
                   jax.ShapeDtypeStruct((N_EXPERTS, LANES), jnp.int32)],
        scratch_shapes=[pltpu.VMEM((N_EXPERTS, LANES), F32),
                        pltpu.VMEM((kdim, D_MODEL), BF16)],
        compiler_params=_cparams(("arbitrary",)),
        name=name,
    )(x, y, w_out_all, ln_g, ln_b, router_wt, router_b, tri)


ATT_SPAN = max(w for w, _ in DIL_CONFIGS)
ATT_PAIR = 2
ATT_PW = ATT_PAIR * DIL_DH
N_DIL = len(DIL_CONFIGS)
ATT_UNROLL = 3


def _attn_unit(q_ref, kbuf, vbuf, bias, g, start, stride, noprev, o_scr, m_scr, l_scr):
    L = CHUNK
    if stride == 1:
        start = pl.multiple_of(start, L)
        qrows = pl.ds(start, L)
        kvrows = pl.ds(start, 2 * L)
    else:
        qrows = pl.ds(start, L, stride=stride)
        kvrows = pl.ds(start, 2 * L, stride=stride)
    q2 = q_ref[qrows, :].astype(BF16)
    kb = kbuf[kvrows, :].astype(BF16)
    vb = vbuf[kvrows, :].astype(BF16)
    head0 = lax.broadcasted_iota(jnp.int32, (2 * L, ATT_PW), 1) < DIL_DH
    zero = jnp.zeros_like(kb)
    k2 = jnp.concatenate([jnp.where(head0, kb, zero), jnp.where(head0, zero, kb)], axis=0)
    v2 = jnp.concatenate([jnp.where(head0, vb, zero), jnp.where(head0, zero, vb)], axis=0)
    s = _dot_nt(q2, k2) + bias
    if noprev is not None:
        col = lax.broadcasted_iota(jnp.int32, (L, 4 * L), 1)
        prev_col = (col % (2 * L)) < L
        s = jnp.where(jnp.logical_and(prev_col, noprev), NEG_INF, s)
    s0, s1 = s[:, 0:2 * L], s[:, 2 * L:4 * L]
    m0 = jnp.max(s0, -1, keepdims=True)
    m1 = jnp.max(s1, -1, keepdims=True)
    p0 = jnp.exp(s0 - m0)
    p1 = jnp.exp(s1 - m1)
    l0 = jnp.sum(p0, -1, keepdims=True)
    l1 = jnp.sum(p1, -1, keepdims=True)
    o = jnp.dot(jnp.concatenate([p0, p1], axis=1).astype(BF16), v2, preferred_element_type=F32)
    left = lax.broadcasted_iota(jnp.int32, (L, ATT_PW), 1) < DIL_DH
    o_scr[g, qrows, :] = o
    m_scr[g, qrows, :] = jnp.where(left, m0, m1)
    l_scr[g, qrows, :] = jnp.where(left, l0, l1)


def _attn_kernel(q0_ref, q1_ref, q2_ref, kvc_ref, kvp0_ref, kvp1_ref, kvp2_ref, bias_ref, y_ref,
                 k0, k1, k2, v0, v1, v2, o_scr, m_scr, l_scr):
    first = pl.program_id(2) == 0
    q_refs = (q0_ref, q1_ref, q2_ref)
    kbufs = (k0, k1, k2)
    vbufs = (v0, v1, v2)
    prevs = (kvp0_ref, kvp1_ref, kvp2_ref)
    kvw = 2 * ATT_PW
    for g, (window, _) in enumerate(DIL_CONFIGS):
        kbufs[g][0:window, :] = prevs[g][:, 0:ATT_PW]
        vbufs[g][0:window, :] = prevs[g][:, ATT_PW:kvw]
        kbufs[g][window:window + ATT_SPAN, :] = kvc_ref[:, g * kvw:g * kvw + ATT_PW]
        vbufs[g][window:window + ATT_SPAN, :] = kvc_ref[:, g * kvw + ATT_PW:(g + 1) * kvw]

    for g, (window, dilation) in enumerate(DIL_CONFIGS):
        bias = bias_ref[0, g]
        nsub = ATT_SPAN // window

        def first_body(r, _, g=g, bias=bias, dilation=dilation):
            _attn_unit(q_refs[g], kbufs[g], vbufs[g], bias, g, r, dilation, first, o_scr, m_scr, l_scr)
            return 0
        lax.fori_loop(0, dilation, first_body, 0, unroll=min(dilation, ATT_UNROLL))

        if nsub > 1:
            def rest_body(idx, _, g=g, bias=bias, window=window, dilation=dilation):
                start = (1 + idx // dilation) * window + idx % dilation
                _attn_unit(q_refs[g], kbufs[g], vbufs[g], bias, g, start, dilation, None, o_scr, m_scr, l_scr)
                return 0
            lax.fori_loop(0, (nsub - 1) * dilation, rest_body, 0, unroll=ATT_UNROLL)

    rows = 256

    def merge(i, _):
        sl = pl.ds(pl.multiple_of(i * rows, rows), rows)
        ms = [m_scr[g, sl, :] for g in range(N_DIL)]
        mx = functools.reduce(jnp.maximum, ms)
        es = [jnp.exp(m - mx) for m in ms]
        num = sum(e * o_scr[g, sl, :] for g, e in enumerate(es))
        den = sum(e * l_scr[g, sl, :] for g, e in enumerate(es))
        y_ref[sl, :] = (num / den).astype(y_ref.dtype)
        return 0
    lax.fori_loop(0, ATT_SPAN // rows, merge, 0)


def _dilated_attention(q, kv, bias, batch, seq):
    n = q.shape[0]
    nspan = seq // ATT_SPAN
    npair = DIL_HEADS // ATT_PAIR
    kvw = 2 * ATT_PW

    def prev_spec(g, window):
        per_span = ATT_SPAN // window
        rows_per_batch = seq // window
        return pl.BlockSpec(
            (window, kvw),
            lambda b, p, s: (jnp.maximum(b * rows_per_batch + s * per_span - 1, 0), p * N_DIL + g))

    return pl.pallas_call(
        _attn_kernel,
        grid=(batch, npair, nspan),
        in_specs=[pl.BlockSpec((ATT_SPAN, ATT_PW), lambda b, p, s, g=g: (b * nspan + s, p * N_DIL + g))
                  for g in range(N_DIL)] +
                 [pl.BlockSpec((ATT_SPAN, N_DIL * kvw), lambda b, p, s: (b * nspan + s, p))] +
                 [prev_spec(g, w) for g, (w, _) in enumerate(DIL_CONFIGS)] +
                 [pl.BlockSpec((1, N_DIL, CHUNK, 4 * CHUNK), lambda b, p, s: (p, 0, 0, 0))],
        out_specs=pl.BlockSpec((ATT_SPAN, ATT_PW), lambda b, p, s: (b * nspan + s, p)),
        out_shape=jax.ShapeDtypeStruct((n, DIL_W), BF16),
        scratch_shapes=[pltpu.VMEM((w + ATT_SPAN, ATT_PW), F32) for w, _ in DIL_CONFIGS] * 2 +
                       [pltpu.VMEM((N_DIL, ATT_SPAN, ATT_PW), F32) for _ in range(3)],
        compiler_params=_cparams(("parallel", "parallel", "arbitrary")),
        name="dilated_attention",
    )(q, q, q, kv, kv, kv, kv, bias)


def _attention_bias():
    L = CHUNK
    qi = jnp.arange(L)[:, None]
    kr = jnp.arange(2 * L)[None, :]
    steps = qi + L - kr
    band = (steps >= 0) & (steps <= L)
    slopes = jnp.exp2(-8.0 * jnp.arange(1, DIL_HEADS + 1, dtype=F32) / DIL_HEADS)
    per_group = []
    for _, dilation in DIL_CONFIGS:
        b = -slopes[:, None, None] * (steps * dilation).astype(F32)[None]
        b = jnp.where(band[None], b, NEG_INF)
        per_group.append(b.reshape(DIL_HEADS // ATT_PAIR, ATT_PAIR, L, 2 * L)
                         .transpose(0, 2, 1, 3).reshape(DIL_HEADS // ATT_PAIR, L, 4 * L))
    return jnp.stack(per_group, axis=1)


def _odd_weights(w):
    w = w.reshape(D_MODEL, N_DIL, 3, DIL_HEADS // ATT_PAIR, ATT_PAIR, DIL_DH)
    wq = (w[:, :, 0] * (DIL_DH ** -0.5)).transpose(0, 2, 1, 3, 4).reshape(D_MODEL, N_DIL * DIL_W)
    wkv = w[:, :, 1:3].transpose(0, 3, 1, 2, 4, 5).reshape(D_MODEL, 2 * N_DIL * DIL_W)
    return wq.astype(BF16), wkv.astype(BF16)


def _tile_copy(src_hbm, row, dst, i, sem):
    src = src_hbm.at[pl.ds(pl.multiple_of(row * TOK_TILES, TOK_TILES), TOK_TILES), :]
    return pltpu.make_async_copy(src, dst.at[pl.ds(i * TOK_TILES, TOK_TILES), :], sem)


def _experts_kernel(eidx_ref, rank_ref, pstart_ref, count_ref, block_e_ref, nused_ref,
                    h8_hbm, wg_ref, wu_ref, wd_ref, ys8_ref, slot_tok, xbuf, sems):
    j = pl.program_id(0)
    nused = nused_ref[0]
    n_tok = eidx_ref.shape[0] // TOP_K

    @pl.when(j == 0)
    def _():
        def fill(t, _):
            for k in range(TOP_K):
                a = k * n_tok + t
                slot_tok[pstart_ref[eidx_ref[a]] + rank_ref[a]] = t
            return 0
        lax.fori_loop(0, n_tok, fill, 0, unroll=8)

        def pad_expert(e, _):
            cnt = count_ref[e]
            padded = (cnt + MOE_BLK - 1) // MOE_BLK * MOE_BLK

            def pad(i, _):
                slot_tok[pstart_ref[e] + i] = 0
                return 0
            lax.fori_loop(cnt, padded, pad, 0)
            return 0
        lax.fori_loop(0, N_EXPERTS, pad_expert, 0)

    def issue(blk, slot):
        for i in range(MOE_BLK):
            _tile_copy(h8_hbm, slot_tok[blk * MOE_BLK + i], xbuf.at[slot], i, sems.at[slot]).start()

    def wait(slot):
        for i in range(MOE_BLK):
            _tile_copy(h8_hbm, 0, xbuf.at[slot], i, sems.at[slot]).wait()

    ahead = MOE_NBUF - 1
    slot = j % MOE_NBUF

    @pl.when(jnp.logical_and(j == 0, nused > 0))
    def _():
        for a in range(ahead):
            issue(jnp.minimum(a, nused - 1), a)

    @pl.when(j < nused)
    def _():
        issue(jnp.minimum(j + ahead, nused - 1), (j + ahead) % MOE_NBUF)
        wait(slot)
        x = _load_token_tiles(xbuf.at[slot], MOE_BLK).astype(BF16)
        a = jnp.dot(x, wg_ref[0, 0].astype(BF16), preferred_element_type=F32)
        u = jnp.dot(x, wu_ref[0, 0].astype(BF16), preferred_element_type=F32)
        hm = (_silu(a) * u).astype(BF16)
        y = jnp.dot(hm, wd_ref[0, 0].astype(BF16), preferred_element_type=F32)
        _store_token_tiles(ys8_ref, y, MOE_BLK)

    @pl.when(j == nused - 1)
    def _():
        for a in range(1, MOE_NBUF):
            wait((slot + a) % MOE_NBUF)

    @pl.when(j >= nused)
    def _():
        ys8_ref[...] = jnp.zeros_like(ys8_ref)


def _experts(h8, eidx, rank, pstart, counts, block_e, nused, wg, wu, wd, layer, p_slots):
    n_blocks = p_slots // MOE_BLK
    wmap = lambda j, ei, rk, ps, ct, be, nu: (layer, be[j], 0, 0)
    grid_spec = pltpu.PrefetchScalarGridSpec(
        num_scalar_prefetch=6,
        grid=(n_blocks,),
        in_specs=[pl.BlockSpec(memory_space=pl.ANY),
                  pl.BlockSpec((1, 1, D_MODEL, D_EXPERT), wmap),
                  pl.BlockSpec((1, 1, D_MODEL, D_EXPERT), wmap),
                  pl.BlockSpec((1, 1, D_EXPERT, D_MODEL), wmap)],
        out_specs=pl.BlockSpec((MOE_BLK * TOK_TILES, LANES), lambda j, *_: (j, 0)),
        scratch_shapes=[pltpu.SMEM((p_slots,), jnp.int32),
                        pltpu.VMEM((MOE_NBUF, MOE_BLK * TOK_TILES, LANES), F32),
                        pltpu.SemaphoreType.DMA((MOE_NBUF,))],
    )
    return pl.pallas_call(
        _experts_kernel,
        grid_spec=grid_spec,
        out_shape=jax.ShapeDtypeStruct((p_slots * TOK_TILES, LANES), F32),
        compiler_params=_cparams(("arbitrary",)),
        name="moe_experts",
    )(eidx, rank, pstart, counts, block_e, nused, h8, wg, wu, wd)


def _combine_kernel(eidx_ref, rank_ref, pstart_ref, h8_ref, gcol_ref, ys8_hbm, lng_ref, lnb_ref,
                    o_ref, buf, sems):
    i = pl.program_id(0)
    nt = pl.num_programs(0)
    n_tok = nt * COMB_TM

    def issue(tile, slot):
        for t in range(COMB_TM):
            for k in range(TOP_K):
                a = k * n_tok + tile * COMB_TM + t
                row = pstart_ref[eidx_ref[a]] + rank_ref[a]
                _tile_copy(ys8_hbm, row, buf.at[slot, k], t, sems.at[slot]).start()

    def wait(slot):
        for t in range(COMB_TM):
            for k in range(TOP_K):
                _tile_copy(ys8_hbm, 0, buf.at[slot, k], t, sems.at[slot]).wait()

    slot = i % 2

    @pl.when(i == 0)
    def _():
        issue(0, 0)

    issue(jnp.minimum(i + 1, nt - 1), 1 - slot)
    wait(slot)
    g = gcol_ref[...]
    ffn = g[:, 0:1] * _load_token_tiles(buf.at[slot, 0], COMB_TM) + \
        g[:, 1:2] * _load_token_tiles(buf.at[slot, 1], COMB_TM)
    z = ALPHA * _load_token_tiles(h8_ref, COMB_TM) + ffn
    o_ref[...] = _layer_norm(z, lng_ref[...], lnb_ref[...])

    @pl.when(i == nt - 1)
    def _():
        wait(1 - slot)


def _combine(h8, gcol, ys8, eidx, rank, pstart, ln_g, ln_b):
    n = gcol.shape[0]
    const2 = lambda i, *_: (0, 0)
    grid_spec = pltpu.PrefetchScalarGridSpec(
        num_scalar_prefetch=3,
        grid=(n // COMB_TM,),
        in_specs=[pl.BlockSpec((COMB_TM * TOK_TILES, LANES), lambda i, *_: (i, 0)),
                  pl.BlockSpec((COMB_TM, LANES), lambda i, *_: (i, 0)),
                  pl.BlockSpec(memory_space=pl.ANY),
                  pl.BlockSpec((1, D_MODEL), const2),
                  pl.BlockSpec((1, D_MODEL), const2)],
        out_specs=pl.BlockSpec((COMB_TM, D_MODEL), lambda i, *_: (i, 0)),
        scratch_shapes=[pltpu.VMEM((2, TOP_K, COMB_TM * TOK_TILES, LANES), F32),
                        pltpu.SemaphoreType.DMA((2,))],
    )
    return pl.pallas_call(
        _combine_kernel,
        grid_spec=grid_spec,
        out_shape=jax.ShapeDtypeStruct((n, D_MODEL), F32),
        compiler_params=_cparams(("arbitrary",)),
        name="moe_combine",
    )(eidx, rank, pstart, h8, gcol, ys8, ln_g, ln_b)


def _moe_and_norm(h8, eidx, rank, gcol, counts, wg, wu, wd, layer, ln_g, ln_b):
    n = gcol.shape[0]
    p_slots = n * TOP_K + N_EXPERTS * MOE_BLK
    n_blocks = p_slots // MOE_BLK
    cnt = counts[:, 0]
    padded = (cnt + MOE_BLK - 1) // MOE_BLK * MOE_BLK
    pend = jnp.cumsum(padded)
    pstart = (pend - padded).astype(jnp.int32)
    block_start = jnp.arange(n_blocks, dtype=jnp.int32) * MOE_BLK
    block_e = jnp.minimum(jnp.sum(pend[None, :] <= block_start[:, None], axis=1), N_EXPERTS - 1).astype(jnp.int32)
    nused = (pend[-1:] // MOE_BLK).astype(jnp.int32)
    eflat, rflat = eidx.reshape(-1), rank.reshape(-1)
    ys8 = _experts(h8, eflat, rflat, pstart, cnt, block_e, nused, wg, wu, wd, layer, p_slots)
    return _combine(h8, gcol, ys8, eflat, rflat, pstart, ln_g, ln_b)


def _retention_constants():
    log_gamma = jnp.log1p(-jnp.exp2(-5.0 - jnp.arange(RET_HEADS, dtype=F32)))
    idx = jnp.arange(CHUNK)
    rel = idx[:, None] - idx[None, :]
    decay_in = jnp.where(rel >= 0, jnp.exp(log_gamma[:, None, None] * jnp.maximum(rel, 0)), 0.0)
    xi = jnp.exp(log_gamma[:, None] * (idx + 1))
    zeta = jnp.exp(log_gamma[:, None] * (CHUNK - 1 - idx))
    g_chunk = jnp.exp(log_gamma * CHUNK)
    rc = jnp.zeros((CHUNK, LANES), F32)
    rc = rc.at[:, 0:RET_HEADS].set(xi.T)
    rc = rc.at[:, RET_HEADS:2 * RET_HEADS].set(zeta.T)
    rc = rc.at[:, 2 * RET_HEADS:3 * RET_HEADS].set(jnp.broadcast_to(g_chunk[None, :], (CHUNK, RET_HEADS)))
    return decay_in.astype(F32), rc


def kernel(x, w_in_even, i_bias, f_bias, conv_w, conv_b, mlstm_norm_g, ret_norm_g, w_out_even, w_in_odd,
           w_out_odd, router_w, router_bias, w_gate, w_up, w_down, ln_g, ln_b):
    batch, seq, _ = x.shape
    n = batch * seq
    assert all(w // d == CHUNK for w, d in DIL_CONFIGS) and seq % ATT_SPAN == 0
    h = x.reshape(n, D_MODEL)
    router_wt = router_w.T.astype(F32)
    router_b = router_bias.astype(F32).reshape(N_EXPERTS, 1)
    decay_in, ret_consts = _retention_constants()
    tri = (jnp.arange(POST_TM)[:, None] <= jnp.arange(POST_TM)[None, :]).astype(BF16)
    att_bias = _attention_bias()
    gate_col0 = 4 * MLSTM_W
    gate_col1 = gate_col0 + 2 * MLSTM_HEADS

    for layer in range(DEPTH):
        j = layer // 2
        lg = lambda s: ln_g[layer, s].reshape(1, D_MODEL)
        lb = lambda s: ln_b[layer, s].reshape(1, D_MODEL)
        if layer % 2 == 0:
            w = w_in_even[j]
            w_re = jnp.concatenate(
                [w[:, :gate_col0], w[:, gate_col1:], w[:, gate_col0:gate_col1],
                 jnp.zeros((D_MODEL, LANES - 2 * MLSTM_HEADS), w.dtype)], axis=1).astype(BF16)
            gate_bias = jnp.zeros((1, LANES), F32)
            gate_bias = gate_bias.at[0, 0:MLSTM_HEADS].set(i_bias[j])
            gate_bias = gate_bias.at[0, MLSTM_HEADS:2 * MLSTM_HEADS].set(f_bias[j])
            norm_g = jnp.concatenate([mlstm_norm_g[j], ret_norm_g[j]]).reshape(1, MLSTM_W + RET_W)
            y = _even_mixer(h, w_re, batch, seq, conv_w[j], conv_b[j].reshape(1, 2 * MLSTM_W), gate_bias,
                            decay_in, ret_consts, norm_g.astype(F32))
            w_out_all = w_out_even
        else:
            wq, wkv = _odd_weights(w_in_odd[j])
            q = _project(h, wq, N_DIL * DIL_W, "proj_odd_q")
            kv = _project(h, wkv, N_DIL * DIL_W, "proj_odd_kv")
            y = _dilated_attention(q, kv, att_bias, batch, seq)
            w_out_all = w_out_odd
        h8, eidx, rank, gcol, counts = _post_mixer(h, y, w_out_all, j, lg(0), lb(0),
                                                   router_wt, router_b, tri, f"post_{layer}")
        h = _moe_and_norm(h8, eidx, rank, gcol, counts, w_gate, w_up, w_down, layer, lg(1), lb(1))
    return h.reshape(batch, seq, D_MODEL)
```

```python
import functools

import jax
import jax.numpy as jnp
from jax import lax
from jax.experimental import pallas as pl
from jax.experimental.pallas import tpu as pltpu

F32 = jnp.float32
BF16 = jnp.bfloat16

D_MODEL = 1024
DEPTH = 2
ALPHA = (2 * DEPTH) ** 0.25
LN_EPS = 1e-5
MLSTM_HEADS = 4
RET_HEADS = 4
HEAD_DIM = 128
MLSTM_W = MLSTM_HEADS * HEAD_DIM
RET_W = RET_HEADS * HEAD_DIM
CHUNK = 128
CONV_WIDTH = 4
DIL_HEADS = 8
DIL_DH = 64
DIL_W = DIL_HEADS * DIL_DH
DIL_CONFIGS = ((128, 1), (512, 4), (2048, 16))
N_EXPERTS = 32
N_GROUPS = 4
EXPERTS_PER_GROUP = N_EXPERTS // N_GROUPS
TOP_K = 2
D_EXPERT = 256

LANES = 128
SUBLANES = 8
VMEM_LIMIT_BYTES = 56 * 1024 * 1024

PROJ_TM = 512
POST_TM = 512
MOE_BLK = 256
MOE_NBUF = 2
COMB_TM = 256
CONV_TAIL = SUBLANES

EV_QM, EV_KM, EV_VM, EV_OM = 0, MLSTM_W, 2 * MLSTM_W, 3 * MLSTM_W
EV_QR, EV_KR, EV_VR, EV_GR = (4 * MLSTM_W, 4 * MLSTM_W + RET_W, 4 * MLSTM_W + 2 * RET_W,
                              4 * MLSTM_W + 3 * RET_W)
EV_GATE = 4 * MLSTM_W + 4 * RET_W
EV_COLS = EV_GATE + LANES

NEG_INF = float("-inf")


def _cparams(sem):
    return pltpu.CompilerParams(dimension_semantics=sem, vmem_limit_bytes=VMEM_LIMIT_BYTES)


def _proj_qkv_kernel(x_ref, w_ref, q_ref, kv_ref):
    xb = x_ref[...].astype(BF16)
    nq = q_ref.shape[1]
    q_ref[...] = jnp.dot(xb, w_ref[:, :nq], preferred_element_type=F32)
    kv_ref[...] = jnp.dot(xb, w_ref[:, nq:], preferred_element_type=F32)


def _project_qkv(x, w, nq):
    n, k = x.shape
    nkv = w.shape[1] - nq
    return pl.pallas_call(
        _proj_qkv_kernel,
        grid=(n // PROJ_TM,),
        in_specs=[pl.BlockSpec((PROJ_TM, k), lambda i: (i, 0)),
                  pl.BlockSpec(w.shape, lambda i: (0, 0))],
        out_specs=[pl.BlockSpec((PROJ_TM, nq), lambda i: (i, 0)),
                   pl.BlockSpec((PROJ_TM, nkv), lambda i: (i, 0))],
        out_shape=[jax.ShapeDtypeStruct((n, nq), F32), jax.ShapeDtypeStruct((n, nkv), F32)],
        compiler_params=_cparams(("parallel",)),
        name="proj_odd",
    )(x, w)


EV_STEP_CHUNKS = 2
EV_PROJ_PIECE = 2 * LANES


def _silu(x):
    return x * jax.nn.sigmoid(x)


def _head_norm(h):
    mu = jnp.mean(h, -1, keepdims=True)
    c = h - mu
    var = jnp.mean(c * c, -1, keepdims=True)
    return c * lax.rsqrt(var + LN_EPS)


def _dot_nt(a, b):
    return lax.dot_general(a, b, (((1,), (1,)), ((), ())), preferred_element_type=F32)


def _dot_tn(a_f32, b):
    return jnp.dot(a_f32.T.astype(BF16), b, preferred_element_type=F32)


def _even_kernel(x0_ref, x_ref, w_ref, convw_ref, convb_ref, gbias_ref, dec_ref, rc_ref, ng_ref, y_ref,
                 p_scr, xb_scr, qkbuf, c_ref, n_ref, m_ref, r_ref, *, steps_per_batch):
    i = pl.program_id(0)

    @pl.when(i == 0)
    def _():
        p_scr[0] = jnp.dot(x0_ref[...].astype(BF16), w_ref[...], preferred_element_type=F32)

    @pl.when(i % steps_per_batch == 0)
    def _():
        qkbuf[0:CONV_TAIL, :] = jnp.zeros((CONV_TAIL, 2 * MLSTM_W), F32)
        c_ref[...] = jnp.zeros_like(c_ref)
        n_ref[...] = jnp.zeros_like(n_ref)
        m_ref[...] = jnp.zeros_like(m_ref)
        r_ref[...] = jnp.zeros_like(r_ref)

    xb_scr[...] = x_ref[...].astype(BF16)
    nxt = (i + 1) % 2
    pieces = iter(range(0, EV_COLS, EV_PROJ_PIECE))

    def project_piece():
        lo = next(pieces, None)
        if lo is not None:
            hi = min(lo + EV_PROJ_PIECE, EV_COLS)
            p_scr[nxt, :, lo:hi] = jnp.dot(xb_scr[...], w_ref[:, lo:hi], preferred_element_type=F32)

    cur = p_scr.at[i % 2]
    for c in range(EV_STEP_CHUNKS):
        rows = pl.ds(c * CHUNK, CHUNK)
        _mixer_chunk(cur.at[rows], y_ref.at[rows], convw_ref, convb_ref, gbias_ref, dec_ref, rc_ref, ng_ref,
                     qkbuf, c_ref, n_ref, m_ref, r_ref, project_piece)
    for _ in pieces:
        raise AssertionError("projection pieces left over: EV_PROJ_PIECE too small for the number of heads")


def _mixer_chunk(p_ref, y_ref, convw_ref, convb_ref, gbias_ref, dec_ref, rc_ref, ng_ref,
                 qkbuf, c_ref, n_ref, m_ref, r_ref, between):
    L = CHUNK
    scale = HEAD_DIM ** -0.5
    between()

    qkbuf[CONV_TAIL:CONV_TAIL + L, :] = p_ref[:, EV_QM:EV_QM + 2 * MLSTM_W]
    acc = jnp.broadcast_to(convb_ref[...], (L, 2 * MLSTM_W))
    for k in range(CONV_WIDTH):
        off = CONV_TAIL - (CONV_WIDTH - 1) + k
        acc = acc + convw_ref[k:k + 1, :] * qkbuf[off:off + L, :]
    qk = _silu(acc)
    qkbuf[0:CONV_TAIL, :] = qkbuf[L:L + CONV_TAIL, :]

    pre = p_ref[:, EV_GATE:EV_GATE + LANES] + gbias_ref[...]
    logf = jnp.minimum(pre, 0.0) - jnp.log1p(jnp.exp(-jnp.abs(pre)))
    row = lax.broadcasted_iota(jnp.int32, (L, L), 0)
    col = lax.broadcasted_iota(jnp.int32, (L, L), 1)
    causal = row >= col
    bcs = jnp.dot(causal.astype(F32), logf, preferred_element_type=F32,
                  precision=lax.Precision.HIGHEST)
    u_t = (pre - pltpu.roll(bcs, LANES - MLSTM_HEADS, 1)).T

    for h in range(MLSTM_HEADS):
        between()
        lo = h * HEAD_DIM
        b_col = bcs[:, MLSTM_HEADS + h:MLSTM_HEADS + h + 1]
        i_col = pre[:, h:h + 1]
        r_row = u_t[h:h + 1, :]
        m_prev = m_ref[h:h + 1, 0:1]
        log_d = jnp.where(causal, b_col + r_row, NEG_INF)
        a = b_col + m_prev
        m_t = jnp.maximum(a, jnp.max(log_d, -1, keepdims=True))
        d = jnp.exp(log_d - m_t)
        inter = jnp.exp(a - m_t)
        q = qk[:, lo:lo + HEAD_DIM]
        k = qk[:, MLSTM_W + lo:MLSTM_W + lo + HEAD_DIM] * scale
        qb = q.astype(BF16)
        vb = p_ref[:, EV_VM + lo:EV_VM + lo + HEAD_DIM].astype(BF16)
        s = _dot_nt(qb, k.astype(BF16)) * d
        c_old = c_ref[h]
        n_old = n_ref[h:h + 1, :]
        num = jnp.dot(s.astype(BF16), vb, preferred_element_type=F32) + \
            inter * jnp.dot(qb, c_old.astype(BF16), preferred_element_type=F32)
        den = jnp.sum(s, -1, keepdims=True) + inter * jnp.sum(q * n_old, -1, keepdims=True)
        hh = num / jnp.maximum(jnp.abs(den), jnp.exp(-m_t))
        g = b_col[L - 1:L, :]
        log_w = g - b_col + i_col
        m_new = jnp.maximum(g + m_prev, jnp.max(log_w, 0, keepdims=True))
        kw = k * jnp.exp(log_w - m_new)
        decay = jnp.exp(g + m_prev - m_new)
        c_ref[h] = decay * c_old + _dot_tn(kw, vb)
        n_ref[h:h + 1, :] = decay * n_old + jnp.sum(kw, 0, keepdims=True)
        m_ref[h:h + 1, :] = jnp.broadcast_to(m_new, (1, LANES))
        o_gate = jax.nn.sigmoid(p_ref[:, EV_OM + lo:EV_OM + lo + HEAD_DIM])
        y_ref[:, lo:lo + HEAD_DIM] = (_head_norm(o_gate * hh) * ng_ref[:, lo:lo + HEAD_DIM]).astype(y_ref.dtype)

    for h in range(RET_HEADS):
        between()
        lo = h * HEAD_DIM
        qb = p_ref[:, EV_QR + lo:EV_QR + lo + HEAD_DIM].astype(BF16)
        k = p_ref[:, EV_KR + lo:EV_KR + lo + HEAD_DIM] * scale
        vb = p_ref[:, EV_VR + lo:EV_VR + lo + HEAD_DIM].astype(BF16)
        xi = rc_ref[:, h:h + 1]
        zeta = rc_ref[:, RET_HEADS + h:RET_HEADS + h + 1]
        g_chunk = rc_ref[0:1, 2 * RET_HEADS + h:2 * RET_HEADS + h + 1]
        r_old = r_ref[h]
        inner = _dot_nt(qb, k.astype(BF16)) * dec_ref[h]
        y = jnp.dot(inner.astype(BF16), vb, preferred_element_type=F32) + \
            xi * jnp.dot(qb, r_old.astype(BF16), preferred_element_type=F32)
        r_ref[h] = g_chunk * r_old + _dot_tn(k * zeta, vb)
        gr = p_ref[:, EV_GR + lo:EV_GR + lo + HEAD_DIM]
        out = _head_norm(y) * ng_ref[:, MLSTM_W + lo:MLSTM_W + lo + HEAD_DIM] * _silu(gr)
        y_ref[:, MLSTM_W + lo:MLSTM_W + lo + HEAD_DIM] = out.astype(y_ref.dtype)


def _even_mixer(x, w, batch, seq, conv_w, conv_b, gate_bias, decay_in, ret_consts, norm_g):
    tm = EV_STEP_CHUNKS * CHUNK
    steps_per_batch = seq // tm
    nsteps = batch * steps_per_batch
    const2 = lambda i: (0, 0)
    return pl.pallas_call(
        functools.partial(_even_kernel, steps_per_batch=steps_per_batch),
        grid=(nsteps,),
        in_specs=[pl.BlockSpec((tm, D_MODEL), const2),
                  pl.BlockSpec((tm, D_MODEL), lambda i: (jnp.minimum(i + 1, nsteps - 1), 0)),
                  pl.BlockSpec((D_MODEL, EV_COLS), const2),
                  pl.BlockSpec((CONV_WIDTH, 2 * MLSTM_W), const2),
                  pl.BlockSpec((1, 2 * MLSTM_W), const2),
                  pl.BlockSpec((1, LANES), const2),
                  pl.BlockSpec((RET_HEADS, CHUNK, CHUNK), lambda i: (0, 0, 0)),
                  pl.BlockSpec((CHUNK, LANES), const2),
                  pl.BlockSpec((1, MLSTM_W + RET_W), const2)],
        out_specs=pl.BlockSpec((tm, MLSTM_W + RET_W), lambda i: (i, 0)),
        out_shape=jax.ShapeDtypeStruct((batch * seq, MLSTM_W + RET_W), BF16),
        scratch_shapes=[pltpu.VMEM((2, tm, EV_COLS), F32),
                        pltpu.VMEM((tm, D_MODEL), BF16),
                        pltpu.VMEM((CONV_TAIL + CHUNK, 2 * MLSTM_W), F32),
                        pltpu.VMEM((MLSTM_HEADS, HEAD_DIM, HEAD_DIM), F32),
                        pltpu.VMEM((SUBLANES, HEAD_DIM), F32),
                        pltpu.VMEM((SUBLANES, LANES), F32),
                        pltpu.VMEM((RET_HEADS, HEAD_DIM, HEAD_DIM), F32)],
        compiler_params=_cparams(("arbitrary",)),
        name="even_mixer",
    )(x, x, w, conv_w, conv_b, gate_bias, decay_in, ret_consts, norm_g)


TOK_TILES = D_MODEL // LANES


def _store_token_tiles(ref, val, rows):
    for c in range(TOK_TILES):
        ref[pl.ds(c, rows, stride=TOK_TILES), :] = val[:, c * LANES:(c + 1) * LANES]


def _load_token_tiles(ref, rows):
    return jnp.concatenate([ref[pl.ds(c, rows, stride=TOK_TILES), :] for c in range(TOK_TILES)], axis=1)


def _layer_norm(z, g, b):
    mu = jnp.mean(z, -1, keepdims=True)
    c = z - mu
    var = jnp.mean(c * c, -1, keepdims=True)
    return c * lax.rsqrt(var + LN_EPS) * g + b


def _route(h, rwt_ref, rb_ref):
    logits = lax.dot_general(rwt_ref[...], h, (((1,), (1,)), ((), ())),
                             preferred_element_type=F32, precision=lax.Precision.HIGHEST)
    aff = jax.nn.sigmoid(logits)
    sel = aff + rb_ref[...]
    tm = h.shape[0]
    sub = lax.broadcasted_iota(jnp.int32, (EXPERTS_PER_GROUP, tm), 0)
    best = None
    for g in range(N_GROUPS):
        lo = g * EXPERTS_PER_GROUP
        sg = sel[lo:lo + EXPERTS_PER_GROUP, :]
        ag = aff[lo:lo + EXPERTS_PER_GROUP, :]
        v1 = jnp.max(sg, 0, keepdims=True)
        i1 = jnp.min(jnp.where(sg == v1, sub, EXPERTS_PER_GROUP), 0, keepdims=True)
        rest = jnp.where(sub == i1, NEG_INF, sg)
        v2 = jnp.max(rest, 0, keepdims=True)
        i2 = jnp.min(jnp.where(rest == v2, sub, EXPERTS_PER_GROUP), 0, keepdims=True)
        a1 = jnp.sum(jnp.where(sub == i1, ag, 0.0), 0, keepdims=True)
        a2 = jnp.sum(jnp.where(sub == i2, ag, 0.0), 0, keepdims=True)
        cand = (v1 + v2, i1 + lo, i2 + lo, a1, a2)
        if best is None:
            best = cand
        else:
            take = cand[0] > best[0]
            best = tuple(jnp.where(take, c, b) for c, b in zip(cand, best))
    _, e1, e2, a1, a2 = best
    tot = a1 + a2
    return e1, e2, a1 / tot, a2 / tot


def _post_kernel(x_ref, y_ref, wo_ref, lng_ref, lnb_ref, rwt_ref, rb_ref, tri_ref,
                 h8_ref, eidx_ref, rank_ref, gcol_ref, cnt_ref, run_ref, wo_scr):
    tm = x_ref.shape[0]

    @pl.when(pl.program_id(0) == 0)
    def _():
        run_ref[...] = jnp.zeros_like(run_ref)
        wo_scr[...] = wo_ref[0].astype(BF16)

    mix = jnp.dot(y_ref[...], wo_scr[...], preferred_element_type=F32)
    h = _layer_norm(ALPHA * x_ref[...] + mix, lng_ref[...], lnb_ref[...])
    _store_token_tiles(h8_ref, h, tm)

    e1, e2, g1, g2 = _route(h, rwt_ref, rb_ref)
    eidx_ref[0:1, :] = e1
    eidx_ref[1:2, :] = e2
    gates = jnp.concatenate([g1, g2, jnp.zeros((LANES - TOP_K, tm), F32)], axis=0)
    gcol_ref[...] = gates.T

    sub = lax.broadcasted_iota(jnp.int32, (N_EXPERTS, tm), 0)
    oh1 = sub == e1
    oh2 = sub == e2
    chosen = jnp.logical_or(oh1, oh2).astype(F32)
    incl = jnp.dot(chosen.astype(BF16), tri_ref[...], preferred_element_type=F32)
    before = run_ref[:, 0:1] + incl - chosen
    rank_ref[0:1, :] = jnp.sum(jnp.where(oh1, before, 0.0), 0, keepdims=True).astype(jnp.int32)
    rank_ref[1:2, :] = jnp.sum(jnp.where(oh2, before, 0.0), 0, keepdims=True).astype(jnp.int32)
    run_new = run_ref[...] + incl[:, tm - 1:tm]
    run_ref[...] = run_new
    cnt_ref[...] = run_new.astype(jnp.int32)


def _post_mixer(x, y, w_out_all, j, ln_g, ln_b, router_wt, router_b, tri, name):
    n = x.shape[0]
    tm = POST_TM
    kdim = w_out_all.shape[1]
    const2 = lambda i: (0, 0)
    return pl.pallas_call(
        _post_kernel,
        grid=(n // tm,),
        in_specs=[pl.BlockSpec((tm, D_MODEL), lambda i: (i, 0)),
                  pl.BlockSpec((tm, y.shape[1]), lambda i: (i, 0)),
                  pl.BlockSpec((1, kdim, D_MODEL), lambda i: (j, 0, 0)),
                  pl.BlockSpec((1, D_MODEL), const2),
                  pl.BlockSpec((1, D_MODEL), const2),
                  pl.BlockSpec((N_EXPERTS, D_MODEL), const2),
                  pl.BlockSpec((N_EXPERTS, 1), const2),
                  pl.BlockSpec((tm, tm), const2)],
        out_specs=[pl.BlockSpec((tm * TOK_TILES, LANES), lambda i: (i, 0)),
                   pl.BlockSpec((TOP_K, tm), lambda i: (0, i)),
                   pl.BlockSpec((TOP_K, tm), lambda i: (0, i)),
                   pl.BlockSpec((tm, LANES), lambda i: (i, 0)),
                   pl.BlockSpec((N_EXPERTS, LANES), const2)],
        out_shape=[jax.ShapeDtypeStruct((n * TOK_TILES, LANES), F32),
                   jax.ShapeDtypeStruct((TOP_K, n), jnp.int32),
                   jax.ShapeDtypeStruct((TOP_K, n), jnp.int32),
                   jax.ShapeDtypeStruct((n, LANES), F32),
                   jax.ShapeDtypeStruct((N_EXPERTS, LANES), jnp.int32)],
        scratch_shapes=[pltpu.VMEM((N_EXPERTS, LANES), F32),
                        pltpu.VMEM((kdim, D_MODEL), BF16)],
        compiler_params=_cparams(("arbitrary",)),
        name=name,
    )(x, y, w_out_all, ln_g, ln_b, router_wt, router_b, tri)


ATT_SPAN = max(w for w, _ in DIL_CONFIGS)
ATT_PAIR = 2
ATT_PW = ATT_PAIR * DIL_DH
N_DIL = len(DIL_CONFIGS)


def _attn_group_size(n):
    return 3 if n % 3 == 0 else (4 if n % 4 == 0 else 1)


def _attn_rows(start, stride):
    L = CHUNK
    if stride == 1:
        start = pl.multiple_of(start, L)
        return pl.ds(start, L), pl.ds(start, 2 * L)
    return pl.ds(start, L, stride=stride), pl.ds(start, 2 * L, stride=stride)


def _attn_units(q_ref, kbuf, vbuf, bias, g, starts, stride, noprev, o_scr, m_scr, l_scr):
    rows = [_attn_rows(s, stride) for s in starts]
    loaded = [(q_ref[qr, :].astype(BF16),
               kbuf[kvr, :].astype(BF16),
               vbuf[kvr, :].astype(BF16)) for qr, kvr in rows]
    outs = [_attn_block(q2, kb, vb, bias, noprev) for q2, kb, vb in loaded]
    for (qr, _), (o, mm, ll) in zip(rows, outs):
        o_scr[g, qr, :] = o
        m_scr[g, qr, :] = mm
        l_scr[g, qr, :] = ll


def _attn_block(q2, kb, vb, bias, noprev):
    L = CHUNK
    head0 = lax.broadcasted_iota(jnp.int32, (2 * L, ATT_PW), 1) < DIL_DH
    zero = jnp.zeros_like(kb)
    k2 = jnp.concatenate([jnp.where(head0, kb, zero), jnp.where(head0, zero, kb)], axis=0)
    v2 = jnp.concatenate([jnp.where(head0, vb, zero), jnp.where(head0, zero, vb)], axis=0)
    s = _dot_nt(q2, k2) + bias
    if noprev is not None:
        col = lax.broadcasted_iota(jnp.int32, (L, 4 * L), 1)
        prev_col = (col % (2 * L)) < L
        s = jnp.where(jnp.logical_and(prev_col, noprev), NEG_INF, s)
    s0, s1 = s[:, 0:2 * L], s[:, 2 * L:4 * L]
    m0 = jnp.max(s0, -1, keepdims=True)
    m1 = jnp.max(s1, -1, keepdims=True)
    p0 = jnp.exp(s0 - m0)
    p1 = jnp.exp(s1 - m1)
    l0 = jnp.sum(p0, -1, keepdims=True)
    l1 = jnp.sum(p1, -1, keepdims=True)
    o = jnp.dot(jnp.concatenate([p0, p1], axis=1).astype(BF16), v2, preferred_element_type=F32)
    left = lax.broadcasted_iota(jnp.int32, (L, ATT_PW), 1) < DIL_DH
    return o, jnp.where(left, m0, m1), jnp.where(left, l0, l1)


def _attn_kernel(q0_ref, q1_ref, q2_ref, kvc_ref, kvp0_ref, kvp1_ref, kvp2_ref, bias_ref, y_ref,
                 k0, k1, k2, v0, v1, v2, o_scr, m_scr, l_scr):
    first = pl.program_id(2) == 0
    q_refs = (q0_ref, q1_ref, q2_ref)
    kbufs = (k0, k1, k2)
    vbufs = (v0, v1, v2)
    prevs = (kvp0_ref, kvp1_ref, kvp2_ref)
    kvw = 2 * ATT_PW
    for g, (window, _) in enumerate(DIL_CONFIGS):
        kbufs[g][0:window, :] = prevs[g][:, 0:ATT_PW]
        vbufs[g][0:window, :] = prevs[g][:, ATT_PW:kvw]
        kbufs[g][window:window + ATT_SPAN, :] = kvc_ref[:, g * kvw:g * kvw + ATT_PW]
        vbufs[g][window:window + ATT_SPAN, :] = kvc_ref[:, g * kvw + ATT_PW:(g + 1) * kvw]

    for g, (window, dilation) in enumerate(DIL_CONFIGS):
        bias = bias_ref[0, g]
        nsub = ATT_SPAN // window

        u0 = _attn_group_size(dilation)

        def first_body(t, _, g=g, bias=bias, dilation=dilation, u0=u0):
            starts = [t * u0 + u for u in range(u0)]
            _attn_units(q_refs[g], kbufs[g], vbufs[g], bias, g, starts, dilation, first, o_scr, m_scr, l_scr)
            return 0
        lax.fori_loop(0, dilation // u0, first_body, 0)

        if nsub > 1:
            n_rest = (nsub - 1) * dilation
            u1 = _attn_group_size(n_rest)

            def rest_body(t, _, g=g, bias=bias, window=window, dilation=dilation, u1=u1):
                idx = [t * u1 + u for u in range(u1)]
                starts = [(1 + i // dilation) * window + i % dilation for i in idx]
                _attn_units(q_refs[g], kbufs[g], vbufs[g], bias, g, starts, dilation, None, o_scr, m_scr, l_scr)
                return 0
            lax.fori_loop(0, n_rest // u1, rest_body, 0)

    rows = 256

    def merge(i, _):
        sl = pl.ds(pl.multiple_of(i * rows, rows), rows)
        ms = [m_scr[g, sl, :] for g in range(N_DIL)]
        mx = functools.reduce(jnp.maximum, ms)
        es = [jnp.exp(m - mx) for m in ms]
        num = sum(e * o_scr[g, sl, :] for g, e in enumerate(es))
        den = sum(e * l_scr[g, sl, :] for g, e in enumerate(es))
        y_ref[sl, :] = (num / den).astype(y_ref.dtype)
        return 0
    lax.fori_loop(0, ATT_SPAN // rows, merge, 0)


def _dilated_attention(q, kv, bias, batch, seq):
    n = q.shape[0]
    nspan = seq // ATT_SPAN
    npair = DIL_HEADS // ATT_PAIR
    kvw = 2 * ATT_PW

    def prev_spec(g, window):
        per_span = ATT_SPAN // window
        rows_per_batch = seq // window
        return pl.BlockSpec(
            (window, kvw),
            lambda b, p, s: (jnp.maximum(b * rows_per_batch + s * per_span - 1, 0), p * N_DIL + g))

    return pl.pallas_call(
        _attn_kernel,
        grid=(batch, npair, nspan),
        in_specs=[pl.BlockSpec((ATT_SPAN, ATT_PW), lambda b, p, s, g=g: (b * nspan + s, p * N_DIL + g))
                  for g in range(N_DIL)] +
                 [pl.BlockSpec((ATT_SPAN, N_DIL * kvw), lambda b, p, s: (b * nspan + s, p))] +
                 [prev_spec(g, w) for g, (w, _) in enumerate(DIL_CONFIGS)] +
                 [pl.BlockSpec((1, N_DIL, CHUNK, 4 * CHUNK), lambda b, p, s: (p, 0, 0, 0))],
        out_specs=pl.BlockSpec((ATT_SPAN, ATT_PW), lambda b, p, s: (b * nspan + s, p)),
        out_shape=jax.ShapeDtypeStruct((n, DIL_W), BF16),
        scratch_shapes=[pltpu.VMEM((w + ATT_SPAN, ATT_PW), F32) for w, _ in DIL_CONFIGS] * 2 +
                       [pltpu.VMEM((N_DIL, ATT_SPAN, ATT_PW), F32) for _ in range(3)],
        compiler_params=_cparams(("parallel", "parallel", "arbitrary")),
        name="dilated_attention",
    )(q, q, q, kv, kv, kv, kv, bias)


def _attention_bias():
    L = CHUNK
    qi = jnp.arange(L)[:, None]
    kr = jnp.arange(2 * L)[None, :]
    steps = qi + L - kr
    band = (steps >= 0) & (steps <= L)
    slopes = jnp.exp2(-8.0 * jnp.arange(1, DIL_HEADS + 1, dtype=F32) / DIL_HEADS)
    per_group = []
    for _, dilation in DIL_CONFIGS:
        b = -slopes[:, None, None] * (steps * dilation).astype(F32)[None]
        b = jnp.where(band[None], b, NEG_INF)
        per_group.append(b.reshape(DIL_HEADS // ATT_PAIR, ATT_PAIR, L, 2 * L)
                         .transpose(0, 2, 1, 3).reshape(DIL_HEADS // ATT_PAIR, L, 4 * L))
    return jnp.stack(per_group, axis=1)


def _odd_weights(w):
    w = w.reshape(D_MODEL, N_DIL, 3, DIL_HEADS // ATT_PAIR, ATT_PAIR, DIL_DH)
    wq = (w[:, :, 0] * (DIL_DH ** -0.5)).transpose(0, 2, 1, 3, 4).reshape(D_MODEL, N_DIL * DIL_W)
    wkv = w[:, :, 1:3].transpose(0, 3, 1, 2, 4, 5).reshape(D_MODEL, 2 * N_DIL * DIL_W)
    return wq.astype(BF16), wkv.astype(BF16)


def _tile_copy(src_hbm, row, dst, i, sem):
    src = src_hbm.at[pl.ds(pl.multiple_of(row * TOK_TILES, TOK_TILES), TOK_TILES), :]
    return pltpu.make_async_copy(src, dst.at[pl.ds(i * TOK_TILES, TOK_TILES), :], sem)


def _experts_kernel(eidx_ref, rank_ref, pstart_ref, count_ref, block_e_ref, nused_ref,
                    h8_hbm, wg_ref, wu_ref, wd_ref, ys8_ref, slot_tok, xbuf, sems):
    j = pl.program_id(0)
    nused = nused_ref[0]
    n_tok = eidx_ref.shape[0] // TOP_K

    @pl.when(j == 0)
    def _():
        def fill(t, _):
            for k in range(TOP_K):
                a = k * n_tok + t
                slot_tok[pstart_ref[eidx_ref[a]] + rank_ref[a]] = t
            return 0
        lax.fori_loop(0, n_tok, fill, 0, unroll=8)

        def pad_expert(e, _):
            cnt = count_ref[e]
            padded = (cnt + MOE_BLK - 1) // MOE_BLK * MOE_BLK

            def pad(i, _):
                s = pstart_ref[e] + i
                slot_tok[s] = s % n_tok
                return 0
            lax.fori_loop(cnt, padded, pad, 0)
            return 0
        lax.fori_loop(0, N_EXPERTS, pad_expert, 0)

    def issue(blk, slot):
        for i in range(MOE_BLK):
            _tile_copy(h8_hbm, slot_tok[blk * MOE_BLK + i], xbuf.at[slot], i, sems.at[slot]).start()

    def wait(slot):
        for i in range(MOE_BLK):
            _tile_copy(h8_hbm, 0, xbuf.at[slot], i, sems.at[slot]).wait()

    ahead = MOE_NBUF - 1
    slot = j % MOE_NBUF

    @pl.when(jnp.logical_and(j == 0, nused > 0))
    def _():
        for a in range(ahead):
            issue(jnp.minimum(a, nused - 1), a)

    @pl.when(j < nused)
    def _():
        issue(jnp.minimum(j + ahead, nused - 1), (j + ahead) % MOE_NBUF)
        wait(slot)
        x = _load_token_tiles(xbuf.at[slot], MOE_BLK).astype(BF16)
        a = jnp.dot(x, wg_ref[0, 0].astype(BF16), preferred_element_type=F32)
        u = jnp.dot(x, wu_ref[0, 0].astype(BF16), preferred_element_type=F32)
        hm = (_silu(a) * u).astype(BF16)
        y = jnp.dot(hm, wd_ref[0, 0].astype(BF16), preferred_element_type=F32)
        _store_token_tiles(ys8_ref, y, MOE_BLK)

    @pl.when(j == nused - 1)
    def _():
        for a in range(1, MOE_NBUF):
            wait((slot + a) % MOE_NBUF)

    @pl.when(j >= nused)
    def _():
        ys8_ref[...] = jnp.zeros_like(ys8_ref)


def _experts(h8, eidx, rank, pstart, counts, block_e, nused, wg, wu, wd, layer, p_slots):
    n_blocks = p_slots // MOE_BLK
    wmap = lambda j, ei, rk, ps, ct, be, nu: (layer, be[j], 0, 0)
    grid_spec = pltpu.PrefetchScalarGridSpec(
        num_scalar_prefetch=6,
        grid=(n_blocks,),
        in_specs=[pl.BlockSpec(memory_space=pl.ANY),
                  pl.BlockSpec((1, 1, D_MODEL, D_EXPERT), wmap),
                  pl.BlockSpec((1, 1, D_MODEL, D_EXPERT), wmap),
                  pl.BlockSpec((1, 1, D_EXPERT, D_MODEL), wmap)],
        out_specs=pl.BlockSpec((MOE_BLK * TOK_TILES, LANES), lambda j, *_: (j, 0)),
        scratch_shapes=[pltpu.SMEM((p_slots,), jnp.int32),
                        pltpu.VMEM((MOE_NBUF, MOE_BLK * TOK_TILES, LANES), F32),
                        pltpu.SemaphoreType.DMA((MOE_NBUF,))],
    )
    return pl.pallas_call(
        _experts_kernel,
        grid_spec=grid_spec,
        out_shape=jax.ShapeDtypeStruct((p_slots * TOK_TILES, LANES), F32),
        compiler_params=_cparams(("arbitrary",)),
        name="moe_experts",
    )(eidx, rank, pstart, counts, block_e, nused, h8, wg, wu, wd)


def _combine_kernel(eidx_ref, rank_ref, pstart_ref, h8_ref, gcol_ref, ys8_hbm, lng_ref, lnb_ref,
                    o_ref, buf, sems):
    i = pl.program_id(0)
    nt = pl.num_programs(0)
    n_tok = nt * COMB_TM

    def issue(tile, slot):
        for t in range(COMB_TM):
            for k in range(TOP_K):
                a = k * n_tok + tile * COMB_TM + t
                row = pstart_ref[eidx_ref[a]] + rank_ref[a]
                _tile_copy(ys8_hbm, row, buf.at[slot, k], t, sems.at[slot]).start()

    def wait(slot):
        for t in range(COMB_TM):
            for k in range(TOP_K):
                _tile_copy(ys8_hbm, 0, buf.at[slot, k], t, sems.at[slot]).wait()

    slot = i % 2

    @pl.when(i == 0)
    def _():
        issue(0, 0)

    issue(jnp.minimum(i + 1, nt - 1), 1 - slot)
    wait(slot)
    g = gcol_ref[...]
    ffn = g[:, 0:1] * _load_token_tiles(buf.at[slot, 0], COMB_TM) + \
        g[:, 1:2] * _load_token_tiles(buf.at[slot, 1], COMB_TM)
    z = ALPHA * _load_token_tiles(h8_ref, COMB_TM) + ffn
    o_ref[...] = _layer_norm(z, lng_ref[...], lnb_ref[...])

    @pl.when(i == nt - 1)
    def _():
        wait(1 - slot)


def _combine(h8, gcol, ys8, eidx, rank, pstart, ln_g, ln_b):
    n = gcol.shape[0]
    const2 = lambda i, *_: (0, 0)
    grid_spec = pltpu.PrefetchScalarGridSpec(
        num_scalar_prefetch=3,
        grid=(n // COMB_TM,),
        in_specs=[pl.BlockSpec((COMB_TM * TOK_TILES, LANES), lambda i, *_: (i, 0)),
                  pl.BlockSpec((COMB_TM, LANES), lambda i, *_: (i, 0)),
                  pl.BlockSpec(memory_space=pl.ANY),
                  pl.BlockSpec((1, D_MODEL), const2),
                  pl.BlockSpec((1, D_MODEL), const2)],
        out_specs=pl.BlockSpec((COMB_TM, D_MODEL), lambda i, *_: (i, 0)),
        scratch_shapes=[pltpu.VMEM((2, TOP_K, COMB_TM * TOK_TILES, LANES), F32),
                        pltpu.SemaphoreType.DMA((2,))],
    )
    return pl.pallas_call(
        _combine_kernel,
        grid_spec=grid_spec,
        out_shape=jax.ShapeDtypeStruct((n, D_MODEL), F32),
        compiler_params=_cparams(("arbitrary",)),
        name="moe_combine",
    )(eidx, rank, pstart, h8, gcol, ys8, ln_g, ln_b)


def _moe_and_norm(h8, eidx, rank, gcol, counts, wg, wu, wd, layer, ln_g, ln_b):
    n = gcol.shape[0]
    p_slots = n * TOP_K + N_EXPERTS * MOE_BLK
    n_blocks = p_slots // MOE_BLK
    cnt = counts[:, 0]
    padded = (cnt + MOE_BLK - 1) // MOE_BLK * MOE_BLK
    pend = jnp.cumsum(padded)
    pstart = (pend - padded).astype(jnp.int32)
    block_start = jnp.arange(n_blocks, dtype=jnp.int32) * MOE_BLK
    block_e = jnp.minimum(jnp.sum(pend[None, :] <= block_start[:, None], axis=1), N_EXPERTS - 1).astype(jnp.int32)
    nused = (pend[-1:] // MOE_BLK).astype(jnp.int32)
    eflat, rflat = eidx.reshape(-1), rank.reshape(-1)
    ys8 = _experts(h8, eflat, rflat, pstart, cnt, block_e, nused, wg, wu, wd, layer, p_slots)
    return _combine(h8, gcol, ys8, eflat, rflat, pstart, ln_g, ln_b)


def _retention_constants():
    log_gamma = jnp.log1p(-jnp.exp2(-5.0 - jnp.arange(RET_HEADS, dtype=F32)))
    idx = jnp.arange(CHUNK)
    rel = idx[:, None] - idx[None, :]
    decay_in = jnp.where(rel >= 0, jnp.exp(log_gamma[:, None, None] * jnp.maximum(rel, 0)), 0.0)
    xi = jnp.exp(log_gamma[:, None] * (idx + 1))
    zeta = jnp.exp(log_gamma[:, None] * (CHUNK - 1 - idx))
    g_chunk = jnp.exp(log_gamma * CHUNK)
    rc = jnp.zeros((CHUNK, LANES), F32)
    rc = rc.at[:, 0:RET_HEADS].set(xi.T)
    rc = rc.at[:, RET_HEADS:2 * RET_HEADS].set(zeta.T)
    rc = rc.at[:, 2 * RET_HEADS:3 * RET_HEADS].set(jnp.broadcast_to(g_chunk[None, :], (CHUNK, RET_HEADS)))
    return decay_in.astype(F32), rc


def kernel(x, w_in_even, i_bias, f_bias, conv_w, conv_b, mlstm_norm_g, ret_norm_g, w_out_even, w_in_odd,
           w_out_odd, router_w, router_bias, w_gate, w_up, w_down, ln_g, ln_b):
    batch, seq, _ = x.shape
    n = batch * seq
    assert all(w // d == CHUNK for w, d in DIL_CONFIGS) and seq % ATT_SPAN == 0
    h = x.reshape(n, D_MODEL)
    router_wt = router_w.T.astype(F32)
    router_b = router_bias.astype(F32).reshape(N_EXPERTS, 1)
    decay_in, ret_consts = _retention_constants()
    tri = (jnp.arange(POST_TM)[:, None] <= jnp.arange(POST_TM)[None, :]).astype(BF16)
    att_bias = _attention_bias()
    gate_col0 = 4 * MLSTM_W
    gate_col1 = gate_col0 + 2 * MLSTM_HEADS

    for layer in range(DEPTH):
        j = layer // 2
        lg = lambda s: ln_g[layer, s].reshape(1, D_MODEL)
        lb = lambda s: ln_b[layer, s].reshape(1, D_MODEL)
        if layer % 2 == 0:
            w = w_in_even[j]
            w_re = jnp.concatenate(
                [w[:, :gate_col0], w[:, gate_col1:], w[:, gate_col0:gate_col1],
                 jnp.zeros((D_MODEL, LANES - 2 * MLSTM_HEADS), w.dtype)], axis=1).astype(BF16)
            gate_bias = jnp.zeros((1, LANES), F32)
            gate_bias = gate_bias.at[0, 0:MLSTM_HEADS].set(i_bias[j])
            gate_bias = gate_bias.at[0, MLSTM_HEADS:2 * MLSTM_HEADS].set(f_bias[j])
            norm_g = jnp.concatenate([mlstm_norm_g[j], ret_norm_g[j]]).reshape(1, MLSTM_W + RET_W)
            y = _even_mixer(h, w_re, batch, seq, conv_w[j], conv_b[j].reshape(1, 2 * MLSTM_W), gate_bias,
                            decay_in, ret_consts, norm_g.astype(F32))
            w_out_all = w_out_even
        else:
            wq, wkv = _odd_weights(w_in_odd[j])
            q, kv = _project_qkv(h, jnp.concatenate([wq, wkv], axis=1), N_DIL * DIL_W)
            y = _dilated_attention(q, kv, att_bias, batch, seq)
            w_out_all = w_out_odd
        h8, eidx, rank, gcol, counts = _post_mixer(h, y, w_out_all, j, lg(0), lb(0),
                                                   router_wt, router_b, tri, f"post_{layer}")
        h = _moe_and_norm(h8, eidx, rank, gcol, counts, w_gate, w_up, w_down, layer, lg(1), lb(1))
    return h.reshape(batch, seq, D_MODEL)
```

```python
import functools

import jax
import jax.numpy as jnp
from jax import lax
from jax.experimental import pallas as pl
from jax.experimental.pallas import tpu as pltpu

F32 = jnp.float32
BF16 = jnp.bfloat16

D_MODEL = 1024
DEPTH = 2
ALPHA = (2 * DEPTH) ** 0.25
LN_EPS = 1e-5
MLSTM_HEADS = 4
RET_HEADS = 4
HEAD_DIM = 128
MLSTM_W = MLSTM_HEADS * HEAD_DIM
RET_W = RET_HEADS * HEAD_DIM
CHUNK = 128
CONV_WIDTH = 4
DIL_HEADS = 8
DIL_DH = 64
DIL_W = DIL_HEADS * DIL_DH
DIL_CONFIGS = ((128, 1), (512, 4), (2048, 16))
N_EXPERTS = 32
N_GROUPS = 4
EXPERTS_PER_GROUP = N_EXPERTS // N_GROUPS
TOP_K = 2
D_EXPERT = 256

LANES = 128
SUBLANES = 8
VMEM_LIMIT_BYTES = 56 * 1024 * 1024

PROJ_TM = 512
POST_TM = 512
MOE_BLK = 256
MOE_NBUF = 2
COMB_TM = 256
CONV_TAIL = SUBLANES

EV_QM, EV_KM, EV_VM, EV_OM = 0, MLSTM_W, 2 * MLSTM_W, 3 * MLSTM_W
EV_QR, EV_KR, EV_VR, EV_GR = (4 * MLSTM_W, 4 * MLSTM_W + RET_W, 4 * MLSTM_W + 2 * RET_W,
                              4 * MLSTM_W + 3 * RET_W)
EV_GATE = 4 * MLSTM_W + 4 * RET_W
EV_COLS = EV_GATE + LANES

NEG_INF = float("-inf")


def _cparams(sem):
    return pltpu.CompilerParams(dimension_semantics=sem, vmem_limit_bytes=VMEM_LIMIT_BYTES)


def _proj_qkv_kernel(x_ref, w_ref, q_ref, kv_ref):
    xb = x_ref[...].astype(BF16)
    nq = q_ref.shape[1]
    q_ref[...] = jnp.dot(xb, w_ref[:, :nq], preferred_element_type=F32)
    kv_ref[...] = jnp.dot(xb, w_ref[:, nq:], preferred_element_type=F32)


def _project_qkv(x, w, nq):
    n, k = x.shape
    nkv = w.shape[1] - nq
    return pl.pallas_call(
        _proj_qkv_kernel,
        grid=(n // PROJ_TM,),
        in_specs=[pl.BlockSpec((PROJ_TM, k), lambda i: (i, 0)),
                  pl.BlockSpec(w.shape, lambda i: (0, 0))],
        out_specs=[pl.BlockSpec((PROJ_TM, nq), lambda i: (i, 0)),
                   pl.BlockSpec((PROJ_TM, nkv), lambda i: (i, 0))],
        out_shape=[jax.ShapeDtypeStruct((n, nq), F32), jax.ShapeDtypeStruct((n, nkv), F32)],
        compiler_params=_cparams(("parallel",)),
        name="proj_odd",
    )(x, w)


EV_STEP_CHUNKS = 2
EV_PROJ_PIECE = 2 * LANES


def _silu(x):
    return x * jax.nn.sigmoid(x)


def _head_norm(h):
    mu = jnp.mean(h, -1, keepdims=True)
    c = h - mu
    var = jnp.mean(c * c, -1, keepdims=True)
    return c * lax.rsqrt(var + LN_EPS)


def _dot_nt(a, b):
    return lax.dot_general(a, b, (((1,), (1,)), ((), ())), preferred_element_type=F32)


def _dot_tn(a_f32, b):
    return lax.dot_general(a_f32.astype(BF16), b, (((0,), (0,)), ((), ())), preferred_element_type=F32)


def _even_kernel(x0_ref, x_ref, w_ref, convw_ref, convb_ref, gbias_ref, dec_ref, rc_ref, ng_ref, y_ref,
                 p_scr, xb_scr, qkbuf, c_ref, n_ref, m_ref, r_ref, *, steps_per_batch):
    i = pl.program_id(0)

    @pl.when(i == 0)
    def _():
        p_scr[0] = jnp.dot(x0_ref[...].astype(BF16), w_ref[...], preferred_element_type=F32)

    @pl.when(i % steps_per_batch == 0)
    def _():
        qkbuf[0:CONV_TAIL, :] = jnp.zeros((CONV_TAIL, 2 * MLSTM_W), F32)
        c_ref[...] = jnp.zeros_like(c_ref)
        n_ref[...] = jnp.zeros_like(n_ref)
        m_ref[...] = jnp.zeros_like(m_ref)
        r_ref[...] = jnp.zeros_like(r_ref)

    xb_scr[...] = x_ref[...].astype(BF16)
    nxt = (i + 1) % 2
    pieces = iter(range(0, EV_COLS, EV_PROJ_PIECE))

    def project_piece():
        lo = next(pieces, None)
        if lo is not None:
            hi = min(lo + EV_PROJ_PIECE, EV_COLS)
            p_scr[nxt, :, lo:hi] = jnp.dot(xb_scr[...], w_ref[:, lo:hi], preferred_element_type=F32)

    cur = p_scr.at[i % 2]
    for c in range(EV_STEP_CHUNKS):
        rows = pl.ds(c * CHUNK, CHUNK)
        _mixer_chunk(cur.at[rows], y_ref.at[rows], convw_ref, convb_ref, gbias_ref, dec_ref, rc_ref, ng_ref,
                     qkbuf, c_ref, n_ref, m_ref, r_ref, project_piece)
    for _ in pieces:
        raise AssertionError("projection pieces left over: EV_PROJ_PIECE too small for the number of heads")


def _mixer_chunk(p_ref, y_ref, convw_ref, convb_ref, gbias_ref, dec_ref, rc_ref, ng_ref,
                 qkbuf, c_ref, n_ref, m_ref, r_ref, between):
    L = CHUNK
    scale = HEAD_DIM ** -0.5
    between()

    qkbuf[CONV_TAIL:CONV_TAIL + L, :] = p_ref[:, EV_QM:EV_QM + 2 * MLSTM_W]
    acc = jnp.broadcast_to(convb_ref[...], (L, 2 * MLSTM_W))
    for k in range(CONV_WIDTH):
        off = CONV_TAIL - (CONV_WIDTH - 1) + k
        acc = acc + convw_ref[k:k + 1, :] * qkbuf[off:off + L, :]
    qk = _silu(acc)
    qkbuf[0:CONV_TAIL, :] = qkbuf[L:L + CONV_TAIL, :]

    pre = p_ref[:, EV_GATE:EV_GATE + LANES] + gbias_ref[...]
    logf = jnp.minimum(pre, 0.0) - jnp.log1p(jnp.exp(-jnp.abs(pre)))
    row = lax.broadcasted_iota(jnp.int32, (L, L), 0)
    col = lax.broadcasted_iota(jnp.int32, (L, L), 1)
    causal = row >= col
    bcs = jnp.dot(causal.astype(F32), logf, preferred_element_type=F32,
                  precision=lax.Precision.HIGHEST)
    u_t = (pre - pltpu.roll(bcs, LANES - MLSTM_HEADS, 1)).T

    for h in range(MLSTM_HEADS):
        between()
        lo = h * HEAD_DIM
        b_col = bcs[:, MLSTM_HEADS + h:MLSTM_HEADS + h + 1]
        i_col = pre[:, h:h + 1]
        r_row = u_t[h:h + 1, :]
        m_prev = m_ref[h:h + 1, 0:1]
        log_d = jnp.where(causal, b_col + r_row, NEG_INF)
        a = b_col + m_prev
        m_t = jnp.maximum(a, jnp.max(log_d, -1, keepdims=True))
        d = jnp.exp(log_d - m_t)
        inter = jnp.exp(a - m_t)
        q = qk[:, lo:lo + HEAD_DIM]
        k = qk[:, MLSTM_W + lo:MLSTM_W + lo + HEAD_DIM] * scale
        qb = q.astype(BF16)
        vb = p_ref[:, EV_VM + lo:EV_VM + lo + HEAD_DIM].astype(BF16)
        s = _dot_nt(qb, k.astype(BF16)) * d
        c_old = c_ref[h]
        n_old = n_ref[h:h + 1, :]
        num = jnp.dot(s.astype(BF16), vb, preferred_element_type=F32) + \
            inter * jnp.dot(qb, c_old.astype(BF16), preferred_element_type=F32)
        den = jnp.sum(s, -1, keepdims=True) + inter * jnp.sum(q * n_old, -1, keepdims=True)
        hh = num / jnp.maximum(jnp.abs(den), jnp.exp(-m_t))
        g = b_col[L - 1:L, :]
        log_w = g - b_col + i_col
        m_new = jnp.maximum(g + m_prev, jnp.max(log_w, 0, keepdims=True))
        kw = k * jnp.exp(log_w - m_new)
        decay = jnp.exp(g + m_prev - m_new)
        c_ref[h] = decay * c_old + _dot_tn(kw, vb)
        n_ref[h:h + 1, :] = decay * n_old + jnp.sum(kw, 0, keepdims=True)
        m_ref[h:h + 1, :] = jnp.broadcast_to(m_new, (1, LANES))
        o_gate = jax.nn.sigmoid(p_ref[:, EV_OM + lo:EV_OM + lo + HEAD_DIM])
        y_ref[:, lo:lo + HEAD_DIM] = (_head_norm(o_gate * hh) * ng_ref[:, lo:lo + HEAD_DIM]).astype(y_ref.dtype)

    for h in range(RET_HEADS):
        between()
        lo = h * HEAD_DIM
        qb = p_ref[:, EV_QR + lo:EV_QR + lo + HEAD_DIM].astype(BF16)
        k = p_ref[:, EV_KR + lo:EV_KR + lo + HEAD_DIM] * scale
        vb = p_ref[:, EV_VR + lo:EV_VR + lo + HEAD_DIM].astype(BF16)
        xi = rc_ref[:, h:h + 1]
        zeta = rc_ref[:, RET_HEADS + h:RET_HEADS + h + 1]
        g_chunk = rc_ref[0:1, 2 * RET_HEADS + h:2 * RET_HEADS + h + 1]
        r_old = r_ref[h]
        inner = _dot_nt(qb, k.astype(BF16)) * dec_ref[h]
        y = jnp.dot(inner.astype(BF16), vb, preferred_element_type=F32) + \
            xi * jnp.dot(qb, r_old.astype(BF16), preferred_element_type=F32)
        r_ref[h] = g_chunk * r_old + _dot_tn(k * zeta, vb)
        gr = p_ref[:, EV_GR + lo:EV_GR + lo + HEAD_DIM]
        out = _head_norm(y) * ng_ref[:, MLSTM_W + lo:MLSTM_W + lo + HEAD_DIM] * _silu(gr)
        y_ref[:, MLSTM_W + lo:MLSTM_W + lo + HEAD_DIM] = out.astype(y_ref.dtype)


def _even_mixer(x, w, batch, seq, conv_w, conv_b, gate_bias, decay_in, ret_consts, norm_g):
    tm = EV_STEP_CHUNKS * CHUNK
    steps_per_batch = seq // tm
    nsteps = batch * steps_per_batch
    const2 = lambda i: (0, 0)
    return pl.pallas_call(
        functools.partial(_even_kernel, steps_per_batch=steps_per_batch),
        grid=(nsteps,),
        in_specs=[pl.BlockSpec((tm, D_MODEL), const2),
                  pl.BlockSpec((tm, D_MODEL), lambda i: (jnp.minimum(i + 1, nsteps - 1), 0)),
                  pl.BlockSpec((D_MODEL, EV_COLS), const2),
                  pl.BlockSpec((CONV_WIDTH, 2 * MLSTM_W), const2),
                  pl.BlockSpec((1, 2 * MLSTM_W), const2),
                  pl.BlockSpec((1, LANES), const2),
                  pl.BlockSpec((RET_HEADS, CHUNK, CHUNK), lambda i: (0, 0, 0)),
                  pl.BlockSpec((CHUNK, LANES), const2),
                  pl.BlockSpec((1, MLSTM_W + RET_W), const2)],
        out_specs=pl.BlockSpec((tm, MLSTM_W + RET_W), lambda i: (i, 0)),
        out_shape=jax.ShapeDtypeStruct((batch * seq, MLSTM_W + RET_W), BF16),
        scratch_shapes=[pltpu.VMEM((2, tm, EV_COLS), F32),
                        pltpu.VMEM((tm, D_MODEL), BF16),
                        pltpu.VMEM((CONV_TAIL + CHUNK, 2 * MLSTM_W), F32),
                        pltpu.VMEM((MLSTM_HEADS, HEAD_DIM, HEAD_DIM), F32),
                        pltpu.VMEM((SUBLANES, HEAD_DIM), F32),
                        pltpu.VMEM((SUBLANES, LANES), F32),
                        pltpu.VMEM((RET_HEADS, HEAD_DIM, HEAD_DIM), F32)],
        compiler_params=_cparams(("arbitrary",)),
        name="even_mixer",
    )(x, x, w, conv_w, conv_b, gate_bias, decay_in, ret_consts, norm_g)


TOK_TILES = D_MODEL // LANES


def _store_token_tiles(ref, val, rows):
    for c in range(TOK_TILES):
        ref[pl.ds(c, rows, stride=TOK_TILES), :] = val[:, c * LANES:(c + 1) * LANES]


def _load_token_tiles(ref, rows):
    return jnp.concatenate([ref[pl.ds(c, rows, stride=TOK_TILES), :] for c in range(TOK_TILES)], axis=1)


def _layer_norm(z, g, b):
    mu = jnp.mean(z, -1, keepdims=True)
    c = z - mu
    var = jnp.mean(c * c, -1, keepdims=True)
    return c * lax.rsqrt(var + LN_EPS) * g + b


def _route(h, rwt_ref, rb_ref):
    logits = lax.dot_general(rwt_ref[...], h, (((1,), (1,)), ((), ())),
                             preferred_element_type=F32, precision=lax.Precision.HIGHEST)
    aff = jax.nn.sigmoid(logits)
    sel = aff + rb_ref[...]
    tm = h.shape[0]
    sub = lax.broadcasted_iota(jnp.int32, (EXPERTS_PER_GROUP, tm), 0)
    best = None
    for g in range(N_GROUPS):
        lo = g * EXPERTS_PER_GROUP
        sg = sel[lo:lo + EXPERTS_PER_GROUP, :]
        ag = aff[lo:lo + EXPERTS_PER_GROUP, :]
        v1 = jnp.max(sg, 0, keepdims=True)
        i1 = jnp.min(jnp.where(sg == v1, sub, EXPERTS_PER_GROUP), 0, keepdims=True)
        rest = jnp.where(sub == i1, NEG_INF, sg)
        v2 = jnp.max(rest, 0, keepdims=True)
        i2 = jnp.min(jnp.where(rest == v2, sub, EXPERTS_PER_GROUP), 0, keepdims=True)
        a1 = jnp.sum(jnp.where(sub == i1, ag, 0.0), 0, keepdims=True)
        a2 = jnp.sum(jnp.where(sub == i2, ag, 0.0), 0, keepdims=True)
        cand = (v1 + v2, i1 + lo, i2 + lo, a1, a2)
        if best is None:
            best = cand
        else:
            take = cand[0] > best[0]
            best = tuple(jnp.where(take, c, b) for c, b in zip(cand, best))
    _, e1, e2, a1, a2 = best
    tot = a1 + a2
    return e1, e2, a1 / tot, a2 / tot


def _post_kernel(x_ref, y_ref, wo_ref, lng_ref, lnb_ref, rwt_ref, rb_ref, tri_ref,
                 h8_ref, eidx_ref, rank_ref, gcol_ref, cnt_ref, run_ref, wo_scr):
    tm = x_ref.shape[0]

    @pl.when(pl.program_id(0) == 0)
    def _():
        run_ref[...] = jnp.zeros_like(run_ref)
        wo_scr[...] = wo_ref[0].astype(BF16)

    mix = jnp.dot(y_ref[...], wo_scr[...], preferred_element_type=F32)
    h = _layer_norm(ALPHA * x_ref[...] + mix, lng_ref[...], lnb_ref[...])
    _store_token_tiles(h8_ref, h, tm)

    e1, e2, g1, g2 = _route(h, rwt_ref, rb_ref)
    eidx_ref[0:1, :] = e1
    eidx_ref[1:2, :] = e2
    gates = jnp.concatenate([g1, g2, jnp.zeros((LANES - TOP_K, tm), F32)], axis=0)
    gcol_ref[...] = gates.T

    sub = lax.broadcasted_iota(jnp.int32, (N_EXPERTS, tm), 0)
    oh1 = sub == e1
    oh2 = sub == e2
    chosen = jnp.logical_or(oh1, oh2).astype(F32)
    incl = jnp.dot(chosen.astype(BF16), tri_ref[...], preferred_element_type=F32)
    before = run_ref[:, 0:1] + incl - chosen
    rank_ref[0:1, :] = jnp.sum(jnp.where(oh1, before, 0.0), 0, keepdims=True).astype(jnp.int32)
    rank_ref[1:2, :] = jnp.sum(jnp.where(oh2, before, 0.0), 0, keepdims=True).astype(jnp.int32)
    run_new = run_ref[...] + incl[:, tm - 1:tm]
    run_ref[...] = run_new
    cnt_ref[...] = run_new.astype(jnp.int32)


def _post_mixer(x, y, w_out_all, j, ln_g, ln_b, router_wt, router_b, tri, name):
    n = x.shape[0]
    tm = POST_TM
    kdim = w_out_all.shape[1]
    const2 = lambda i: (0, 0)
    return pl.pallas_call(
        _post_kernel,
        grid=(n // tm,),
        in_specs=[pl.BlockSpec((tm, D_MODEL), lambda i: (i, 0)),
                  pl.BlockSpec((tm, y.shape[1]), lambda i: (i, 0)),
                  pl.BlockSpec((1, kdim, D_MODEL), lambda i: (j, 0, 0)),
                  pl.BlockSpec((1, D_MODEL), const2),
                  pl.BlockSpec((1, D_MODEL), const2),
                  pl.BlockSpec((N_EXPERTS, D_MODEL), const2),
                  pl.BlockSpec((N_EXPERTS, 1), const2),
                  pl.BlockSpec((tm, tm), const2)],
        out_specs=[pl.BlockSpec((tm * TOK_TILES, LANES), lambda i: (i, 0)),
                   pl.BlockSpec((TOP_K, tm), lambda i: (0, i)),
                   pl.BlockSpec((TOP_K, tm), lambda i: (0, i)),
                   pl.BlockSpec((tm, LANES), lambda i: (i, 0)),
                   pl.BlockSpec((N_EXPERTS, LANES), const2)],
        out_shape=[jax.ShapeDtypeStruct((n * TOK_TILES, LANES), F32),
                   jax.ShapeDtypeStruct((TOP_K, n), jnp.int32),
                   jax.ShapeDtypeStruct((TOP_K, n), jnp.int32),
                   jax.ShapeDtypeStruct((n, LANES), F32),
                   jax.ShapeDtypeStruct((N_EXPERTS, LANES), jnp.int32)],
        scratch_shapes=[pltpu.VMEM((N_EXPERTS, LANES), F32),
                        pltpu.VMEM((kdim, D_MODEL), BF16)],
        compiler_params=_cparams(("arbitrary",)),
        name=name,
    )(x, y, w_out_all, ln_g, ln_b, router_wt, router_b, tri)


ATT_SPAN = max(w for w, _ in DIL_CONFIGS)
ATT_PAIR = 2
ATT_PW = ATT_PAIR * DIL_DH
N_DIL = len(DIL_CONFIGS)


def _attn_group_size(n):
    return 3 if n % 3 == 0 else (4 if n % 4 == 0 else 1)


def _attn_rows(start, stride):
    L = CHUNK
    if stride == 1:
        start = pl.multiple_of(start, L)
        return pl.ds(start, L), pl.ds(start, 2 * L)
    return pl.ds(start, L, stride=stride), pl.ds(start, 2 * L, stride=stride)


def _attn_units(q_ref, kbuf, vbuf, bias, g, starts, stride, noprev, o_scr, m_scr, l_scr):
    rows = [_attn_rows(s, stride) for s in starts]
    loaded = [(q_ref[qr, :].astype(BF16),
               kbuf[kvr, :].astype(BF16),
               vbuf[kvr, :].astype(BF16)) for qr, kvr in rows]
    outs = [_attn_block(q2, kb, vb, bias, noprev) for q2, kb, vb in loaded]
    for (qr, _), (o, mm, ll) in zip(rows, outs):
        o_scr[g, qr, :] = o
        m_scr[g, qr, :] = mm
        l_scr[g, qr, :] = ll


def _attn_block(q2, kb, vb, bias, noprev):
    L = CHUNK
    head0 = lax.broadcasted_iota(jnp.int32, (2 * L, ATT_PW), 1) < DIL_DH
    zero = jnp.zeros_like(kb)
    k2 = jnp.concatenate([jnp.where(head0, kb, zero), jnp.where(head0, zero, kb)], axis=0)
    v2 = jnp.concatenate([jnp.where(head0, vb, zero), jnp.where(head0, zero, vb)], axis=0)
    s = _dot_nt(q2, k2) + bias
    if noprev is not None:
        col = lax.broadcasted_iota(jnp.int32, (L, 4 * L), 1)
        prev_col = (col % (2 * L)) < L
        s = jnp.where(jnp.logical_and(prev_col, noprev), NEG_INF, s)
    s0, s1 = s[:, 0:2 * L], s[:, 2 * L:4 * L]
    m0 = jnp.max(s0, -1, keepdims=True)
    m1 = jnp.max(s1, -1, keepdims=True)
    p0 = jnp.exp(s0 - m0)
    p1 = jnp.exp(s1 - m1)
    l0 = jnp.sum(p0, -1, keepdims=True)
    l1 = jnp.sum(p1, -1, keepdims=True)
    o = jnp.dot(jnp.concatenate([p0, p1], axis=1).astype(BF16), v2, preferred_element_type=F32)
    left = lax.broadcasted_iota(jnp.int32, (L, ATT_PW), 1) < DIL_DH
    return o, jnp.where(left, m0, m1), jnp.where(left, l0, l1)


def _attn_kernel(q0_ref, q1_ref, q2_ref, kvc_ref, kvp0_ref, kvp1_ref, kvp2_ref, bias_ref, y_ref,
                 k0, k1, k2, v0, v1, v2, o_scr, m_scr, l_scr):
    first = pl.program_id(2) == 0
    q_refs = (q0_ref, q1_ref, q2_ref)
    kbufs = (k0, k1, k2)
    vbufs = (v0, v1, v2)
    prevs = (kvp0_ref, kvp1_ref, kvp2_ref)
    kvw = 2 * ATT_PW
    for g, (window, _) in enumerate(DIL_CONFIGS):
        kbufs[g][0:window, :] = prevs[g][:, 0:ATT_PW]
        vbufs[g][0:window, :] = prevs[g][:, ATT_PW:kvw]
        kbufs[g][window:window + ATT_SPAN, :] = kvc_ref[:, g * kvw:g * kvw + ATT_PW]
        vbufs[g][window:window + ATT_SPAN, :] = kvc_ref[:, g * kvw + ATT_PW:(g + 1) * kvw]

    for g, (window, dilation) in enumerate(DIL_CONFIGS):
        bias = bias_ref[0, g]
        nsub = ATT_SPAN // window

        u0 = _attn_group_size(dilation)

        def first_body(t, _, g=g, bias=bias, dilation=dilation, u0=u0):
            starts = [t * u0 + u for u in range(u0)]
            _attn_units(q_refs[g], kbufs[g], vbufs[g], bias, g, starts, dilation, first, o_scr, m_scr, l_scr)
            return 0
        lax.fori_loop(0, dilation // u0, first_body, 0)

        if nsub > 1:
            n_rest = (nsub - 1) * dilation
            u1 = _attn_group_size(n_rest)

            def rest_body(t, _, g=g, bias=bias, window=window, dilation=dilation, u1=u1):
                idx = [t * u1 + u for u in range(u1)]
                starts = [(1 + i // dilation) * window + i % dilation for i in idx]
                _attn_units(q_refs[g], kbufs[g], vbufs[g], bias, g, starts, dilation, None, o_scr, m_scr, l_scr)
                return 0
            lax.fori_loop(0, n_rest // u1, rest_body, 0)

    rows = 256

    def merge(i, _):
        sl = pl.ds(pl.multiple_of(i * rows, rows), rows)
        ms = [m_scr[g, sl, :] for g in range(N_DIL)]
        mx = functools.reduce(jnp.maximum, ms)
        es = [jnp.exp(m - mx) for m in ms]
        num = sum(e * o_scr[g, sl, :] for g, e in enumerate(es))
        den = sum(e * l_scr[g, sl, :] for g, e in enumerate(es))
        y_ref[sl, :] = (num / den).astype(y_ref.dtype)
        return 0
    lax.fori_loop(0, ATT_SPAN // rows, merge, 0)


def _dilated_attention(q, kv, bias, batch, seq):
    n = q.shape[0]
    nspan = seq // ATT_SPAN
    npair = DIL_HEADS // ATT_PAIR
    kvw = 2 * ATT_PW

    def prev_spec(g, window):
        per_span = ATT_SPAN // window
        rows_per_batch = seq // window
        return pl.BlockSpec(
            (window, kvw),
            lambda b, p, s: (jnp.maximum(b * rows_per_batch + s * per_span - 1, 0), p * N_DIL + g))

    return pl.pallas_call(
        _attn_kernel,
        grid=(batch, npair, nspan),
        in_specs=[pl.BlockSpec((ATT_SPAN, ATT_PW), lambda b, p, s, g=g: (b * nspan + s, p * N_DIL + g))
                  for g in range(N_DIL)] +
                 [pl.BlockSpec((ATT_SPAN, N_DIL * kvw), lambda b, p, s: (b * nspan + s, p))] +
                 [prev_spec(g, w) for g, (w, _) in enumerate(DIL_CONFIGS)] +
                 [pl.BlockSpec((1, N_DIL, CHUNK, 4 * CHUNK), lambda b, p, s: (p, 0, 0, 0))],
        out_specs=pl.BlockSpec((ATT_SPAN, ATT_PW), lambda b, p, s: (b * nspan + s, p)),
        out_shape=jax.ShapeDtypeStruct((n, DIL_W), BF16),
        scratch_shapes=[pltpu.VMEM((w + ATT_SPAN, ATT_PW), F32) for w, _ in DIL_CONFIGS] * 2 +
                       [pltpu.VMEM((N_DIL, ATT_SPAN, ATT_PW), F32) for _ in range(3)],
        compiler_params=_cparams(("parallel", "parallel", "arbitrary")),
        name="dilated_attention",
    )(q, q, q, kv, kv, kv, kv, bias)


def _attention_bias():
    L = CHUNK
    qi = jnp.arange(L)[:, None]
    kr = jnp.arange(2 * L)[None, :]
    steps = qi + L - kr
    band = (steps >= 0) & (steps <= L)
    slopes = jnp.exp2(-8.0 * jnp.arange(1, DIL_HEADS + 1, dtype=F32) / DIL_HEADS)
    per_group = []
    for _, dilation in DIL_CONFIGS:
        b = -slopes[:, None, None] * (steps * dilation).astype(F32)[None]
        b = jnp.where(band[None], b, NEG_INF)
        per_group.append(b.reshape(DIL_HEADS // ATT_PAIR, ATT_PAIR, L, 2 * L)
                         .transpose(0, 2, 1, 3).reshape(DIL_HEADS // ATT_PAIR, L, 4 * L))
    return jnp.stack(per_group, axis=1)


def _odd_weights(w):
    w = w.reshape(D_MODEL, N_DIL, 3, DIL_HEADS // ATT_PAIR, ATT_PAIR, DIL_DH)
    wq = (w[:, :, 0] * (DIL_DH ** -0.5)).transpose(0, 2, 1, 3, 4).reshape(D_MODEL, N_DIL * DIL_W)
    wkv = w[:, :, 1:3].transpose(0, 3, 1, 2, 4, 5).reshape(D_MODEL, 2 * N_DIL * DIL_W)
    return wq.astype(BF16), wkv.astype(BF16)


def _tile_copy(src_hbm, row, dst, i, sem):
    src = src_hbm.at[pl.ds(pl.multiple_of(row * TOK_TILES, TOK_TILES), TOK_TILES), :]
    return pltpu.make_async_copy(src, dst.at[pl.ds(i * TOK_TILES, TOK_TILES), :], sem)


META_PSTART, META_COUNT, META_NUSED, META_BLOCK_E = 0, 1, 2, 3
META_ROWS = SUBLANES


def _slot_plan_kernel(eidx_ref, rank_ref, cnt_ref, dest_ref, meta_ref, *, n_blocks):
    blk = float(MOE_BLK)
    hi = lax.Precision.HIGHEST
    cnt_b = cnt_ref[...].astype(F32)
    padded_b = jnp.floor((cnt_b + (blk - 1.0)) / blk) * blk
    er = lax.broadcasted_iota(jnp.int32, (LANES, LANES), 0)
    ec = lax.broadcasted_iota(jnp.int32, (LANES, LANES), 1)
    padded_full = jnp.concatenate([padded_b, jnp.zeros((LANES - N_EXPERTS, LANES), F32)], axis=0)
    pstart_b = jnp.dot((ec < er).astype(F32), padded_full, preferred_element_type=F32,
                       precision=hi)[0:N_EXPERTS, :]
    pend_b = pstart_b + padded_b

    sub = lax.broadcasted_iota(jnp.int32, (N_EXPERTS, eidx_ref.shape[1]), 0)
    pcol = pstart_b[:, 0:1]
    for k in range(TOP_K):
        start = jnp.sum(jnp.where(sub == eidx_ref[k:k + 1, :], pcol, 0.0), 0, keepdims=True)
        dest_ref[k:k + 1, :] = start.astype(jnp.int32) + rank_ref[k:k + 1, :]

    def as_row(col_b):
        full = jnp.concatenate([col_b, jnp.zeros((LANES - N_EXPERTS, LANES), F32)], axis=0)
        return full.T[0:1, :]
    meta_ref[...] = jnp.zeros_like(meta_ref)
    meta_ref[META_PSTART:META_PSTART + 1, :] = as_row(pstart_b).astype(jnp.int32)
    meta_ref[META_COUNT:META_COUNT + 1, :] = as_row(cnt_b).astype(jnp.int32)
    total = pend_b[N_EXPERTS - 1:N_EXPERTS, :]
    meta_ref[META_NUSED:META_NUSED + 1, :] = (total / blk).astype(jnp.int32)
    for r in range(-(-n_blocks // LANES)):
        first = (lax.broadcasted_iota(jnp.int32, (N_EXPERTS, LANES), 1) + r * LANES).astype(F32) * blk
        owner = jnp.sum((pend_b[:, 0:1] <= first).astype(jnp.int32), 0, keepdims=True)
        meta_ref[META_BLOCK_E + r:META_BLOCK_E + r + 1, :] = jnp.minimum(owner, N_EXPERTS - 1)


def _slot_plan(eidx, rank, counts, n_blocks):
    n = eidx.shape[1]
    assert META_BLOCK_E + -(-n_blocks // LANES) <= META_ROWS
    dest, meta = pl.pallas_call(
        functools.partial(_slot_plan_kernel, n_blocks=n_blocks),
        out_shape=[jax.ShapeDtypeStruct((TOP_K, n), jnp.int32),
                   jax.ShapeDtypeStruct((META_ROWS, LANES), jnp.int32)],
        compiler_params=pltpu.CompilerParams(vmem_limit_bytes=VMEM_LIMIT_BYTES),
        name="moe_slot_plan",
    )(eidx, rank, counts)
    pstart = meta[META_PSTART, :N_EXPERTS]
    cnt = meta[META_COUNT, :N_EXPERTS]
    nused = meta[META_NUSED, :1]
    block_e = meta[META_BLOCK_E:, :].reshape(-1)[:n_blocks]
    return dest.reshape(-1), pstart, cnt, block_e, nused


def _experts_kernel(dest_ref, pstart_ref, count_ref, block_e_ref, nused_ref,
                    h8_hbm, wg_ref, wu_ref, wd_ref, ys8_ref, slot_tok, xbuf, sems):
    j = pl.program_id(0)
    nused = nused_ref[0]
    n_tok = dest_ref.shape[0] // TOP_K

    @pl.when(j == 0)
    def _():
        def fill(t, _):
            for k in range(TOP_K):
                slot_tok[dest_ref[k * n_tok + t]] = t
            return 0
        lax.fori_loop(0, n_tok, fill, 0, unroll=8)

        def pad_expert(e, _):
            cnt = count_ref[e]
            padded = (cnt + MOE_BLK - 1) // MOE_BLK * MOE_BLK

            def pad(i, _):
                s = pstart_ref[e] + i
                slot_tok[s] = s % n_tok
                return 0
            lax.fori_loop(cnt, padded, pad, 0)
            return 0
        lax.fori_loop(0, N_EXPERTS, pad_expert, 0)

    def issue(blk, slot):
        for i in range(MOE_BLK):
            _tile_copy(h8_hbm, slot_tok[blk * MOE_BLK + i], xbuf.at[slot], i, sems.at[slot]).start()

    def wait(slot):
        for i in range(MOE_BLK):
            _tile_copy(h8_hbm, 0, xbuf.at[slot], i, sems.at[slot]).wait()

    ahead = MOE_NBUF - 1
    slot = j % MOE_NBUF

    @pl.when(jnp.logical_and(j == 0, nused > 0))
    def _():
        for a in range(ahead):
            issue(jnp.minimum(a, nused - 1), a)

    @pl.when(j < nused)
    def _():
        issue(jnp.minimum(j + ahead, nused - 1), (j + ahead) % MOE_NBUF)
        wait(slot)
        x = _load_token_tiles(xbuf.at[slot], MOE_BLK).astype(BF16)
        a = jnp.dot(x, wg_ref[0, 0].astype(BF16), preferred_element_type=F32)
        u = jnp.dot(x, wu_ref[0, 0].astype(BF16), preferred_element_type=F32)
        hm = (_silu(a) * u).astype(BF16)
        y = jnp.dot(hm, wd_ref[0, 0].astype(BF16), preferred_element_type=F32)
        _store_token_tiles(ys8_ref, y, MOE_BLK)

    @pl.when(j == nused - 1)
    def _():
        for a in range(1, MOE_NBUF):
            wait((slot + a) % MOE_NBUF)

    @pl.when(j >= nused)
    def _():
        ys8_ref[...] = jnp.zeros_like(ys8_ref)


def _experts(h8, dest, pstart, counts, block_e, nused, wg, wu, wd, layer, p_slots):
    n_blocks = p_slots // MOE_BLK
    wmap = lambda j, de, ps, ct, be, nu: (layer, be[j], 0, 0)
    grid_spec = pltpu.PrefetchScalarGridSpec(
        num_scalar_prefetch=5,
        grid=(n_blocks,),
        in_specs=[pl.BlockSpec(memory_space=pl.ANY),
                  pl.BlockSpec((1, 1, D_MODEL, D_EXPERT), wmap),
                  pl.BlockSpec((1, 1, D_MODEL, D_EXPERT), wmap),
                  pl.BlockSpec((1, 1, D_EXPERT, D_MODEL), wmap)],
        out_specs=pl.BlockSpec((MOE_BLK * TOK_TILES, LANES), lambda j, *_: (j, 0)),
        scratch_shapes=[pltpu.SMEM((p_slots,), jnp.int32),
                        pltpu.VMEM((MOE_NBUF, MOE_BLK * TOK_TILES, LANES), F32),
                        pltpu.SemaphoreType.DMA((MOE_NBUF,))],
    )
    return pl.pallas_call(
        _experts_kernel,
        grid_spec=grid_spec,
        out_shape=jax.ShapeDtypeStruct((p_slots * TOK_TILES, LANES), F32),
        compiler_params=_cparams(("arbitrary",)),
        name="moe_experts",
    )(dest, pstart, counts, block_e, nused, h8, wg, wu, wd)


def _combine_kernel(dest_ref, h8_ref, gcol_ref, ys8_hbm, lng_ref, lnb_ref, o_ref, buf, sems):
    i = pl.program_id(0)
    nt = pl.num_programs(0)
    n_tok = nt * COMB_TM

    def issue(tile, slot):
        for t in range(COMB_TM):
            for k in range(TOP_K):
                row = dest_ref[k * n_tok + tile * COMB_TM + t]
                _tile_copy(ys8_hbm, row, buf.at[slot, k], t, sems.at[slot]).start()

    def wait(slot):
        for t in range(COMB_TM):
            for k in range(TOP_K):
                _tile_copy(ys8_hbm, 0, buf.at[slot, k], t, sems.at[slot]).wait()

    slot = i % 2

    @pl.when(i == 0)
    def _():
        issue(0, 0)

    issue(jnp.minimum(i + 1, nt - 1), 1 - slot)
    wait(slot)
    g = gcol_ref[...]
    ffn = g[:, 0:1] * _load_token_tiles(buf.at[slot, 0], COMB_TM) + \
        g[:, 1:2] * _load_token_tiles(buf.at[slot, 1], COMB_TM)
    z = ALPHA * _load_token_tiles(h8_ref, COMB_TM) + ffn
    o_ref[...] = _layer_norm(z, lng_ref[...], lnb_ref[...])

    @pl.when(i == nt - 1)
    def _():
        wait(1 - slot)


def _combine(h8, gcol, ys8, dest, ln_g, ln_b):
    n = gcol.shape[0]
    const2 = lambda i, *_: (0, 0)
    grid_spec = pltpu.PrefetchScalarGridSpec(
        num_scalar_prefetch=1,
        grid=(n // COMB_TM,),
        in_specs=[pl.BlockSpec((COMB_TM * TOK_TILES, LANES), lambda i, *_: (i, 0)),
                  pl.BlockSpec((COMB_TM, LANES), lambda i, *_: (i, 0)),
                  pl.BlockSpec(memory_space=pl.ANY),
                  pl.BlockSpec((1, D_MODEL), const2),
                  pl.BlockSpec((1, D_MODEL), const2)],
        out_specs=pl.BlockSpec((COMB_TM, D_MODEL), lambda i, *_: (i, 0)),
        scratch_shapes=[pltpu.VMEM((2, TOP_K, COMB_TM * TOK_TILES, LANES), F32),
                        pltpu.SemaphoreType.DMA((2,))],
    )
    return pl.pallas_call(
        _combine_kernel,
        grid_spec=grid_spec,
        out_shape=jax.ShapeDtypeStruct((n, D_MODEL), F32),
        compiler_params=_cparams(("arbitrary",)),
        name="moe_combine",
    )(dest, h8, gcol, ys8, ln_g, ln_b)


def _moe_and_norm(h8, eidx, rank, gcol, counts, wg, wu, wd, layer, ln_g, ln_b):
    n = gcol.shape[0]
    p_slots = n * TOP_K + N_EXPERTS * MOE_BLK
    dest, pstart, cnt, block_e, nused = _slot_plan(eidx, rank, counts, p_slots // MOE_BLK)
    ys8 = _experts(h8, dest, pstart, cnt, block_e, nused, wg, wu, wd, layer, p_slots)
    return _combine(h8, gcol, ys8, dest, ln_g, ln_b)


def _retention_constants():
    log_gamma = jnp.log1p(-jnp.exp2(-5.0 - jnp.arange(RET_HEADS, dtype=F32)))
    idx = jnp.arange(CHUNK)
    rel = idx[:, None] - idx[None, :]
    decay_in = jnp.where(rel >= 0, jnp.exp(log_gamma[:, None, None] * jnp.maximum(rel, 0)), 0.0)
    xi = jnp.exp(log_gamma[:, None] * (idx + 1))
    zeta = jnp.exp(log_gamma[:, None] * (CHUNK - 1 - idx))
    g_chunk = jnp.exp(log_gamma * CHUNK)
    rc = jnp.zeros((CHUNK, LANES), F32)
    rc = rc.at[:, 0:RET_HEADS].set(xi.T)
    rc = rc.at[:, RET_HEADS:2 * RET_HEADS].set(zeta.T)
    rc = rc.at[:, 2 * RET_HEADS:3 * RET_HEADS].set(jnp.broadcast_to(g_chunk[None, :], (CHUNK, RET_HEADS)))
    return decay_in.astype(F32), rc


def kernel(x, w_in_even, i_bias, f_bias, conv_w, conv_b, mlstm_norm_g, ret_norm_g, w_out_even, w_in_odd,
           w_out_odd, router_w, router_bias, w_gate, w_up, w_down, ln_g, ln_b):
    batch, seq, _ = x.shape
    n = batch * seq
    assert all(w // d == CHUNK for w, d in DIL_CONFIGS) and seq % ATT_SPAN == 0
    h = x.reshape(n, D_MODEL)
    router_wt = router_w.T.astype(F32)
    router_b = router_bias.astype(F32).reshape(N_EXPERTS, 1)
    decay_in, ret_consts = _retention_constants()
    tri = (jnp.arange(POST_TM)[:, None] <= jnp.arange(POST_TM)[None, :]).astype(BF16)
    att_bias = _attention_bias()
    gate_col0 = 4 * MLSTM_W
    gate_col1 = gate_col0 + 2 * MLSTM_HEADS

    for layer in range(DEPTH):
        j = layer // 2
        lg = lambda s: ln_g[layer, s].reshape(1, D_MODEL)
        lb = lambda s: ln_b[layer, s].reshape(1, D_MODEL)
        if layer % 2 == 0:
            w = w_in_even[j]
            w_re = jnp.concatenate(
                [w[:, :gate_col0], w[:, gate_col1:], w[:, gate_col0:gate_col1],
                 jnp.zeros((D_MODEL, LANES - 2 * MLSTM_HEADS), w.dtype)], axis=1).astype(BF16)
            gate_bias = jnp.zeros((1, LANES), F32)
            gate_bias = gate_bias.at[0, 0:MLSTM_HEADS].set(i_bias[j])
            gate_bias = gate_bias.at[0, MLSTM_HEADS:2 * MLSTM_HEADS].set(f_bias[j])
            norm_g = jnp.concatenate([mlstm_norm_g[j], ret_norm_g[j]]).reshape(1, MLSTM_W + RET_W)
            y = _even_mixer(h, w_re, batch, seq, conv_w[j], conv_b[j].reshape(1, 2 * MLSTM_W), gate_bias,
                            decay_in, ret_consts, norm_g.astype(F32))
            w_out_all = w_out_even
        else:
            wq, wkv = _odd_weights(w_in_odd[j])
            q, kv = _project_qkv(h, jnp.concatenate([wq, wkv], axis=1), N_DIL * DIL_W)
            y = _dilated_attention(q, kv, att_bias, batch, seq)
            w_out_all = w_out_odd
        h8, eidx, rank, gcol, counts = _post_mixer(h, y, w_out_all, j, lg(0), lb(0),
                                                   router_wt, router_b, tri, f"post_{layer}")
        h = _moe_and_norm(h8, eidx, rank, gcol, counts, w_gate, w_up, w_down, layer, lg(1), lb(1))
    return h.reshape(batch, seq, D_MODEL)
```

```python
import functools

import jax
import jax.numpy as jnp
from jax import lax
from jax.experimental import pallas as pl
from jax.experimental.pallas import tpu as pltpu

F32 = jnp.float32
BF16 = jnp.bfloat16

D_MODEL = 1024
DEPTH = 2
ALPHA = (2 * DEPTH) ** 0.25
LN_EPS = 1e-5
MLSTM_HEADS = 4
RET_HEADS = 4
HEAD_DIM = 128
MLSTM_W = MLSTM_HEADS * HEAD_DIM
RET_W = RET_HEADS * HEAD_DIM
CHUNK = 128
CONV_WIDTH = 4
DIL_HEADS = 8
DIL_DH = 64
DIL_W = DIL_HEADS * DIL_DH
DIL_CONFIGS = ((128, 1), (512, 4), (2048, 16))
N_EXPERTS = 32
N_GROUPS = 4
EXPERTS_PER_GROUP = N_EXPERTS // N_GROUPS
TOP_K = 2
D_EXPERT = 256

LANES = 128
SUBLANES = 8
VMEM_LIMIT_BYTES = 56 * 1024 * 1024

PROJ_TM = 512
POST_TM = 512
MOE_BLK = 256
MOE_NBUF = 2
COMB_TM = 256
CONV_TAIL = SUBLANES

EV_QM, EV_KM, EV_VM, EV_OM = 0, MLSTM_W, 2 * MLSTM_W, 3 * MLSTM_W
EV_QR, EV_KR, EV_VR, EV_GR = (4 * MLSTM_W, 4 * MLSTM_W + RET_W, 4 * MLSTM_W + 2 * RET_W,
                              4 * MLSTM_W + 3 * RET_W)
EV_GATE = 4 * MLSTM_W + 4 * RET_W
EV_COLS = EV_GATE + LANES

NEG_INF = float("-inf")


def _cparams(sem):
    return pltpu.CompilerParams(dimension_semantics=sem, vmem_limit_bytes=VMEM_LIMIT_BYTES)


def _proj_kernel(x_ref, w_ref, o_ref):
    o_ref[...] = jnp.dot(x_ref[...].astype(BF16), w_ref[...], preferred_element_type=F32)


def _project(x, w, name):
    n, k = x.shape
    m = w.shape[1]
    return pl.pallas_call(
        _proj_kernel,
        grid=(n // PROJ_TM,),
        in_specs=[pl.BlockSpec((PROJ_TM, k), lambda i: (i, 0)),
                  pl.BlockSpec((k, m), lambda i: (0, 0))],
        out_specs=pl.BlockSpec((PROJ_TM, m), lambda i: (i, 0)),
        out_shape=jax.ShapeDtypeStruct((n, m), F32),
        compiler_params=_cparams(("parallel",)),
        name=name,
    )(x, w)


EV_STEP_CHUNKS = 2
EV_PROJ_PIECE = 2 * LANES


def _silu(x):
    return x * jax.nn.sigmoid(x)


def _head_norm(h):
    mu = jnp.mean(h, -1, keepdims=True)
    c = h - mu
    var = jnp.mean(c * c, -1, keepdims=True)
    return c * lax.rsqrt(var + LN_EPS)


def _dot_nt(a, b):
    return lax.dot_general(a, b, (((1,), (1,)), ((), ())), preferred_element_type=F32)


def _dot_tn(a_f32, b):
    return lax.dot_general(a_f32.astype(BF16), b, (((0,), (0,)), ((), ())), preferred_element_type=F32)


def _even_kernel(x0_ref, x_ref, w_ref, convw_ref, convb_ref, gbias_ref, dec_ref, rc_ref, ng_ref, y_ref,
                 p_scr, xb_scr, qkbuf, c_ref, n_ref, m_ref, r_ref, *, steps_per_batch):
    i = pl.program_id(0)

    @pl.when(i == 0)
    def _():
        p_scr[0] = jnp.dot(x0_ref[...].astype(BF16), w_ref[...], preferred_element_type=F32)

    @pl.when(i % steps_per_batch == 0)
    def _():
        qkbuf[0:CONV_TAIL, :] = jnp.zeros((CONV_TAIL, 2 * MLSTM_W), F32)
        c_ref[...] = jnp.zeros_like(c_ref)
        n_ref[...] = jnp.zeros_like(n_ref)
        m_ref[...] = jnp.zeros_like(m_ref)
        r_ref[...] = jnp.zeros_like(r_ref)

    xb_scr[...] = x_ref[...].astype(BF16)
    nxt = (i + 1) % 2
    pieces = iter(range(0, EV_COLS, EV_PROJ_PIECE))

    def project_piece():
        lo = next(pieces, None)
        if lo is not None:
            hi = min(lo + EV_PROJ_PIECE, EV_COLS)
            p_scr[nxt, :, lo:hi] = jnp.dot(xb_scr[...], w_ref[:, lo:hi], preferred_element_type=F32)

    cur = p_scr.at[i % 2]
    for c in range(EV_STEP_CHUNKS):
        rows = pl.ds(c * CHUNK, CHUNK)
        _mixer_chunk(cur.at[rows], y_ref.at[rows], convw_ref, convb_ref, gbias_ref, dec_ref, rc_ref, ng_ref,
                     qkbuf, c_ref, n_ref, m_ref, r_ref, project_piece)
    for _ in pieces:
        raise AssertionError("projection pieces left over: EV_PROJ_PIECE too small for the number of heads")


def _mixer_chunk(p_ref, y_ref, convw_ref, convb_ref, gbias_ref, dec_ref, rc_ref, ng_ref,
                 qkbuf, c_ref, n_ref, m_ref, r_ref, between):
    L = CHUNK
    scale = HEAD_DIM ** -0.5
    between()

    qkbuf[CONV_TAIL:CONV_TAIL + L, :] = p_ref[:, EV_QM:EV_QM + 2 * MLSTM_W]
    acc = jnp.broadcast_to(convb_ref[...], (L, 2 * MLSTM_W))
    for k in range(CONV_WIDTH):
        off = CONV_TAIL - (CONV_WIDTH - 1) + k
        acc = acc + convw_ref[k:k + 1, :] * qkbuf[off:off + L, :]
    qk = _silu(acc)
    qkbuf[0:CONV_TAIL, :] = qkbuf[L:L + CONV_TAIL, :]

    pre = p_ref[:, EV_GATE:EV_GATE + LANES] + gbias_ref[...]
    logf = jnp.minimum(pre, 0.0) - jnp.log1p(jnp.exp(-jnp.abs(pre)))
    row = lax.broadcasted_iota(jnp.int32, (L, L), 0)
    col = lax.broadcasted_iota(jnp.int32, (L, L), 1)
    causal = row >= col
    bcs = jnp.dot(causal.astype(F32), logf, preferred_element_type=F32,
                  precision=lax.Precision.HIGHEST)
    u_t = (pre - pltpu.roll(bcs, LANES - MLSTM_HEADS, 1)).T

    for h in range(MLSTM_HEADS):
        between()
        lo = h * HEAD_DIM
        b_col = bcs[:, MLSTM_HEADS + h:MLSTM_HEADS + h + 1]
        i_col = pre[:, h:h + 1]
        r_row = u_t[h:h + 1, :]
        m_prev = m_ref[h:h + 1, 0:1]
        log_d = jnp.where(causal, b_col + r_row, NEG_INF)
        a = b_col + m_prev
        m_t = jnp.maximum(a, jnp.max(log_d, -1, keepdims=True))
        d = jnp.exp(log_d - m_t)
        inter = jnp.exp(a - m_t)
        q = qk[:, lo:lo + HEAD_DIM]
        k = qk[:, MLSTM_W + lo:MLSTM_W + lo + HEAD_DIM] * scale
        qb = q.astype(BF16)
        vb = p_ref[:, EV_VM + lo:EV_VM + lo + HEAD_DIM].astype(BF16)
        s = _dot_nt(qb, k.astype(BF16)) * d
        c_old = c_ref[h]
        n_old = n_ref[h:h + 1, :]
        num = jnp.dot(s.astype(BF16), vb, preferred_element_type=F32) + \
            inter * jnp.dot(qb, c_old.astype(BF16), preferred_element_type=F32)
        den = jnp.sum(s, -1, keepdims=True) + inter * jnp.sum(q * n_old, -1, keepdims=True)
        hh = num / jnp.maximum(jnp.abs(den), jnp.exp(-m_t))
        g = b_col[L - 1:L, :]
        log_w = g - b_col + i_col
        m_new = jnp.maximum(g + m_prev, jnp.max(log_w, 0, keepdims=True))
        kw = k * jnp.exp(log_w - m_new)
        decay = jnp.exp(g + m_prev - m_new)
        c_ref[h] = decay * c_old + _dot_tn(kw, vb)
        n_ref[h:h + 1, :] = decay * n_old + jnp.sum(kw, 0, keepdims=True)
        m_ref[h:h + 1, :] = jnp.broadcast_to(m_new, (1, LANES))
        o_gate = jax.nn.sigmoid(p_ref[:, EV_OM + lo:EV_OM + lo + HEAD_DIM])
        y_ref[:, lo:lo + HEAD_DIM] = (_head_norm(o_gate * hh) * ng_ref[:, lo:lo + HEAD_DIM]).astype(y_ref.dtype)

    for h in range(RET_HEADS):
        between()
        lo = h * HEAD_DIM
        qb = p_ref[:, EV_QR + lo:EV_QR + lo + HEAD_DIM].astype(BF16)
        k = p_ref[:, EV_KR + lo:EV_KR + lo + HEAD_DIM] * scale
        vb = p_ref[:, EV_VR + lo:EV_VR + lo + HEAD_DIM].astype(BF16)
        xi = rc_ref[:, h:h + 1]
        zeta = rc_ref[:, RET_HEADS + h:RET_HEADS + h + 1]
        g_chunk = rc_ref[0:1, 2 * RET_HEADS + h:2 * RET_HEADS + h + 1]
        r_old = r_ref[h]
        inner = _dot_nt(qb, k.astype(BF16)) * dec_ref[h]
        y = jnp.dot(inner.astype(BF16), vb, preferred_element_type=F32) + \
            xi * jnp.dot(qb, r_old.astype(BF16), preferred_element_type=F32)
        r_ref[h] = g_chunk * r_old + _dot_tn(k * zeta, vb)
        gr = p_ref[:, EV_GR + lo:EV_GR + lo + HEAD_DIM]
        out = _head_norm(y) * ng_ref[:, MLSTM_W + lo:MLSTM_W + lo + HEAD_DIM] * _silu(gr)
        y_ref[:, MLSTM_W + lo:MLSTM_W + lo + HEAD_DIM] = out.astype(y_ref.dtype)


def _even_mixer(x, w, batch, seq, conv_w, conv_b, gate_bias, decay_in, ret_consts, norm_g):
    tm = EV_STEP_CHUNKS * CHUNK
    steps_per_batch = seq // tm
    nsteps = batch * steps_per_batch
    const2 = lambda i: (0, 0)
    return pl.pallas_call(
        functools.partial(_even_kernel, steps_per_batch=steps_per_batch),
        grid=(nsteps,),
        in_specs=[pl.BlockSpec((tm, D_MODEL), const2),
                  pl.BlockSpec((tm, D_MODEL), lambda i: (jnp.minimum(i + 1, nsteps - 1), 0)),
                  pl.BlockSpec((D_MODEL, EV_COLS), const2),
                  pl.BlockSpec((CONV_WIDTH, 2 * MLSTM_W), const2),
                  pl.BlockSpec((1, 2 * MLSTM_W), const2),
                  pl.BlockSpec((1, LANES), const2),
                  pl.BlockSpec((RET_HEADS, CHUNK, CHUNK), lambda i: (0, 0, 0)),
                  pl.BlockSpec((CHUNK, LANES), const2),
                  pl.BlockSpec((1, MLSTM_W + RET_W), const2)],
        out_specs=pl.BlockSpec((tm, MLSTM_W + RET_W), lambda i: (i, 0)),
        out_shape=jax.ShapeDtypeStruct((batch * seq, MLSTM_W + RET_W), BF16),
        scratch_shapes=[pltpu.VMEM((2, tm, EV_COLS), F32),
                        pltpu.VMEM((tm, D_MODEL), BF16),
                        pltpu.VMEM((CONV_TAIL + CHUNK, 2 * MLSTM_W), F32),
                        pltpu.VMEM((MLSTM_HEADS, HEAD_DIM, HEAD_DIM), F32),
                        pltpu.VMEM((SUBLANES, HEAD_DIM), F32),
                        pltpu.VMEM((SUBLANES, LANES), F32),
                        pltpu.VMEM((RET_HEADS, HEAD_DIM, HEAD_DIM), F32)],
        compiler_params=_cparams(("arbitrary",)),
        name="even_mixer",
    )(x, x, w, conv_w, conv_b, gate_bias, decay_in, ret_consts, norm_g)


TOK_TILES = D_MODEL // LANES


def _store_token_tiles(ref, val, rows):
    for c in range(TOK_TILES):
        ref[pl.ds(c, rows, stride=TOK_TILES), :] = val[:, c * LANES:(c + 1) * LANES]


def _load_token_tiles(ref, rows):
    return jnp.concatenate([ref[pl.ds(c, rows, stride=TOK_TILES), :] for c in range(TOK_TILES)], axis=1)


def _layer_norm(z, g, b):
    mu = jnp.mean(z, -1, keepdims=True)
    c = z - mu
    var = jnp.mean(c * c, -1, keepdims=True)
    return c * lax.rsqrt(var + LN_EPS) * g + b


def _route(h, rwt_ref, rb_ref):
    h_hi = h.astype(BF16)
    h_lo = (h - h_hi.astype(F32)).astype(BF16)
    logits = _dot_nt(rwt_ref[0], h_hi) + (_dot_nt(rwt_ref[0], h_lo) + _dot_nt(rwt_ref[1], h_hi))
    aff = jax.nn.sigmoid(logits)
    sel = aff + rb_ref[...]
    tm = h.shape[0]
    sub = lax.broadcasted_iota(jnp.int32, (EXPERTS_PER_GROUP, tm), 0)
    best = None
    for g in range(N_GROUPS):
        lo = g * EXPERTS_PER_GROUP
        sg = sel[lo:lo + EXPERTS_PER_GROUP, :]
        ag = aff[lo:lo + EXPERTS_PER_GROUP, :]
        v1 = jnp.max(sg, 0, keepdims=True)
        i1 = jnp.min(jnp.where(sg == v1, sub, EXPERTS_PER_GROUP), 0, keepdims=True)
        rest = jnp.where(sub == i1, NEG_INF, sg)
        v2 = jnp.max(rest, 0, keepdims=True)
        i2 = jnp.min(jnp.where(rest == v2, sub, EXPERTS_PER_GROUP), 0, keepdims=True)
        a1 = jnp.sum(jnp.where(sub == i1, ag, 0.0), 0, keepdims=True)
        a2 = jnp.sum(jnp.where(sub == i2, ag, 0.0), 0, keepdims=True)
        cand = (v1 + v2, i1 + lo, i2 + lo, a1, a2)
        if best is None:
            best = cand
        else:
            take = cand[0] > best[0]
            best = tuple(jnp.where(take, c, b) for c, b in zip(cand, best))
    _, e1, e2, a1, a2 = best
    tot = a1 + a2
    return e1, e2, a1 / tot, a2 / tot


def _post_kernel(x_ref, y_ref, wo_ref, lng_ref, lnb_ref, rwt_ref, rb_ref, tri_ref,
                 h8_ref, eidx_ref, rank_ref, gcol_ref, cnt_ref, run_ref, wo_scr):
    tm = x_ref.shape[0]

    @pl.when(pl.program_id(0) == 0)
    def _():
        run_ref[...] = jnp.zeros_like(run_ref)
        wo_scr[...] = wo_ref[0].astype(BF16)

    mix = jnp.dot(y_ref[...], wo_scr[...], preferred_element_type=F32)
    h = _layer_norm(ALPHA * x_ref[...] + mix, lng_ref[...], lnb_ref[...])
    _store_token_tiles(h8_ref, h, tm)

    e1, e2, g1, g2 = _route(h, rwt_ref, rb_ref)
    eidx_ref[0:1, :] = e1
    eidx_ref[1:2, :] = e2
    gates = jnp.concatenate([g1, g2, jnp.zeros((LANES - TOP_K, tm), F32)], axis=0)
    gcol_ref[...] = gates.T

    sub = lax.broadcasted_iota(jnp.int32, (N_EXPERTS, tm), 0)
    oh1 = sub == e1
    oh2 = sub == e2
    chosen = jnp.logical_or(oh1, oh2).astype(F32)
    incl = jnp.dot(chosen.astype(BF16), tri_ref[...], preferred_element_type=F32)
    before = run_ref[:, 0:1] + incl - chosen
    rank_ref[0:1, :] = jnp.sum(jnp.where(oh1, before, 0.0), 0, keepdims=True).astype(jnp.int32)
    rank_ref[1:2, :] = jnp.sum(jnp.where(oh2, before, 0.0), 0, keepdims=True).astype(jnp.int32)
    run_new = run_ref[...] + incl[:, tm - 1:tm]
    run_ref[...] = run_new
    cnt_ref[...] = run_new.astype(jnp.int32)


def _post_mixer(x, y, w_out_all, j, ln_g, ln_b, router_wt, router_b, tri, name):
    n = x.shape[0]
    tm = POST_TM
    kdim = w_out_all.shape[1]
    const2 = lambda i: (0, 0)
    return pl.pallas_call(
        _post_kernel,
        grid=(n // tm,),
        in_specs=[pl.BlockSpec((tm, D_MODEL), lambda i: (i, 0)),
                  pl.BlockSpec((tm, y.shape[1]), lambda i: (i, 0)),
                  pl.BlockSpec((1, kdim, D_MODEL), lambda i: (j, 0, 0)),
                  pl.BlockSpec((1, D_MODEL), const2),
                  pl.BlockSpec((1, D_MODEL), const2),
                  pl.BlockSpec((2, N_EXPERTS, D_MODEL), lambda i: (0, 0, 0)),
                  pl.BlockSpec((N_EXPERTS, 1), const2),
                  pl.BlockSpec((tm, tm), const2)],
        out_specs=[pl.BlockSpec((tm * TOK_TILES, LANES), lambda i: (i, 0)),
                   pl.BlockSpec((TOP_K, tm), lambda i: (0, i)),
                   pl.BlockSpec((TOP_K, tm), lambda i: (0, i)),
                   pl.BlockSpec((tm, LANES), lambda i: (i, 0)),
                   pl.BlockSpec((N_EXPERTS, LANES), const2)],
        out_shape=[jax.ShapeDtypeStruct((n * TOK_TILES, LANES), F32),
                   jax.ShapeDtypeStruct((TOP_K, n), jnp.int32),
                   jax.ShapeDtypeStruct((TOP_K, n), jnp.int32),
                   jax.ShapeDtypeStruct((n, LANES), F32),
                   jax.ShapeDtypeStruct((N_EXPERTS, LANES), jnp.int32)],
        scratch_shapes=[pltpu.VMEM((N_EXPERTS, LANES), F32),
                        pltpu.VMEM((kdim, D_MODEL), BF16)],
        compiler_params=_cparams(("arbitrary",)),
        name=name,
    )(x, y, w_out_all, ln_g, ln_b, router_wt, router_b, tri)


ATT_SPAN = max(w for w, _ in DIL_CONFIGS)
ATT_PAIR = 2
ATT_PW = ATT_PAIR * DIL_DH
N_DIL = len(DIL_CONFIGS)


def _attn_group_size(n):
    return 3 if n % 3 == 0 else (4 if n % 4 == 0 else 1)


def _attn_rows(start, stride):
    L = CHUNK
    if stride == 1:
        start = pl.multiple_of(start, L)
        return pl.ds(start, L), pl.ds(start, 2 * L)
    return pl.ds(start, L, stride=stride), pl.ds(start, 2 * L, stride=stride)


def _attn_units(q_ref, kbuf, vbuf, bias, g, starts, stride, noprev, o_scr, m_scr, l_scr):
    rows = [_attn_rows(s, stride) for s in starts]
    loaded = [(q_ref[qr, :].astype(BF16),
               kbuf[kvr, :].astype(BF16),
               vbuf[kvr, :].astype(BF16)) for qr, kvr in rows]
    outs = [_attn_block(q2, kb, vb, bias, noprev) for q2, kb, vb in loaded]
    for (qr, _), (o, mm, ll) in zip(rows, outs):
        o_scr[g, qr, :] = o
        m_scr[g, qr, :] = mm
        l_scr[g, qr, :] = ll


def _attn_block(q2, kb, vb, bias, noprev):
    L = CHUNK
    head0 = lax.broadcasted_iota(jnp.int32, (2 * L, ATT_PW), 1) < DIL_DH
    zero = jnp.zeros_like(kb)
    k2 = jnp.concatenate([jnp.where(head0, kb, zero), jnp.where(head0, zero, kb)], axis=0)
    v2 = jnp.concatenate([jnp.where(head0, vb, zero), jnp.where(head0, zero, vb)], axis=0)
    s = _dot_nt(q2, k2) + bias
    if noprev is not None:
        col = lax.broadcasted_iota(jnp.int32, (L, 4 * L), 1)
        prev_col = (col % (2 * L)) < L
        s = jnp.where(jnp.logical_and(prev_col, noprev), NEG_INF, s)
    s0, s1 = s[:, 0:2 * L], s[:, 2 * L:4 * L]
    m0 = jnp.max(s0, -1, keepdims=True)
    m1 = jnp.max(s1, -1, keepdims=True)
    p0 = jnp.exp(s0 - m0)
    p1 = jnp.exp(s1 - m1)
    l0 = jnp.sum(p0, -1, keepdims=True)
    l1 = jnp.sum(p1, -1, keepdims=True)
    o = jnp.dot(jnp.concatenate([p0, p1], axis=1).astype(BF16), v2, preferred_element_type=F32)
    left = lax.broadcasted_iota(jnp.int32, (L, ATT_PW), 1) < DIL_DH
    return o, jnp.where(left, m0, m1), jnp.where(left, l0, l1)


def _attn_kernel(*refs):
    q_refs, kc_refs, vc_refs, kp_refs, vp_refs = (refs[i * N_DIL:(i + 1) * N_DIL] for i in range(5))
    bias_ref, y_ref = refs[5 * N_DIL:5 * N_DIL + 2]
    scr = refs[5 * N_DIL + 2:]
    kbufs, vbufs = scr[0:N_DIL], scr[N_DIL:2 * N_DIL]
    o_scr, m_scr, l_scr = scr[2 * N_DIL:]
    first = pl.program_id(2) == 0
    for g, (window, _) in enumerate(DIL_CONFIGS):
        kbufs[g][0:window, :] = kp_refs[g][...]
        vbufs[g][0:window, :] = vp_refs[g][...]
        kbufs[g][window:window + ATT_SPAN, :] = kc_refs[g][...]
        vbufs[g][window:window + ATT_SPAN, :] = vc_refs[g][...]

    for g, (window, dilation) in enumerate(DIL_CONFIGS):
        bias = bias_ref[0, g]
        nsub = ATT_SPAN // window

        u0 = _attn_group_size(dilation)

        def first_body(t, _, g=g, bias=bias, dilation=dilation, u0=u0):
            starts = [t * u0 + u for u in range(u0)]
            _attn_units(q_refs[g], kbufs[g], vbufs[g], bias, g, starts, dilation, first, o_scr, m_scr, l_scr)
            return 0
        lax.fori_loop(0, dilation // u0, first_body, 0)

        if nsub > 1:
            n_rest = (nsub - 1) * dilation
            u1 = _attn_group_size(n_rest)

            def rest_body(t, _, g=g, bias=bias, window=window, dilation=dilation, u1=u1):
                idx = [t * u1 + u for u in range(u1)]
                starts = [(1 + i // dilation) * window + i % dilation for i in idx]
                _attn_units(q_refs[g], kbufs[g], vbufs[g], bias, g, starts, dilation, None, o_scr, m_scr, l_scr)
                return 0
            lax.fori_loop(0, n_rest // u1, rest_body, 0)

    rows = 256

    def merge(i, _):
        sl = pl.ds(pl.multiple_of(i * rows, rows), rows)
        ms = [m_scr[g, sl, :] for g in range(N_DIL)]
        mx = functools.reduce(jnp.maximum, ms)
        es = [jnp.exp(m - mx) for m in ms]
        num = sum(e * o_scr[g, sl, :] for g, e in enumerate(es))
        den = sum(e * l_scr[g, sl, :] for g, e in enumerate(es))
        y_ref[sl, :] = (num / den).astype(y_ref.dtype)
        return 0
    lax.fori_loop(0, ATT_SPAN // rows, merge, 0)


def _dilated_attention(p, bias, batch, seq):
    n = p.shape[0]
    nspan = seq // ATT_SPAN
    npair = DIL_HEADS // ATT_PAIR
    slab = DIL_W // ATT_PW

    def col(g, c, pair):
        return (g * 3 + c) * slab + pair

    def cur_spec(g, c):
        return pl.BlockSpec((ATT_SPAN, ATT_PW), lambda b, pr, s: (b * nspan + s, col(g, c, pr)))

    def prev_spec(g, c):
        window = DIL_CONFIGS[g][0]
        per_span = ATT_SPAN // window
        rows_per_batch = seq // window
        return pl.BlockSpec(
            (window, ATT_PW),
            lambda b, pr, s: (jnp.maximum(b * rows_per_batch + s * per_span - 1, 0), col(g, c, pr)))

    groups = range(N_DIL)
    in_specs = ([cur_spec(g, 0) for g in groups] + [cur_spec(g, 1) for g in groups] +
                [cur_spec(g, 2) for g in groups] + [prev_spec(g, 1) for g in groups] +
                [prev_spec(g, 2) for g in groups] +
                [pl.BlockSpec((1, N_DIL, CHUNK, 4 * CHUNK), lambda b, pr, s: (pr, 0, 0, 0))])
    return pl.pallas_call(
        _attn_kernel,
        grid=(batch, npair, nspan),
        in_specs=in_specs,
        out_specs=pl.BlockSpec((ATT_SPAN, ATT_PW), lambda b, pr, s: (b * nspan + s, pr)),
        out_shape=jax.ShapeDtypeStruct((n, DIL_W), BF16),
        scratch_shapes=[pltpu.VMEM((w + ATT_SPAN, ATT_PW), F32) for w, _ in DIL_CONFIGS] * 2 +
                       [pltpu.VMEM((N_DIL, ATT_SPAN, ATT_PW), F32) for _ in range(3)],
        compiler_params=_cparams(("parallel", "parallel", "arbitrary")),
        name="dilated_attention",
    )(*([p] * (5 * N_DIL)), bias)


def _attention_bias():
    L = CHUNK
    qi = jnp.arange(L)[:, None]
    kr = jnp.arange(2 * L)[None, :]
    steps = qi + L - kr
    band = (steps >= 0) & (steps <= L)
    slopes = jnp.exp2(-8.0 * jnp.arange(1, DIL_HEADS + 1, dtype=F32) / DIL_HEADS)
    per_group = []
    for _, dilation in DIL_CONFIGS:
        b = -slopes[:, None, None] * (steps * dilation).astype(F32)[None]
        b = jnp.where(band[None], b, NEG_INF)
        per_group.append(b.reshape(DIL_HEADS // ATT_PAIR, ATT_PAIR, L, 2 * L)
                         .transpose(0, 2, 1, 3).reshape(DIL_HEADS // ATT_PAIR, L, 4 * L))
    return jnp.stack(per_group, axis=1)


def _odd_weights(w):
    is_q = (jnp.arange(w.shape[1]) // DIL_W) % 3 == 0
    return (w * jnp.where(is_q, DIL_DH ** -0.5, 1.0)[None, :]).astype(BF16)


def _tile_copy(src_hbm, row, dst, i, sem):
    src = src_hbm.at[pl.ds(pl.multiple_of(row * TOK_TILES, TOK_TILES), TOK_TILES), :]
    return pltpu.make_async_copy(src, dst.at[pl.ds(i * TOK_TILES, TOK_TILES), :], sem)


META_PSTART, META_COUNT, META_NUSED, META_BLOCK_E = 0, 1, 2, 3
META_ROWS = SUBLANES


def _slot_plan_kernel(eidx_ref, rank_ref, cnt_ref, dest_ref, meta_ref, *, n_blocks):
    blk = float(MOE_BLK)
    hi = lax.Precision.HIGHEST
    cnt_b = cnt_ref[...].astype(F32)
    padded_b = jnp.floor((cnt_b + (blk - 1.0)) / blk) * blk
    er = lax.broadcasted_iota(jnp.int32, (LANES, LANES), 0)
    ec = lax.broadcasted_iota(jnp.int32, (LANES, LANES), 1)
    padded_full = jnp.concatenate([padded_b, jnp.zeros((LANES - N_EXPERTS, LANES), F32)], axis=0)
    pstart_b = jnp.dot((ec < er).astype(F32), padded_full, preferred_element_type=F32,
                       precision=hi)[0:N_EXPERTS, :]
    pend_b = pstart_b + padded_b

    sub = lax.broadcasted_iota(jnp.int32, (N_EXPERTS, eidx_ref.shape[1]), 0)
    pcol = pstart_b[:, 0:1]
    for k in range(TOP_K):
        start = jnp.sum(jnp.where(sub == eidx_ref[k:k + 1, :], pcol, 0.0), 0, keepdims=True)
        dest_ref[k:k + 1, :] = start.astype(jnp.int32) + rank_ref[k:k + 1, :]

    def as_row(col_b):
        full = jnp.concatenate([col_b, jnp.zeros((LANES - N_EXPERTS, LANES), F32)], axis=0)
        return full.T[0:1, :]
    meta_ref[...] = jnp.zeros_like(meta_ref)
    meta_ref[META_PSTART:META_PSTART + 1, :] = as_row(pstart_b).astype(jnp.int32)
    meta_ref[META_COUNT:META_COUNT + 1, :] = as_row(cnt_b).astype(jnp.int32)
    total = pend_b[N_EXPERTS - 1:N_EXPERTS, :]
    meta_ref[META_NUSED:META_NUSED + 1, :] = (total / blk).astype(jnp.int32)
    for r in range(-(-n_blocks // LANES)):
        first = (lax.broadcasted_iota(jnp.int32, (N_EXPERTS, LANES), 1) + r * LANES).astype(F32) * blk
        owner = jnp.sum((pend_b[:, 0:1] <= first).astype(jnp.int32), 0, keepdims=True)
        meta_ref[META_BLOCK_E + r:META_BLOCK_E + r + 1, :] = jnp.minimum(owner, N_EXPERTS - 1)


def _slot_plan(eidx, rank, counts, n_blocks):
    n = eidx.shape[1]
    assert META_BLOCK_E + -(-n_blocks // LANES) <= META_ROWS
    dest, meta = pl.pallas_call(
        functools.partial(_slot_plan_kernel, n_blocks=n_blocks),
        out_shape=[jax.ShapeDtypeStruct((TOP_K, n), jnp.int32),
                   jax.ShapeDtypeStruct((META_ROWS, LANES), jnp.int32)],
        compiler_params=pltpu.CompilerParams(vmem_limit_bytes=VMEM_LIMIT_BYTES),
        name="moe_slot_plan",
    )(eidx, rank, counts)
    pstart = meta[META_PSTART, :N_EXPERTS]
    cnt = meta[META_COUNT, :N_EXPERTS]
    nused = meta[META_NUSED, :1]
    block_e = meta[META_BLOCK_E:, :].reshape(-1)[:n_blocks]
    return dest.reshape(-1), pstart, cnt, block_e, nused


def _experts_kernel(dest_ref, pstart_ref, count_ref, block_e_ref, nused_ref,
                    h8_hbm, wg_ref, wu_ref, wd_ref, ys8_ref, slot_tok, xbuf, sems, wg_b, wu_b, wd_b):
    j = pl.program_id(0)
    nused = nused_ref[0]
    n_tok = dest_ref.shape[0] // TOP_K

    @pl.when(j == 0)
    def _():
        def fill(t, _):
            for k in range(TOP_K):
                slot_tok[dest_ref[k * n_tok + t]] = t
            return 0
        lax.fori_loop(0, n_tok, fill, 0, unroll=8)

        def pad_expert(e, _):
            cnt = count_ref[e]
            padded = (cnt + MOE_BLK - 1) // MOE_BLK * MOE_BLK

            def pad(i, _):
                s = pstart_ref[e] + i
                slot_tok[s] = s % n_tok
                return 0
            lax.fori_loop(cnt, padded, pad, 0)
            return 0
        lax.fori_loop(0, N_EXPERTS, pad_expert, 0)

    def issue(blk, slot):
        for i in range(MOE_BLK):
            _tile_copy(h8_hbm, slot_tok[blk * MOE_BLK + i], xbuf.at[slot], i, sems.at[slot]).start()

    def wait(slot):
        for i in range(MOE_BLK):
            _tile_copy(h8_hbm, 0, xbuf.at[slot], i, sems.at[slot]).wait()

    ahead = MOE_NBUF - 1
    slot = j % MOE_NBUF

    @pl.when(jnp.logical_and(j == 0, nused > 0))
    def _():
        for a in range(ahead):
            issue(jnp.minimum(a, nused - 1), a)

    @pl.when(j < nused)
    def _():
        issue(jnp.minimum(j + ahead, nused - 1), (j + ahead) % MOE_NBUF)

        @pl.when(jnp.logical_or(j == 0, block_e_ref[j] != block_e_ref[jnp.maximum(j - 1, 0)]))
        def _():
            wg_b[...] = wg_ref[0, 0].astype(BF16)
            wu_b[...] = wu_ref[0, 0].astype(BF16)
            wd_b[...] = wd_ref[0, 0].astype(BF16)

        wait(slot)
        x = _load_token_tiles(xbuf.at[slot], MOE_BLK).astype(BF16)
        a = jnp.dot(x, wg_b[...], preferred_element_type=F32)
        u = jnp.dot(x, wu_b[...], preferred_element_type=F32)
        hm = (_silu(a) * u).astype(BF16)
        y = jnp.dot(hm, wd_b[...], preferred_element_type=F32)
        _store_token_tiles(ys8_ref, y, MOE_BLK)

    @pl.when(j == nused - 1)
    def _():
        for a in range(1, MOE_NBUF):
            wait((slot + a) % MOE_NBUF)

    @pl.when(j >= nused)
    def _():
        ys8_ref[...] = jnp.zeros_like(ys8_ref)


def _experts(h8, dest, pstart, counts, block_e, nused, wg, wu, wd, layer, p_slots):
    n_blocks = p_slots // MOE_BLK
    wmap = lambda j, de, ps, ct, be, nu: (layer, be[j], 0, 0)
    grid_spec = pltpu.PrefetchScalarGridSpec(
        num_scalar_prefetch=5,
        grid=(n_blocks,),
        in_specs=[pl.BlockSpec(memory_space=pl.ANY),
                  pl.BlockSpec((1, 1, D_MODEL, D_EXPERT), wmap),
                  pl.BlockSpec((1, 1, D_MODEL, D_EXPERT), wmap),
                  pl.BlockSpec((1, 1, D_EXPERT, D_MODEL), wmap)],
        out_specs=pl.BlockSpec((MOE_BLK * TOK_TILES, LANES), lambda j, *_: (j, 0)),
        scratch_shapes=[pltpu.SMEM((p_slots,), jnp.int32),
                        pltpu.VMEM((MOE_NBUF, MOE_BLK * TOK_TILES, LANES), F32),
                        pltpu.SemaphoreType.DMA((MOE_NBUF,)),
                        pltpu.VMEM((D_MODEL, D_EXPERT), BF16),
                        pltpu.VMEM((D_MODEL, D_EXPERT), BF16),
                        pltpu.VMEM((D_EXPERT, D_MODEL), BF16)],
    )
    return pl.pallas_call(
        _experts_kernel,
        grid_spec=grid_spec,
        out_shape=jax.ShapeDtypeStruct((p_slots * TOK_TILES, LANES), F32),
        compiler_params=_cparams(("arbitrary",)),
        name="moe_experts",
    )(dest, pstart, counts, block_e, nused, h8, wg, wu, wd)


def _combine_kernel(dest_ref, h8_ref, gcol_ref, ys8_hbm, lng_ref, lnb_ref, o_ref, buf, sems):
    i = pl.program_id(0)
    nt = pl.num_programs(0)
    n_tok = nt * COMB_TM

    def issue(tile, slot):
        for t in range(COMB_TM):
            for k in range(TOP_K):
                row = dest_ref[k * n_tok + tile * COMB_TM + t]
                _tile_copy(ys8_hbm, row, buf.at[slot, k], t, sems.at[slot]).start()

    def wait(slot):
        for t in range(COMB_TM):
            for k in range(TOP_K):
                _tile_copy(ys8_hbm, 0, buf.at[slot, k], t, sems.at[slot]).wait()

    slot = i % 2

    @pl.when(i == 0)
    def _():
        issue(0, 0)

    issue(jnp.minimum(i + 1, nt - 1), 1 - slot)
    wait(slot)
    g = gcol_ref[...]
    ffn = g[:, 0:1] * _load_token_tiles(buf.at[slot, 0], COMB_TM) + \
        g[:, 1:2] * _load_token_tiles(buf.at[slot, 1], COMB_TM)
    z = ALPHA * _load_token_tiles(h8_ref, COMB_TM) + ffn
    o_ref[...] = _layer_norm(z, lng_ref[...], lnb_ref[...])

    @pl.when(i == nt - 1)
    def _():
        wait(1 - slot)


def _combine(h8, gcol, ys8, dest, ln_g, ln_b):
    n = gcol.shape[0]
    const2 = lambda i, *_: (0, 0)
    grid_spec = pltpu.PrefetchScalarGridSpec(
        num_scalar_prefetch=1,
        grid=(n // COMB_TM,),
        in_specs=[pl.BlockSpec((COMB_TM * TOK_TILES, LANES), lambda i, *_: (i, 0)),
                  pl.BlockSpec((COMB_TM, LANES), lambda i, *_: (i, 0)),
                  pl.BlockSpec(memory_space=pl.ANY),
                  pl.BlockSpec((1, D_MODEL), const2),
                  pl.BlockSpec((1, D_MODEL), const2)],
        out_specs=pl.BlockSpec((COMB_TM, D_MODEL), lambda i, *_: (i, 0)),
        scratch_shapes=[pltpu.VMEM((2, TOP_K, COMB_TM * TOK_TILES, LANES), F32),
                        pltpu.SemaphoreType.DMA((2,))],
    )
    return pl.pallas_call(
        _combine_kernel,
        grid_spec=grid_spec,
        out_shape=jax.ShapeDtypeStruct((n, D_MODEL), F32),
        compiler_params=_cparams(("arbitrary",)),
        name="moe_combine",
    )(dest, h8, gcol, ys8, ln_g, ln_b)


def _moe_and_norm(h8, eidx, rank, gcol, counts, wg, wu, wd, layer, ln_g, ln_b):
    n = gcol.shape[0]
    p_slots = n * TOP_K + N_EXPERTS * MOE_BLK
    dest, pstart, cnt, block_e, nused = _slot_plan(eidx, rank, counts, p_slots // MOE_BLK)
    ys8 = _experts(h8, dest, pstart, cnt, block_e, nused, wg, wu, wd, layer, p_slots)
    return _combine(h8, gcol, ys8, dest, ln_g, ln_b)


def _retention_constants():
    log_gamma = jnp.log1p(-jnp.exp2(-5.0 - jnp.arange(RET_HEADS, dtype=F32)))
    idx = jnp.arange(CHUNK)
    rel = idx[:, None] - idx[None, :]
    decay_in = jnp.where(rel >= 0, jnp.exp(log_gamma[:, None, None] * jnp.maximum(rel, 0)), 0.0)
    xi = jnp.exp(log_gamma[:, None] * (idx + 1))
    zeta = jnp.exp(log_gamma[:, None] * (CHUNK - 1 - idx))
    g_chunk = jnp.exp(log_gamma * CHUNK)
    rc = jnp.zeros((CHUNK, LANES), F32)
    rc = rc.at[:, 0:RET_HEADS].set(xi.T)
    rc = rc.at[:, RET_HEADS:2 * RET_HEADS].set(zeta.T)
    rc = rc.at[:, 2 * RET_HEADS:3 * RET_HEADS].set(jnp.broadcast_to(g_chunk[None, :], (CHUNK, RET_HEADS)))
    return decay_in.astype(F32), rc


def kernel(x, w_in_even, i_bias, f_bias, conv_w, conv_b, mlstm_norm_g, ret_norm_g, w_out_even, w_in_odd,
           w_out_odd, router_w, router_bias, w_gate, w_up, w_down, ln_g, ln_b):
    batch, seq, _ = x.shape
    n = batch * seq
    assert all(w // d == CHUNK for w, d in DIL_CONFIGS) and seq % ATT_SPAN == 0
    h = x.reshape(n, D_MODEL)
    rw_t = router_w.T.astype(F32)
    rw_hi = rw_t.astype(BF16)
    router_wt = jnp.stack([rw_hi, (rw_t - rw_hi.astype(F32)).astype(BF16)])
    router_b = router_bias.astype(F32).reshape(N_EXPERTS, 1)
    decay_in, ret_consts = _retention_constants()
    tri = (jnp.arange(POST_TM)[:, None] <= jnp.arange(POST_TM)[None, :]).astype(BF16)
    att_bias = _attention_bias()
    gate_col0 = 4 * MLSTM_W
    gate_col1 = gate_col0 + 2 * MLSTM_HEADS

    for layer in range(DEPTH):
        j = layer // 2
        lg = lambda s: ln_g[layer, s].reshape(1, D_MODEL)
        lb = lambda s: ln_b[layer, s].reshape(1, D_MODEL)
        if layer % 2 == 0:
            w = w_in_even[j]
            w_re = jnp.concatenate(
                [w[:, :gate_col0], w[:, gate_col1:], w[:, gate_col0:gate_col1],
                 jnp.zeros((D_MODEL, LANES - 2 * MLSTM_HEADS), w.dtype)], axis=1).astype(BF16)
            gate_bias = jnp.zeros((1, LANES), F32)
            gate_bias = gate_bias.at[0, 0:MLSTM_HEADS].set(i_bias[j])
            gate_bias = gate_bias.at[0, MLSTM_HEADS:2 * MLSTM_HEADS].set(f_bias[j])
            norm_g = jnp.concatenate([mlstm_norm_g[j], ret_norm_g[j]]).reshape(1, MLSTM_W + RET_W)
            y = _even_mixer(h, w_re, batch, seq, conv_w[j], conv_b[j].reshape(1, 2 * MLSTM_W), gate_bias,
                            decay_in, ret_consts, norm_g.astype(F32))
            w_out_all = w_out_even
        else:
            p = _project(h, _odd_weights(w_in_odd[j]), "proj_odd")
            y = _dilated_attention(p, att_bias, batch, seq)
            w_out_all = w_out_odd
        h8, eidx, rank, gcol, counts = _post_mixer(h, y, w_out_all, j, lg(0), lb(0),
                                                   router_wt, router_b, tri, f"post_{layer}")
        h = _moe_and_norm(h8, eidx, rank, gcol, counts, w_gate, w_up, w_down, layer, lg(1), lb(1))
    return h.reshape(batch, seq, D_MODEL)
```

```python
import functools

import jax
import jax.numpy as jnp
from jax import lax
from jax.experimental import pallas as pl
from jax.experimental.pallas import tpu as pltpu

F32 = jnp.float32
BF16 = jnp.bfloat16

D_MODEL = 1024
DEPTH = 2
ALPHA = (2 * DEPTH) ** 0.25
LN_EPS = 1e-5
MLSTM_HEADS = 4
RET_HEADS = 4
HEAD_DIM = 128
MLSTM_W = MLSTM_HEADS * HEAD_DIM
RET_W = RET_HEADS * HEAD_DIM
CHUNK = 128
CONV_WIDTH = 4
DIL_HEADS = 8
DIL_DH = 64
DIL_W = DIL_HEADS * DIL_DH
DIL_CONFIGS = ((128, 1), (512, 4), (2048, 16))
N_EXPERTS = 32
N_GROUPS = 4
EXPERTS_PER_GROUP = N_EXPERTS // N_GROUPS
TOP_K = 2
D_EXPERT = 256

LANES = 128
SUBLANES = 8
VMEM_LIMIT_BYTES = 56 * 1024 * 1024

PROJ_TM = 512
POST_TM = 1024
MOE_BLK = 256
MOE_NBUF = 2
COMB_TM = 256
CONV_TAIL = SUBLANES

EV_QM, EV_KM, EV_VM, EV_OM = 0, MLSTM_W, 2 * MLSTM_W, 3 * MLSTM_W
EV_QR, EV_KR, EV_VR, EV_GR = (4 * MLSTM_W, 4 * MLSTM_W + RET_W, 4 * MLSTM_W + 2 * RET_W,
                              4 * MLSTM_W + 3 * RET_W)
EV_GATE = 4 * MLSTM_W + 4 * RET_W
EV_COLS = EV_GATE + LANES

NEG_INF = float("-inf")


def _cparams(sem):
    return pltpu.CompilerParams(dimension_semantics=sem, vmem_limit_bytes=VMEM_LIMIT_BYTES)


def _proj_kernel(x_ref, w_ref, o_ref):
    o_ref[...] = jnp.dot(x_ref[...].astype(BF16), w_ref[...], preferred_element_type=F32)


def _project(x, w, name):
    n, k = x.shape
    m = w.shape[1]
    return pl.pallas_call(
        _proj_kernel,
        grid=(n // PROJ_TM,),
        in_specs=[pl.BlockSpec((PROJ_TM, k), lambda i: (i, 0)),
                  pl.BlockSpec((k, m), lambda i: (0, 0))],
        out_specs=pl.BlockSpec((PROJ_TM, m), lambda i: (i, 0)),
        out_shape=jax.ShapeDtypeStruct((n, m), F32),
        compiler_params=_cparams(("parallel",)),
        name=name,
    )(x, w)


EV_STEP_CHUNKS = 2
EV_PROJ_PIECE = 2 * LANES


def _silu(x):
    return x * jax.nn.sigmoid(x)


def _head_norm(h):
    mu = jnp.mean(h, -1, keepdims=True)
    c = h - mu
    var = jnp.mean(c * c, -1, keepdims=True)
    return c * lax.rsqrt(var + LN_EPS)


def _dot_nt(a, b):
    return lax.dot_general(a, b, (((1,), (1,)), ((), ())), preferred_element_type=F32)


def _dot_tn(a_f32, b):
    return lax.dot_general(a_f32.astype(BF16), b, (((0,), (0,)), ((), ())), preferred_element_type=F32)


def _even_kernel(x0_ref, x_ref, w_ref, convw_ref, convb_ref, gbias_ref, dec_ref, rc_ref, ng_ref, y_ref,
                 p_scr, xb_scr, qkbuf, c_ref, n_ref, m_ref, r_ref, *, steps_per_batch):
    i = pl.program_id(0)

    @pl.when(i == 0)
    def _():
        p_scr[0] = jnp.dot(x0_ref[...].astype(BF16), w_ref[...], preferred_element_type=F32)

    @pl.when(i % steps_per_batch == 0)
    def _():
        qkbuf[0:CONV_TAIL, :] = jnp.zeros((CONV_TAIL, 2 * MLSTM_W), F32)
        c_ref[...] = jnp.zeros_like(c_ref)
        n_ref[...] = jnp.zeros_like(n_ref)
        m_ref[...] = jnp.zeros_like(m_ref)
        r_ref[...] = jnp.zeros_like(r_ref)

    xb_scr[...] = x_ref[...].astype(BF16)
    nxt = (i + 1) % 2
    pieces = iter(range(0, EV_COLS, EV_PROJ_PIECE))

    def project_piece():
        lo = next(pieces, None)
        if lo is not None:
            hi = min(lo + EV_PROJ_PIECE, EV_COLS)
            p_scr[nxt, :, lo:hi] = jnp.dot(xb_scr[...], w_ref[:, lo:hi], preferred_element_type=F32)

    cur = p_scr.at[i % 2]
    for c in range(EV_STEP_CHUNKS):
        rows = pl.ds(c * CHUNK, CHUNK)
        _mixer_chunk(cur.at[rows], y_ref.at[rows], convw_ref, convb_ref, gbias_ref, dec_ref, rc_ref, ng_ref,
                     qkbuf, c_ref, n_ref, m_ref, r_ref, project_piece)
    for _ in pieces:
        raise AssertionError("projection pieces left over: EV_PROJ_PIECE too small for the number of heads")


def _mixer_chunk(p_ref, y_ref, convw_ref, convb_ref, gbias_ref, dec_ref, rc_ref, ng_ref,
                 qkbuf, c_ref, n_ref, m_ref, r_ref, between):
    L = CHUNK
    scale = HEAD_DIM ** -0.5
    between()

    qkbuf[CONV_TAIL:CONV_TAIL + L, :] = p_ref[:, EV_QM:EV_QM + 2 * MLSTM_W]
    acc = jnp.broadcast_to(convb_ref[...], (L, 2 * MLSTM_W))
    for k in range(CONV_WIDTH):
        off = CONV_TAIL - (CONV_WIDTH - 1) + k
        acc = acc + convw_ref[k:k + 1, :] * qkbuf[off:off + L, :]
    qk = _silu(acc)
    qkbuf[0:CONV_TAIL, :] = qkbuf[L:L + CONV_TAIL, :]

    pre = p_ref[:, EV_GATE:EV_GATE + LANES] + gbias_ref[...]
    logf = jnp.minimum(pre, 0.0) - jnp.log1p(jnp.exp(-jnp.abs(pre)))
    row = lax.broadcasted_iota(jnp.int32, (L, L), 0)
    col = lax.broadcasted_iota(jnp.int32, (L, L), 1)
    causal = row >= col
    bcs = jnp.dot(causal.astype(F32), logf, preferred_element_type=F32,
                  precision=lax.Precision.HIGHEST)
    u_t = (pre - pltpu.roll(bcs, LANES - MLSTM_HEADS, 1)).T

    for h in range(MLSTM_HEADS):
        between()
        lo = h * HEAD_DIM
        b_col = bcs[:, MLSTM_HEADS + h:MLSTM_HEADS + h + 1]
        i_col = pre[:, h:h + 1]
        r_row = u_t[h:h + 1, :]
        m_prev = m_ref[h:h + 1, 0:1]
        log_d = jnp.where(causal, b_col + r_row, NEG_INF)
        a = b_col + m_prev
        m_t = jnp.maximum(a, jnp.max(log_d, -1, keepdims=True))
        d = jnp.exp(log_d - m_t)
        inter = jnp.exp(a - m_t)
        q = qk[:, lo:lo + HEAD_DIM]
        k = qk[:, MLSTM_W + lo:MLSTM_W + lo + HEAD_DIM] * scale
        qb = q.astype(BF16)
        vb = p_ref[:, EV_VM + lo:EV_VM + lo + HEAD_DIM].astype(BF16)
        s = _dot_nt(qb, k.astype(BF16)) * d
        c_old = c_ref[h]
        n_old = n_ref[h:h + 1, :]
        num = jnp.dot(s.astype(BF16), vb, preferred_element_type=F32) + \
            inter * jnp.dot(qb, c_old.astype(BF16), preferred_element_type=F32)
        den = jnp.sum(s, -1, keepdims=True) + inter * jnp.sum(q * n_old, -1, keepdims=True)
        hh = num / jnp.maximum(jnp.abs(den), jnp.exp(-m_t))
        g = b_col[L - 1:L, :]
        log_w = g - b_col + i_col
        m_new = jnp.maximum(g + m_prev, jnp.max(log_w, 0, keepdims=True))
        kw = k * jnp.exp(log_w - m_new)
        decay = jnp.exp(g + m_prev - m_new)
        c_ref[h] = decay * c_old + _dot_tn(kw, vb)
        n_ref[h:h + 1, :] = decay * n_old + jnp.sum(kw, 0, keepdims=True)
        m_ref[h:h + 1, :] = jnp.broadcast_to(m_new, (1, LANES))
        o_gate = jax.nn.sigmoid(p_ref[:, EV_OM + lo:EV_OM + lo + HEAD_DIM])
        y_ref[:, lo:lo + HEAD_DIM] = (_head_norm(o_gate * hh) * ng_ref[:, lo:lo + HEAD_DIM]).astype(y_ref.dtype)

    for h in range(RET_HEADS):
        between()
        lo = h * HEAD_DIM
        qb = p_ref[:, EV_QR + lo:EV_QR + lo + HEAD_DIM].astype(BF16)
        k = p_ref[:, EV_KR + lo:EV_KR + lo + HEAD_DIM] * scale
        vb = p_ref[:, EV_VR + lo:EV_VR + lo + HEAD_DIM].astype(BF16)
        xi = rc_ref[:, h:h + 1]
        zeta = rc_ref[:, RET_HEADS + h:RET_HEADS + h + 1]
        g_chunk = rc_ref[0:1, 2 * RET_HEADS + h:2 * RET_HEADS + h + 1]
        r_old = r_ref[h]
        inner = _dot_nt(qb, k.astype(BF16)) * dec_ref[h]
        y = jnp.dot(inner.astype(BF16), vb, preferred_element_type=F32) + \
            xi * jnp.dot(qb, r_old.astype(BF16), preferred_element_type=F32)
        r_ref[h] = g_chunk * r_old + _dot_tn(k * zeta, vb)
        gr = p_ref[:, EV_GR + lo:EV_GR + lo + HEAD_DIM]
        out = _head_norm(y) * ng_ref[:, MLSTM_W + lo:MLSTM_W + lo + HEAD_DIM] * _silu(gr)
        y_ref[:, MLSTM_W + lo:MLSTM_W + lo + HEAD_DIM] = out.astype(y_ref.dtype)


def _even_mixer(x, w, batch, seq, conv_w, conv_b, gate_bias, decay_in, ret_consts, norm_g):
    tm = EV_STEP_CHUNKS * CHUNK
    steps_per_batch = seq // tm
    nsteps = batch * steps_per_batch
    const2 = lambda i: (0, 0)
    return pl.pallas_call(
        functools.partial(_even_kernel, steps_per_batch=steps_per_batch),
        grid=(nsteps,),
        in_specs=[pl.BlockSpec((tm, D_MODEL), const2),
                  pl.BlockSpec((tm, D_MODEL), lambda i: (jnp.minimum(i + 1, nsteps - 1), 0)),
                  pl.BlockSpec((D_MODEL, EV_COLS), const2),
                  pl.BlockSpec((CONV_WIDTH, 2 * MLSTM_W), const2),
                  pl.BlockSpec((1, 2 * MLSTM_W), const2),
                  pl.BlockSpec((1, LANES), const2),
                  pl.BlockSpec((RET_HEADS, CHUNK, CHUNK), lambda i: (0, 0, 0)),
                  pl.BlockSpec((CHUNK, LANES), const2),
                  pl.BlockSpec((1, MLSTM_W + RET_W), const2)],
        out_specs=pl.BlockSpec((tm, MLSTM_W + RET_W), lambda i: (i, 0)),
        out_shape=jax.ShapeDtypeStruct((batch * seq, MLSTM_W + RET_W), BF16),
        scratch_shapes=[pltpu.VMEM((2, tm, EV_COLS), F32),
                        pltpu.VMEM((tm, D_MODEL), BF16),
                        pltpu.VMEM((CONV_TAIL + CHUNK, 2 * MLSTM_W), F32),
                        pltpu.VMEM((MLSTM_HEADS, HEAD_DIM, HEAD_DIM), F32),
                        pltpu.VMEM((SUBLANES, HEAD_DIM), F32),
                        pltpu.VMEM((SUBLANES, LANES), F32),
                        pltpu.VMEM((RET_HEADS, HEAD_DIM, HEAD_DIM), F32)],
        compiler_params=_cparams(("arbitrary",)),
        name="even_mixer",
    )(x, x, w, conv_w, conv_b, gate_bias, decay_in, ret_consts, norm_g)


TOK_TILES = D_MODEL // LANES


def _store_token_tiles(ref, val, rows):
    for c in range(TOK_TILES):
        ref[pl.ds(c, rows, stride=TOK_TILES), :] = val[:, c * LANES:(c + 1) * LANES]


def _load_token_tiles(ref, rows):
    return jnp.concatenate([ref[pl.ds(c, rows, stride=TOK_TILES), :] for c in range(TOK_TILES)], axis=1)


def _layer_norm(z, g, b):
    mu = jnp.mean(z, -1, keepdims=True)
    c = z - mu
    var = jnp.mean(c * c, -1, keepdims=True)
    return c * lax.rsqrt(var + LN_EPS) * g + b


def _route(h, rwt_ref, rb_ref):
    h_hi = h.astype(BF16)
    h_lo = (h - h_hi.astype(F32)).astype(BF16)
    logits = _dot_nt(rwt_ref[0], h_hi) + (_dot_nt(rwt_ref[0], h_lo) + _dot_nt(rwt_ref[1], h_hi))
    aff = jax.nn.sigmoid(logits)
    sel = aff + rb_ref[...]
    tm = h.shape[0]
    sub = lax.broadcasted_iota(jnp.int32, (EXPERTS_PER_GROUP, tm), 0)
    best = None
    for g in range(N_GROUPS):
        lo = g * EXPERTS_PER_GROUP
        sg = sel[lo:lo + EXPERTS_PER_GROUP, :]
        ag = aff[lo:lo + EXPERTS_PER_GROUP, :]
        v1 = jnp.max(sg, 0, keepdims=True)
        i1 = jnp.min(jnp.where(sg == v1, sub, EXPERTS_PER_GROUP), 0, keepdims=True)
        rest = jnp.where(sub == i1, NEG_INF, sg)
        v2 = jnp.max(rest, 0, keepdims=True)
        i2 = jnp.min(jnp.where(rest == v2, sub, EXPERTS_PER_GROUP), 0, keepdims=True)
        a1 = jnp.sum(jnp.where(sub == i1, ag, 0.0), 0, keepdims=True)
        a2 = jnp.sum(jnp.where(sub == i2, ag, 0.0), 0, keepdims=True)
        cand = (v1 + v2, i1 + lo, i2 + lo, a1, a2)
        if best is None:
            best = cand
        else:
            take = cand[0] > best[0]
            best = tuple(jnp.where(take, c, b) for c, b in zip(cand, best))
    _, e1, e2, a1, a2 = best
    tot = a1 + a2
    return e1, e2, a1 / tot, a2 / tot


def _post_kernel(x_ref, y_ref, wo_ref, lng_ref, lnb_ref, rwt_ref, rb_ref, tri_ref,
                 h8_ref, eidx_ref, rank_ref, gcol_ref, cnt_ref, run_ref, wo_scr):
    tm = x_ref.shape[0]

    @pl.when(pl.program_id(0) == 0)
    def _():
        run_ref[...] = jnp.zeros_like(run_ref)
        wo_scr[...] = wo_ref[0].astype(BF16)

    mix = jnp.dot(y_ref[...], wo_scr[...], preferred_element_type=F32)
    h = _layer_norm(ALPHA * x_ref[...] + mix, lng_ref[...], lnb_ref[...])
    _store_token_tiles(h8_ref, h, tm)

    e1, e2, g1, g2 = _route(h, rwt_ref, rb_ref)
    eidx_ref[0:1, :] = e1
    eidx_ref[1:2, :] = e2
    gates = jnp.concatenate([g1, g2, jnp.zeros((LANES - TOP_K, tm), F32)], axis=0)
    gcol_ref[...] = gates.T

    sub = lax.broadcasted_iota(jnp.int32, (N_EXPERTS, tm), 0)
    oh1 = sub == e1
    oh2 = sub == e2
    chosen = jnp.logical_or(oh1, oh2).astype(F32)
    incl = jnp.dot(chosen.astype(BF16), tri_ref[...], preferred_element_type=F32)
    before = run_ref[:, 0:1] + incl - chosen
    rank_ref[0:1, :] = jnp.sum(jnp.where(oh1, before, 0.0), 0, keepdims=True).astype(jnp.int32)
    rank_ref[1:2, :] = jnp.sum(jnp.where(oh2, before, 0.0), 0, keepdims=True).astype(jnp.int32)
    run_new = run_ref[...] + incl[:, tm - 1:tm]
    run_ref[...] = run_new
    cnt_ref[...] = run_new.astype(jnp.int32)


def _post_mixer(x, y, w_out_all, j, ln_g, ln_b, router_wt, router_b, tri, name):
    n = x.shape[0]
    tm = POST_TM
    kdim = w_out_all.shape[1]
    const2 = lambda i: (0, 0)
    return pl.pallas_call(
        _post_kernel,
        grid=(n // tm,),
        in_specs=[pl.BlockSpec((tm, D_MODEL), lambda i: (i, 0)),
                  pl.BlockSpec((tm, y.shape[1]), lambda i: (i, 0)),
                  pl.BlockSpec((1, kdim, D_MODEL), lambda i: (j, 0, 0)),
                  pl.BlockSpec((1, D_MODEL), const2),
                  pl.BlockSpec((1, D_MODEL), const2),
                  pl.BlockSpec((2, N_EXPERTS, D_MODEL), lambda i: (0, 0, 0)),
                  pl.BlockSpec((N_EXPERTS, 1), const2),
                  pl.BlockSpec((tm, tm), const2)],
        out_specs=[pl.BlockSpec((tm * TOK_TILES, LANES), lambda i: (i, 0)),
                   pl.BlockSpec((TOP_K, tm), lambda i: (0, i)),
                   pl.BlockSpec((TOP_K, tm), lambda i: (0, i)),
                   pl.BlockSpec((tm, LANES), lambda i: (i, 0)),
                   pl.BlockSpec((N_EXPERTS, LANES), const2)],
        out_shape=[jax.ShapeDtypeStruct((n * TOK_TILES, LANES), F32),
                   jax.ShapeDtypeStruct((TOP_K, n), jnp.int32),
                   jax.ShapeDtypeStruct((TOP_K, n), jnp.int32),
                   jax.ShapeDtypeStruct((n, LANES), F32),
                   jax.ShapeDtypeStruct((N_EXPERTS, LANES), jnp.int32)],
        scratch_shapes=[pltpu.VMEM((N_EXPERTS, LANES), F32),
                        pltpu.VMEM((kdim, D_MODEL), BF16)],
        compiler_params=_cparams(("arbitrary",)),
        name=name,
    )(x, y, w_out_all, ln_g, ln_b, router_wt, router_b, tri)


ATT_SPAN = max(w for w, _ in DIL_CONFIGS)
ATT_PAIR = 2
ATT_PW = ATT_PAIR * DIL_DH
N_DIL = len(DIL_CONFIGS)


def _attn_group_size(n):
    return 3 if n % 3 == 0 else (4 if n % 4 == 0 else 1)


def _attn_rows(start, stride):
    L = CHUNK
    if stride == 1:
        start = pl.multiple_of(start, L)
        return pl.ds(start, L), pl.ds(start, 2 * L)
    return pl.ds(start, L, stride=stride), pl.ds(start, 2 * L, stride=stride)


def _attn_units(q_ref, kbuf, vbuf, bias, g, starts, stride, noprev, o_scr, m_scr, l_scr):
    rows = [_attn_rows(s, stride) for s in starts]
    loaded = [(q_ref[qr, :].astype(BF16),
               kbuf[kvr, :].astype(BF16),
               vbuf[kvr, :].astype(BF16)) for qr, kvr in rows]
    outs = [_attn_block(q2, kb, vb, bias, noprev) for q2, kb, vb in loaded]
    for (qr, _), (o, mm, ll) in zip(rows, outs):
        o_scr[g, qr, :] = o
        m_scr[g, qr, :] = mm
        l_scr[g, qr, :] = ll


def _attn_block(q2, kb, vb, bias, noprev):
    L = CHUNK
    head0 = lax.broadcasted_iota(jnp.int32, (2 * L, ATT_PW), 1) < DIL_DH
    zero = jnp.zeros_like(kb)
    k2 = jnp.concatenate([jnp.where(head0, kb, zero), jnp.where(head0, zero, kb)], axis=0)
    v2 = jnp.concatenate([jnp.where(head0, vb, zero), jnp.where(head0, zero, vb)], axis=0)
    s = _dot_nt(q2, k2) + bias
    if noprev is not None:
        col = lax.broadcasted_iota(jnp.int32, (L, 4 * L), 1)
        prev_col = (col % (2 * L)) < L
        s = jnp.where(jnp.logical_and(prev_col, noprev), NEG_INF, s)
    s0, s1 = s[:, 0:2 * L], s[:, 2 * L:4 * L]
    m0 = jnp.max(s0, -1, keepdims=True)
    m1 = jnp.max(s1, -1, keepdims=True)
    p0 = jnp.exp(s0 - m0)
    p1 = jnp.exp(s1 - m1)
    l0 = jnp.sum(p0, -1, keepdims=True)
    l1 = jnp.sum(p1, -1, keepdims=True)
    o = jnp.dot(jnp.concatenate([p0, p1], axis=1).astype(BF16), v2, preferred_element_type=F32)
    left = lax.broadcasted_iota(jnp.int32, (L, ATT_PW), 1) < DIL_DH
    return o, jnp.where(left, m0, m1), jnp.where(left, l0, l1)


def _attn_kernel(*refs):
    q_refs, kc_refs, vc_refs, kp_refs, vp_refs = (refs[i * N_DIL:(i + 1) * N_DIL] for i in range(5))
    bias_ref, y_ref = refs[5 * N_DIL:5 * N_DIL + 2]
    scr = refs[5 * N_DIL + 2:]
    kbufs, vbufs = scr[0:N_DIL], scr[N_DIL:2 * N_DIL]
    o_scr, m_scr, l_scr = scr[2 * N_DIL:]
    first = pl.program_id(2) == 0
    for g, (window, _) in enumerate(DIL_CONFIGS):
        kbufs[g][0:window, :] = kp_refs[g][...]
        vbufs[g][0:window, :] = vp_refs[g][...]
        kbufs[g][window:window + ATT_SPAN, :] = kc_refs[g][...]
        vbufs[g][window:window + ATT_SPAN, :] = vc_refs[g][...]

    for g, (window, dilation) in enumerate(DIL_CONFIGS):
        bias = bias_ref[0, g]
        nsub = ATT_SPAN // window

        u0 = _attn_group_size(dilation)

        def first_body(t, _, g=g, bias=bias, dilation=dilation, u0=u0):
            starts = [t * u0 + u for u in range(u0)]
            _attn_units(q_refs[g], kbufs[g], vbufs[g], bias, g, starts, dilation, first, o_scr, m_scr, l_scr)
            return 0
        lax.fori_loop(0, dilation // u0, first_body, 0)

        if nsub > 1:
            n_rest = (nsub - 1) * dilation
            u1 = _attn_group_size(n_rest)

            def rest_body(t, _, g=g, bias=bias, window=window, dilation=dilation, u1=u1):
                idx = [t * u1 + u for u in range(u1)]
                starts = [(1 + i // dilation) * window + i % dilation for i in idx]
                _attn_units(q_refs[g], kbufs[g], vbufs[g], bias, g, starts, dilation, None, o_scr, m_scr, l_scr)
                return 0
            lax.fori_loop(0, n_rest // u1, rest_body, 0)

    rows = 256

    def merge(i, _):
        sl = pl.ds(pl.multiple_of(i * rows, rows), rows)
        ms = [m_scr[g, sl, :] for g in range(N_DIL)]
        mx = functools.reduce(jnp.maximum, ms)
        es = [jnp.exp(m - mx) for m in ms]
        num = sum(e * o_scr[g, sl, :] for g, e in enumerate(es))
        den = sum(e * l_scr[g, sl, :] for g, e in enumerate(es))
        y_ref[sl, :] = (num / den).astype(y_ref.dtype)
        return 0
    lax.fori_loop(0, ATT_SPAN // rows, merge, 0)


def _dilated_attention(p, bias, batch, seq):
    n = p.shape[0]
    nspan = seq // ATT_SPAN
    npair = DIL_HEADS // ATT_PAIR
    slab = DIL_W // ATT_PW

    def col(g, c, pair):
        return (g * 3 + c) * slab + pair

    def cur_spec(g, c):
        return pl.BlockSpec((ATT_SPAN, ATT_PW), lambda b, pr, s: (b * nspan + s, col(g, c, pr)))

    def prev_spec(g, c):
        window = DIL_CONFIGS[g][0]
        per_span = ATT_SPAN // window
        rows_per_batch = seq // window
        return pl.BlockSpec(
            (window, ATT_PW),
            lambda b, pr, s: (jnp.maximum(b * rows_per_batch + s * per_span - 1, 0), col(g, c, pr)))

    groups = range(N_DIL)
    in_specs = ([cur_spec(g, 0) for g in groups] + [cur_spec(g, 1) for g in groups] +
                [cur_spec(g, 2) for g in groups] + [prev_spec(g, 1) for g in groups] +
                [prev_spec(g, 2) for g in groups] +
                [pl.BlockSpec((1, N_DIL, CHUNK, 4 * CHUNK), lambda b, pr, s: (pr, 0, 0, 0))])
    return pl.pallas_call(
        _attn_kernel,
        grid=(batch, npair, nspan),
        in_specs=in_specs,
        out_specs=pl.BlockSpec((ATT_SPAN, ATT_PW), lambda b, pr, s: (b * nspan + s, pr)),
        out_shape=jax.ShapeDtypeStruct((n, DIL_W), BF16),
        scratch_shapes=[pltpu.VMEM((w + ATT_SPAN, ATT_PW), F32) for w, _ in DIL_CONFIGS] * 2 +
                       [pltpu.VMEM((N_DIL, ATT_SPAN, ATT_PW), F32) for _ in range(3)],
        compiler_params=_cparams(("parallel", "parallel", "arbitrary")),
        name="dilated_attention",
    )(*([p] * (5 * N_DIL)), bias)


def _attention_bias():
    L = CHUNK
    qi = jnp.arange(L)[:, None]
    kr = jnp.arange(2 * L)[None, :]
    steps = qi + L - kr
    band = (steps >= 0) & (steps <= L)
    slopes = jnp.exp2(-8.0 * jnp.arange(1, DIL_HEADS + 1, dtype=F32) / DIL_HEADS)
    per_group = []
    for _, dilation in DIL_CONFIGS:
        b = -slopes[:, None, None] * (steps * dilation).astype(F32)[None]
        b = jnp.where(band[None], b, NEG_INF)
        per_group.append(b.reshape(DIL_HEADS // ATT_PAIR, ATT_PAIR, L, 2 * L)
                         .transpose(0, 2, 1, 3).reshape(DIL_HEADS // ATT_PAIR, L, 4 * L))
    return jnp.stack(per_group, axis=1)


def _odd_weights(w):
    is_q = (jnp.arange(w.shape[1]) // DIL_W) % 3 == 0
    return (w * jnp.where(is_q, DIL_DH ** -0.5, 1.0)[None, :]).astype(BF16)


def _tile_copy(src_hbm, row, dst, i, sem):
    src = src_hbm.at[pl.ds(pl.multiple_of(row * TOK_TILES, TOK_TILES), TOK_TILES), :]
    return pltpu.make_async_copy(src, dst.at[pl.ds(i * TOK_TILES, TOK_TILES), :], sem)


META_PSTART, META_COUNT, META_NUSED, META_BLOCK_E = 0, 1, 2, 3
META_ROWS = SUBLANES


def _slot_plan_kernel(eidx_ref, rank_ref, cnt_ref, dest_ref, meta_ref, *, n_blocks):
    blk = float(MOE_BLK)
    hi = lax.Precision.HIGHEST
    cnt_b = cnt_ref[...].astype(F32)
    padded_b = jnp.floor((cnt_b + (blk - 1.0)) / blk) * blk
    er = lax.broadcasted_iota(jnp.int32, (LANES, LANES), 0)
    ec = lax.broadcasted_iota(jnp.int32, (LANES, LANES), 1)
    padded_full = jnp.concatenate([padded_b, jnp.zeros((LANES - N_EXPERTS, LANES), F32)], axis=0)
    pstart_b = jnp.dot((ec < er).astype(F32), padded_full, preferred_element_type=F32,
                       precision=hi)[0:N_EXPERTS, :]
    pend_b = pstart_b + padded_b

    sub = lax.broadcasted_iota(jnp.int32, (N_EXPERTS, eidx_ref.shape[1]), 0)
    pcol = pstart_b[:, 0:1]
    for k in range(TOP_K):
        start = jnp.sum(jnp.where(sub == eidx_ref[k:k + 1, :], pcol, 0.0), 0, keepdims=True)
        dest_ref[k:k + 1, :] = start.astype(jnp.int32) + rank_ref[k:k + 1, :]

    def as_row(col_b):
        full = jnp.concatenate([col_b, jnp.zeros((LANES - N_EXPERTS, LANES), F32)], axis=0)
        return full.T[0:1, :]
    meta_ref[...] = jnp.zeros_like(meta_ref)
    meta_ref[META_PSTART:META_PSTART + 1, :] = as_row(pstart_b).astype(jnp.int32)
    meta_ref[META_COUNT:META_COUNT + 1, :] = as_row(cnt_b).astype(jnp.int32)
    total = pend_b[N_EXPERTS - 1:N_EXPERTS, :]
    meta_ref[META_NUSED:META_NUSED + 1, :] = (total / blk).astype(jnp.int32)
    for r in range(-(-n_blocks // LANES)):
        first = (lax.broadcasted_iota(jnp.int32, (N_EXPERTS, LANES), 1) + r * LANES).astype(F32) * blk
        owner = jnp.sum((pend_b[:, 0:1] <= first).astype(jnp.int32), 0, keepdims=True)
        meta_ref[META_BLOCK_E + r:META_BLOCK_E + r + 1, :] = jnp.minimum(owner, N_EXPERTS - 1)


def _slot_plan(eidx, rank, counts, n_blocks):
    n = eidx.shape[1]
    assert META_BLOCK_E + -(-n_blocks // LANES) <= META_ROWS
    dest, meta = pl.pallas_call(
        functools.partial(_slot_plan_kernel, n_blocks=n_blocks),
        out_shape=[jax.ShapeDtypeStruct((TOP_K, n), jnp.int32),
                   jax.ShapeDtypeStruct((META_ROWS, LANES), jnp.int32)],
        compiler_params=pltpu.CompilerParams(vmem_limit_bytes=VMEM_LIMIT_BYTES),
        name="moe_slot_plan",
    )(eidx, rank, counts)
    pstart = meta[META_PSTART, :N_EXPERTS]
    cnt = meta[META_COUNT, :N_EXPERTS]
    nused = meta[META_NUSED, :1]
    block_e = meta[META_BLOCK_E:, :].reshape(-1)[:n_blocks]
    return dest.reshape(-1), pstart, cnt, block_e, nused


def _experts_kernel(dest_ref, pstart_ref, count_ref, block_e_ref, nused_ref,
                    h8_hbm, wg_ref, wu_ref, wd_ref, ys8_ref, slot_tok, xbuf, sems):
    j = pl.program_id(0)
    nused = nused_ref[0]
    n_tok = dest_ref.shape[0] // TOP_K

    @pl.when(j == 0)
    def _():
        def fill(t, _):
            for k in range(TOP_K):
                slot_tok[dest_ref[k * n_tok + t]] = t
            return 0
        lax.fori_loop(0, n_tok, fill, 0, unroll=8)

        def pad_expert(e, _):
            cnt = count_ref[e]
            padded = (cnt + MOE_BLK - 1) // MOE_BLK * MOE_BLK

            def pad(i, _):
                s = pstart_ref[e] + i
                slot_tok[s] = s % n_tok
                return 0
            lax.fori_loop(cnt, padded, pad, 0)
            return 0
        lax.fori_loop(0, N_EXPERTS, pad_expert, 0)

    def issue(blk, slot):
        for i in range(MOE_BLK):
            _tile_copy(h8_hbm, slot_tok[blk * MOE_BLK + i], xbuf.at[slot], i, sems.at[slot]).start()

    def wait(slot):
        for i in range(MOE_BLK):
            _tile_copy(h8_hbm, 0, xbuf.at[slot], i, sems.at[slot]).wait()

    ahead = MOE_NBUF - 1
    slot = j % MOE_NBUF

    @pl.when(jnp.logical_and(j == 0, nused > 0))
    def _():
        for a in range(ahead):
            issue(jnp.minimum(a, nused - 1), a)

    @pl.when(j < nused)
    def _():
        issue(jnp.minimum(j + ahead, nused - 1), (j + ahead) % MOE_NBUF)
        wait(slot)
        x = _load_token_tiles(xbuf.at[slot], MOE_BLK).astype(BF16)
        a = jnp.dot(x, wg_ref[0, 0].astype(BF16), preferred_element_type=F32)
        u = jnp.dot(x, wu_ref[0, 0].astype(BF16), preferred_element_type=F32)
        hm = (_silu(a) * u).astype(BF16)
        y = jnp.dot(hm, wd_ref[0, 0].astype(BF16), preferred_element_type=F32)
        _store_token_tiles(ys8_ref, y, MOE_BLK)

    @pl.when(j == nused - 1)
    def _():
        for a in range(1, MOE_NBUF):
            wait((slot + a) % MOE_NBUF)

    @pl.when(j >= nused)
    def _():
        ys8_ref[...] = jnp.zeros_like(ys8_ref)


def _experts(h8, dest, pstart, counts, block_e, nused, wg, wu, wd, layer, p_slots):
    n_blocks = p_slots // MOE_BLK
    wmap = lambda j, de, ps, ct, be, nu: (layer, be[j], 0, 0)
    grid_spec = pltpu.PrefetchScalarGridSpec(
        num_scalar_prefetch=5,
        grid=(n_blocks,),
        in_specs=[pl.BlockSpec(memory_space=pl.ANY),
                  pl.BlockSpec((1, 1, D_MODEL, D_EXPERT), wmap),
                  pl.BlockSpec((1, 1, D_MODEL, D_EXPERT), wmap),
                  pl.BlockSpec((1, 1, D_EXPERT, D_MODEL), wmap)],
        out_specs=pl.BlockSpec((MOE_BLK * TOK_TILES, LANES), lambda j, *_: (j, 0)),
        scratch_shapes=[pltpu.SMEM((p_slots,), jnp.int32),
                        pltpu.VMEM((MOE_NBUF, MOE_BLK * TOK_TILES, LANES), F32),
                        pltpu.SemaphoreType.DMA((MOE_NBUF,))],
    )
    return pl.pallas_call(
        _experts_kernel,
        grid_spec=grid_spec,
        out_shape=jax.ShapeDtypeStruct((p_slots * TOK_TILES, LANES), F32),
        compiler_params=_cparams(("arbitrary",)),
        name="moe_experts",
    )(dest, pstart, counts, block_e, nused, h8, wg, wu, wd)


def _combine_kernel(dest_ref, h8_ref, gcol_ref, ys8_hbm, lng_ref, lnb_ref, o_ref, buf, sems):
    i = pl.program_id(0)
    nt = pl.num_programs(0)
    n_tok = nt * COMB_TM

    def issue(tile, slot):
        for t in range(COMB_TM):
            for k in range(TOP_K):
                row = dest_ref[k * n_tok + tile * COMB_TM + t]
                _tile_copy(ys8_hbm, row, buf.at[slot, k], t, sems.at[slot]).start()

    def wait(slot):
        for t in range(COMB_TM):
            for k in range(TOP_K):
                _tile_copy(ys8_hbm, 0, buf.at[slot, k], t, sems.at[slot]).wait()

    slot = i % 2

    @pl.when(i == 0)
    def _():
        issue(0, 0)

    issue(jnp.minimum(i + 1, nt - 1), 1 - slot)
    wait(slot)
    g = gcol_ref[...]
    ffn = g[:, 0:1] * _load_token_tiles(buf.at[slot, 0], COMB_TM) + \
        g[:, 1:2] * _load_token_tiles(buf.at[slot, 1], COMB_TM)
    z = ALPHA * _load_token_tiles(h8_ref, COMB_TM) + ffn
    o_ref[...] = _layer_norm(z, lng_ref[...], lnb_ref[...])

    @pl.when(i == nt - 1)
    def _():
        wait(1 - slot)


def _combine(h8, gcol, ys8, dest, ln_g, ln_b):
    n = gcol.shape[0]
    const2 = lambda i, *_: (0, 0)
    grid_spec = pltpu.PrefetchScalarGridSpec(
        num_scalar_prefetch=1,
        grid=(n // COMB_TM,),
        in_specs=[pl.BlockSpec((COMB_TM * TOK_TILES, LANES), lambda i, *_: (i, 0)),
                  pl.BlockSpec((COMB_TM, LANES), lambda i, *_: (i, 0)),
                  pl.BlockSpec(memory_space=pl.ANY),
                  pl.BlockSpec((1, D_MODEL), const2),
                  pl.BlockSpec((1, D_MODEL), const2)],
        out_specs=pl.BlockSpec((COMB_TM, D_MODEL), lambda i, *_: (i, 0)),
        scratch_shapes=[pltpu.VMEM((2, TOP_K, COMB_TM * TOK_TILES, LANES), F32),
                        pltpu.SemaphoreType.DMA((2,))],
    )
    return pl.pallas_call(
        _combine_kernel,
        grid_spec=grid_spec,
        out_shape=jax.ShapeDtypeStruct((n, D_MODEL), F32),
        compiler_params=_cparams(("arbitrary",)),
        name="moe_combine",
    )(dest, h8, gcol, ys8, ln_g, ln_b)


def _moe_and_norm(h8, eidx, rank, gcol, counts, wg, wu, wd, layer, ln_g, ln_b):
    n = gcol.shape[0]
    p_slots = n * TOP_K + N_EXPERTS * MOE_BLK
    dest, pstart, cnt, block_e, nused = _slot_plan(eidx, rank, counts, p_slots // MOE_BLK)
    ys8 = _experts(h8, dest, pstart, cnt, block_e, nused, wg, wu, wd, layer, p_slots)
    return _combine(h8, gcol, ys8, dest, ln_g, ln_b)


def _retention_constants():
    log_gamma = jnp.log1p(-jnp.exp2(-5.0 - jnp.arange(RET_HEADS, dtype=F32)))
    idx = jnp.arange(CHUNK)
    rel = idx[:, None] - idx[None, :]
    decay_in = jnp.where(rel >= 0, jnp.exp(log_gamma[:, None, None] * jnp.maximum(rel, 0)), 0.0)
    xi = jnp.exp(log_gamma[:, None] * (idx + 1))
    zeta = jnp.exp(log_gamma[:, None] * (CHUNK - 1 - idx))
    g_chunk = jnp.exp(log_gamma * CHUNK)
    rc = jnp.zeros((CHUNK, LANES), F32)
    rc = rc.at[:, 0:RET_HEADS].set(xi.T)
    rc = rc.at[:, RET_HEADS:2 * RET_HEADS].set(zeta.T)
    rc = rc.at[:, 2 * RET_HEADS:3 * RET_HEADS].set(jnp.broadcast_to(g_chunk[None, :], (CHUNK, RET_HEADS)))
    return decay_in.astype(F32), rc


def kernel(x, w_in_even, i_bias, f_bias, conv_w, conv_b, mlstm_norm_g, ret_norm_g, w_out_even, w_in_odd,
           w_out_odd, router_w, router_bias, w_gate, w_up, w_down, ln_g, ln_b):
    batch, seq, _ = x.shape
    n = batch * seq
    assert all(w // d == CHUNK for w, d in DIL_CONFIGS) and seq % ATT_SPAN == 0
    h = x.reshape(n, D_MODEL)
    rw_t = router_w.T.astype(F32)
    rw_hi = rw_t.astype(BF16)
    router_wt = jnp.stack([rw_hi, (rw_t - rw_hi.astype(F32)).astype(BF16)])
    router_b = router_bias.astype(F32).reshape(N_EXPERTS, 1)
    decay_in, ret_consts = _retention_constants()
    tri = (jnp.arange(POST_TM)[:, None] <= jnp.arange(POST_TM)[None, :]).astype(BF16)
    att_bias = _attention_bias()
    gate_col0 = 4 * MLSTM_W
    gate_col1 = gate_col0 + 2 * MLSTM_HEADS

    for layer in range(DEPTH):
        j = layer // 2
        lg = lambda s: ln_g[layer, s].reshape(1, D_MODEL)
        lb = lambda s: ln_b[layer, s].reshape(1, D_MODEL)
        if layer % 2 == 0:
            w = w_in_even[j]
            w_re = jnp.concatenate(
                [w[:, :gate_col0], w[:, gate_col1:], w[:, gate_col0:gate_col1],
                 jnp.zeros((D_MODEL, LANES - 2 * MLSTM_HEADS), w.dtype)], axis=1).astype(BF16)
            gate_bias = jnp.zeros((1, LANES), F32)
            gate_bias = gate_bias.at[0, 0:MLSTM_HEADS].set(i_bias[j])
            gate_bias = gate_bias.at[0, MLSTM_HEADS:2 * MLSTM_HEADS].set(f_bias[j])
            norm_g = jnp.concatenate([mlstm_norm_g[j], ret_norm_g[j]]).reshape(1, MLSTM_W + RET_W)
            y = _even_mixer(h, w_re, batch, seq, conv_w[j], conv_b[j].reshape(1, 2 * MLSTM_W), gate_bias,
                            decay_in, ret_consts, norm_g.astype(F32))
            w_out_all = w_out_even
        else:
            p = _project(h, _odd_weights(w_in_odd[j]), "proj_odd")
            y = _dilated_attention(p, att_bias, batch, seq)
            w_out_all = w_out_odd
        h8, eidx, rank, gcol, counts = _post_mixer(h, y, w_out_all, j, lg(0), lb(0),
                                                   router_wt, router_b, tri, f"post_{layer}")
        h = _moe_and_norm(h8, eidx, rank, gcol, counts, w_gate, w_up, w_down, layer, lg(1), lb(1))
    return h.reshape(batch, seq, D_MODEL)
```

```python
import functools

import jax
import jax.numpy as jnp
from jax import lax
from jax.experimental import pallas as pl
from jax.experimental.pallas import tpu as pltpu

F32 = jnp.float32
BF16 = jnp.bfloat16

D_MODEL = 1024
DEPTH = 2
ALPHA = (2 * DEPTH) ** 0.25
LN_EPS = 1e-5
MLSTM_HEADS = 4
RET_HEADS = 4
HEAD_DIM = 128
MLSTM_W = MLSTM_HEADS * HEAD_DIM
RET_W = RET_HEADS * HEAD_DIM
CHUNK = 128
CONV_WIDTH = 4
DIL_HEADS = 8
DIL_DH = 64
DIL_W = DIL_HEADS * DIL_DH
DIL_CONFIGS = ((128, 1), (512, 4), (2048, 16))
N_EXPERTS = 32
N_GROUPS = 4
EXPERTS_PER_GROUP = N_EXPERTS // N_GROUPS
TOP_K = 2
D_EXPERT = 256

LANES = 128
SUBLANES = 8
VMEM_LIMIT_BYTES = 56 * 1024 * 1024

PROJ_TM = 512
POST_TM = 1024
MOE_BLK = 256
MOE_NBUF = 2
COMB_TM = 256
CONV_TAIL = SUBLANES

EV_QM, EV_KM, EV_VM, EV_OM = 0, MLSTM_W, 2 * MLSTM_W, 3 * MLSTM_W
EV_QR, EV_KR, EV_VR, EV_GR = (4 * MLSTM_W, 4 * MLSTM_W + RET_W, 4 * MLSTM_W + 2 * RET_W,
                              4 * MLSTM_W + 3 * RET_W)
EV_GATE = 4 * MLSTM_W + 4 * RET_W
EV_COLS = EV_GATE + LANES

NEG_INF = float("-inf")


def _cparams(sem):
    return pltpu.CompilerParams(dimension_semantics=sem, vmem_limit_bytes=VMEM_LIMIT_BYTES)


def _proj_kernel(x_ref, w_ref, o_ref):
    o_ref[...] = jnp.dot(x_ref[...].astype(BF16), w_ref[...], preferred_element_type=F32)


def _project(x, w, name):
    n, k = x.shape
    m = w.shape[1]
    return pl.pallas_call(
        _proj_kernel,
        grid=(n // PROJ_TM,),
        in_specs=[pl.BlockSpec((PROJ_TM, k), lambda i: (i, 0)),
                  pl.BlockSpec((k, m), lambda i: (0, 0))],
        out_specs=pl.BlockSpec((PROJ_TM, m), lambda i: (i, 0)),
        out_shape=jax.ShapeDtypeStruct((n, m), F32),
        compiler_params=_cparams(("parallel",)),
        name=name,
    )(x, w)


EV_STEP_CHUNKS = 2
EV_PROJ_PIECE = 2 * LANES


def _silu(x):
    return x * jax.nn.sigmoid(x)


def _head_norm(h):
    mu = jnp.mean(h, -1, keepdims=True)
    c = h - mu
    var = jnp.mean(c * c, -1, keepdims=True)
    return c * lax.rsqrt(var + LN_EPS)


def _dot_nt(a, b):
    return lax.dot_general(a, b, (((1,), (1,)), ((), ())), preferred_element_type=F32)


def _dot_tn(a_f32, b):
    return lax.dot_general(a_f32.astype(BF16), b, (((0,), (0,)), ((), ())), preferred_element_type=F32)


def _even_kernel(x0_ref, x_ref, w_ref, convw_ref, convb_ref, gbias_ref, dec_ref, rc_ref, ng_ref, y_ref,
                 p_scr, xb_scr, qkbuf, c_ref, n_ref, m_ref, r_ref, *, steps_per_batch):
    i = pl.program_id(0)

    @pl.when(i == 0)
    def _():
        p_scr[0] = jnp.dot(x0_ref[...].astype(BF16), w_ref[...], preferred_element_type=F32)

    @pl.when(i % steps_per_batch == 0)
    def _():
        qkbuf[0:CONV_TAIL, :] = jnp.zeros((CONV_TAIL, 2 * MLSTM_W), F32)
        c_ref[...] = jnp.zeros_like(c_ref)
        n_ref[...] = jnp.zeros_like(n_ref)
        m_ref[...] = jnp.zeros_like(m_ref)
        r_ref[...] = jnp.zeros_like(r_ref)

    xb_scr[...] = x_ref[...].astype(BF16)
    nxt = (i + 1) % 2
    pieces = iter(range(0, EV_COLS, EV_PROJ_PIECE))

    def project_piece():
        lo = next(pieces, None)
        if lo is not None:
            hi = min(lo + EV_PROJ_PIECE, EV_COLS)
            p_scr[nxt, :, lo:hi] = jnp.dot(xb_scr[...], w_ref[:, lo:hi], preferred_element_type=F32)

    cur = p_scr.at[i % 2]
    for c in range(EV_STEP_CHUNKS):
        rows = pl.ds(c * CHUNK, CHUNK)
        _mixer_chunk(cur.at[rows], y_ref.at[rows], convw_ref, convb_ref, gbias_ref, dec_ref, rc_ref, ng_ref,
                     qkbuf, c_ref, n_ref, m_ref, r_ref, project_piece)
    for _ in pieces:
        raise AssertionError("projection pieces left over: EV_PROJ_PIECE too small for the number of heads")


def _mixer_chunk(p_ref, y_ref, convw_ref, convb_ref, gbias_ref, dec_ref, rc_ref, ng_ref,
                 qkbuf, c_ref, n_ref, m_ref, r_ref, between):
    L = CHUNK
    scale = HEAD_DIM ** -0.5
    between()

    qkbuf[CONV_TAIL:CONV_TAIL + L, :] = p_ref[:, EV_QM:EV_QM + 2 * MLSTM_W]
    acc = jnp.broadcast_to(convb_ref[...], (L, 2 * MLSTM_W))
    for k in range(CONV_WIDTH):
        off = CONV_TAIL - (CONV_WIDTH - 1) + k
        acc = acc + convw_ref[k:k + 1, :] * qkbuf[off:off + L, :]
    qk = _silu(acc)
    qkbuf[0:CONV_TAIL, :] = qkbuf[L:L + CONV_TAIL, :]

    pre = p_ref[:, EV_GATE:EV_GATE + LANES] + gbias_ref[...]
    logf = jnp.minimum(pre, 0.0) - jnp.log1p(jnp.exp(-jnp.abs(pre)))
    row = lax.broadcasted_iota(jnp.int32, (L, L), 0)
    col = lax.broadcasted_iota(jnp.int32, (L, L), 1)
    causal = row >= col
    bcs = jnp.dot(causal.astype(F32), logf, preferred_element_type=F32,
                  precision=lax.Precision.HIGHEST)
    u_t = (pre - pltpu.roll(bcs, LANES - MLSTM_HEADS, 1)).T

    for h in range(MLSTM_HEADS):
        between()
        lo = h * HEAD_DIM
        b_col = bcs[:, MLSTM_HEADS + h:MLSTM_HEADS + h + 1]
        i_col = pre[:, h:h + 1]
        r_row = u_t[h:h + 1, :]
        m_prev = m_ref[h:h + 1, 0:1]
        log_d = jnp.where(causal, b_col + r_row, NEG_INF)
        a = b_col + m_prev
        m_t = jnp.maximum(a, jnp.max(log_d, -1, keepdims=True))
        d = jnp.exp(log_d - m_t)
        inter = jnp.exp(a - m_t)
        q = qk[:, lo:lo + HEAD_DIM]
        k = qk[:, MLSTM_W + lo:MLSTM_W + lo + HEAD_DIM] * scale
        qb = q.astype(BF16)
        vb = p_ref[:, EV_VM + lo:EV_VM + lo + HEAD_DIM].astype(BF16)
        s = _dot_nt(qb, k.astype(BF16)) * d
        c_old = c_ref[h]
        n_old = n_ref[h:h + 1, :]
        num = jnp.dot(s.astype(BF16), vb, preferred_element_type=F32) + \
            inter * jnp.dot(qb, c_old.astype(BF16), preferred_element_type=F32)
        den = jnp.sum(s, -1, keepdims=True) + inter * jnp.sum(q * n_old, -1, keepdims=True)
        hh = num / jnp.maximum(jnp.abs(den), jnp.exp(-m_t))
        g = b_col[L - 1:L, :]
        log_w = g - b_col + i_col
        m_new = jnp.maximum(g + m_prev, jnp.max(log_w, 0, keepdims=True))
        kw = k * jnp.exp(log_w - m_new)
        decay = jnp.exp(g + m_prev - m_new)
        c_ref[h] = decay * c_old + _dot_tn(kw, vb)
        n_ref[h:h + 1, :] = decay * n_old + jnp.sum(kw, 0, keepdims=True)
        m_ref[h:h + 1, :] = jnp.broadcast_to(m_new, (1, LANES))
        o_gate = jax.nn.sigmoid(p_ref[:, EV_OM + lo:EV_OM + lo + HEAD_DIM])
        y_ref[:, lo:lo + HEAD_DIM] = (_head_norm(o_gate * hh) * ng_ref[:, lo:lo + HEAD_DIM]).astype(y_ref.dtype)

    for h in range(RET_HEADS):
        between()
        lo = h * HEAD_DIM
        qb = p_ref[:, EV_QR + lo:EV_QR + lo + HEAD_DIM].astype(BF16)
        k = p_ref[:, EV_KR + lo:EV_KR + lo + HEAD_DIM] * scale
        vb = p_ref[:, EV_VR + lo:EV_VR + lo + HEAD_DIM].astype(BF16)
        xi = rc_ref[:, h:h + 1]
        zeta = rc_ref[:, RET_HEADS + h:RET_HEADS + h + 1]
        g_chunk = rc_ref[0:1, 2 * RET_HEADS + h:2 * RET_HEADS + h + 1]
        r_old = r_ref[h]
        inner = _dot_nt(qb, k.astype(BF16)) * dec_ref[h]
        y = jnp.dot(inner.astype(BF16), vb, preferred_element_type=F32) + \
            xi * jnp.dot(qb, r_old.astype(BF16), preferred_element_type=F32)
        r_ref[h] = g_chunk * r_old + _dot_tn(k * zeta, vb)
        gr = p_ref[:, EV_GR + lo:EV_GR + lo + HEAD_DIM]
        out = _head_norm(y) * ng_ref[:, MLSTM_W + lo:MLSTM_W + lo + HEAD_DIM] * _silu(gr)
        y_ref[:, MLSTM_W + lo:MLSTM_W + lo + HEAD_DIM] = out.astype(y_ref.dtype)


def _even_mixer(x, w, batch, seq, conv_w, conv_b, gate_bias, decay_in, ret_consts, norm_g):
    tm = EV_STEP_CHUNKS * CHUNK
    steps_per_batch = seq // tm
    nsteps = batch * steps_per_batch
    const2 = lambda i: (0, 0)
    return pl.pallas_call(
        functools.partial(_even_kernel, steps_per_batch=steps_per_batch),
        grid=(nsteps,),
        in_specs=[pl.BlockSpec((tm, D_MODEL), const2),
                  pl.BlockSpec((tm, D_MODEL), lambda i: (jnp.minimum(i + 1, nsteps - 1), 0)),
                  pl.BlockSpec((D_MODEL, EV_COLS), const2),
                  pl.BlockSpec((CONV_WIDTH, 2 * MLSTM_W), const2),
                  pl.BlockSpec((1, 2 * MLSTM_W), const2),
                  pl.BlockSpec((1, LANES), const2),
                  pl.BlockSpec((RET_HEADS, CHUNK, CHUNK), lambda i: (0, 0, 0)),
                  pl.BlockSpec((CHUNK, LANES), const2),
                  pl.BlockSpec((1, MLSTM_W + RET_W), const2)],
        out_specs=pl.BlockSpec((tm, MLSTM_W + RET_W), lambda i: (i, 0)),
        out_shape=jax.ShapeDtypeStruct((batch * seq, MLSTM_W + RET_W), BF16),
        scratch_shapes=[pltpu.VMEM((2, tm, EV_COLS), F32),
                        pltpu.VMEM((tm, D_MODEL), BF16),
                        pltpu.VMEM((CONV_TAIL + CHUNK, 2 * MLSTM_W), F32),
                        pltpu.VMEM((MLSTM_HEADS, HEAD_DIM, HEAD_DIM), F32),
                        pltpu.VMEM((SUBLANES, HEAD_DIM), F32),
                        pltpu.VMEM((SUBLANES, LANES), F32),
                        pltpu.VMEM((RET_HEADS, HEAD_DIM, HEAD_DIM), F32)],
        compiler_params=_cparams(("arbitrary",)),
        name="even_mixer",
    )(x, x, w, conv_w, conv_b, gate_bias, decay_in, ret_consts, norm_g)


TOK_TILES = D_MODEL // LANES


def _store_token_tiles(ref, val, rows):
    for c in range(TOK_TILES):
        ref[pl.ds(c, rows, stride=TOK_TILES), :] = val[:, c * LANES:(c + 1) * LANES]


def _load_token_tiles(ref, rows):
    return jnp.concatenate([ref[pl.ds(c, rows, stride=TOK_TILES), :] for c in range(TOK_TILES)], axis=1)


def _layer_norm(z, g, b):
    mu = jnp.mean(z, -1, keepdims=True)
    c = z - mu
    var = jnp.mean(c * c, -1, keepdims=True)
    return c * lax.rsqrt(var + LN_EPS) * g + b


def _route(h, rwt_ref, rb_ref):
    h_hi = h.astype(BF16)
    h_lo = (h - h_hi.astype(F32)).astype(BF16)
    logits = _dot_nt(rwt_ref[0], h_hi) + (_dot_nt(rwt_ref[0], h_lo) + _dot_nt(rwt_ref[1], h_hi))
    aff = jax.nn.sigmoid(logits)
    sel = aff + rb_ref[...]
    tm = h.shape[0]
    sub = lax.broadcasted_iota(jnp.int32, (EXPERTS_PER_GROUP, tm), 0)
    best = None
    for g in range(N_GROUPS):
        lo = g * EXPERTS_PER_GROUP
        sg = sel[lo:lo + EXPERTS_PER_GROUP, :]
        ag = aff[lo:lo + EXPERTS_PER_GROUP, :]
        v1 = jnp.max(sg, 0, keepdims=True)
        i1 = jnp.min(jnp.where(sg == v1, sub, EXPERTS_PER_GROUP), 0, keepdims=True)
        rest = jnp.where(sub == i1, NEG_INF, sg)
        v2 = jnp.max(rest, 0, keepdims=True)
        i2 = jnp.min(jnp.where(rest == v2, sub, EXPERTS_PER_GROUP), 0, keepdims=True)
        a1 = jnp.sum(jnp.where(sub == i1, ag, 0.0), 0, keepdims=True)
        a2 = jnp.sum(jnp.where(sub == i2, ag, 0.0), 0, keepdims=True)
        cand = (v1 + v2, i1 + lo, i2 + lo, a1, a2)
        if best is None:
            best = cand
        else:
            take = cand[0] > best[0]
            best = tuple(jnp.where(take, c, b) for c, b in zip(cand, best))
    _, e1, e2, a1, a2 = best
    tot = a1 + a2
    return e1, e2, a1 / tot, a2 / tot


def _post_kernel(x_ref, y_ref, wo_ref, lng_ref, lnb_ref, rwt_ref, rb_ref, tri_ref,
                 h8_ref, eidx_ref, rank_ref, gcol_ref, cnt_ref, run_ref, wo_scr):
    tm = x_ref.shape[0]

    @pl.when(pl.program_id(0) == 0)
    def _():
        run_ref[...] = jnp.zeros_like(run_ref)
        wo_scr[...] = wo_ref[0].astype(BF16)

    mix = jnp.dot(y_ref[...], wo_scr[...], preferred_element_type=F32)
    h = _layer_norm(ALPHA * x_ref[...] + mix, lng_ref[...], lnb_ref[...])
    _store_token_tiles(h8_ref, h, tm)

    e1, e2, g1, g2 = _route(h, rwt_ref, rb_ref)
    eidx_ref[0:1, :] = e1
    eidx_ref[1:2, :] = e2
    gates = jnp.concatenate([g1, g2, jnp.zeros((LANES - TOP_K, tm), F32)], axis=0)
    gcol_ref[...] = gates.T

    sub = lax.broadcasted_iota(jnp.int32, (N_EXPERTS, tm), 0)
    oh1 = sub == e1
    oh2 = sub == e2
    chosen = jnp.logical_or(oh1, oh2).astype(F32)
    incl = jnp.dot(chosen.astype(BF16), tri_ref[...], preferred_element_type=F32)
    before = run_ref[:, 0:1] + incl - chosen
    rank_ref[0:1, :] = jnp.sum(jnp.where(oh1, before, 0.0), 0, keepdims=True).astype(jnp.int32)
    rank_ref[1:2, :] = jnp.sum(jnp.where(oh2, before, 0.0), 0, keepdims=True).astype(jnp.int32)
    run_new = run_ref[...] + incl[:, tm - 1:tm]
    run_ref[...] = run_new
    cnt_ref[...] = run_new.astype(jnp.int32)


def _post_mixer(x, y, w_out_all, j, ln_g, ln_b, router_wt, router_b, tri, name):
    n = x.shape[0]
    tm = POST_TM
    kdim = w_out_all.shape[1]
    const2 = lambda i: (0, 0)
    return pl.pallas_call(
        _post_kernel,
        grid=(n // tm,),
        in_specs=[pl.BlockSpec((tm, D_MODEL), lambda i: (i, 0)),
                  pl.BlockSpec((tm, y.shape[1]), lambda i: (i, 0)),
                  pl.BlockSpec((1, kdim, D_MODEL), lambda i: (j, 0, 0)),
                  pl.BlockSpec((1, D_MODEL), const2),
                  pl.BlockSpec((1, D_MODEL), const2),
                  pl.BlockSpec((2, N_EXPERTS, D_MODEL), lambda i: (0, 0, 0)),
                  pl.BlockSpec((N_EXPERTS, 1), const2),
                  pl.BlockSpec((tm, tm), const2)],
        out_specs=[pl.BlockSpec((tm * TOK_TILES, LANES), lambda i: (i, 0)),
                   pl.BlockSpec((TOP_K, tm), lambda i: (0, i)),
                   pl.BlockSpec((TOP_K, tm), lambda i: (0, i)),
                   pl.BlockSpec((tm, LANES), lambda i: (i, 0)),
                   pl.BlockSpec((N_EXPERTS, LANES), const2)],
        out_shape=[jax.ShapeDtypeStruct((n * TOK_TILES, LANES), F32),
                   jax.ShapeDtypeStruct((TOP_K, n), jnp.int32),
                   jax.ShapeDtypeStruct((TOP_K, n), jnp.int32),
                   jax.ShapeDtypeStruct((n, LANES), F32),
                   jax.ShapeDtypeStruct((N_EXPERTS, LANES), jnp.int32)],
        scratch_shapes=[pltpu.VMEM((N_EXPERTS, LANES), F32),
                        pltpu.VMEM((kdim, D_MODEL), BF16)],
        compiler_params=_cparams(("arbitrary",)),
        name=name,
    )(x, y, w_out_all, ln_g, ln_b, router_wt, router_b, tri)


ATT_SPAN = max(w for w, _ in DIL_CONFIGS)
ATT_PAIR = 2
ATT_PW = ATT_PAIR * DIL_DH
N_DIL = len(DIL_CONFIGS)


ATT_GROUP = 3


def _for_block_groups(n, body):
    full = n // ATT_GROUP
    if full:
        def trip(t, _):
            body([t * ATT_GROUP + u for u in range(ATT_GROUP)])
            return 0
        lax.fori_loop(0, full, trip, 0)
    if n % ATT_GROUP:
        body(list(range(full * ATT_GROUP, n)))


def _attn_rows(start, stride):
    L = CHUNK
    if stride == 1:
        if not isinstance(start, int):
            start = pl.multiple_of(start, L)
        return pl.ds(start, L), pl.ds(start, 2 * L)
    return pl.ds(start, L, stride=stride), pl.ds(start, 2 * L, stride=stride)


def _attn_units(q_ref, kbuf, vbuf, bias, g, starts, stride, noprev, o_scr, m_scr, l_scr):
    rows = [_attn_rows(s, stride) for s in starts]
    loaded = [(q_ref[qr, :].astype(BF16),
               kbuf[kvr, :].astype(BF16),
               vbuf[kvr, :].astype(BF16)) for qr, kvr in rows]
    scores = [_attn_scores(q2, kb, bias, noprev) for q2, kb, _ in loaded]
    probs = [_attn_softmax(s) for s in scores]
    outs = [_attn_values(p, vb) for (p, _, _), (_, _, vb) in zip(probs, loaded)]
    for (qr, _), o, (_, mm, ll) in zip(rows, outs, probs):
        o_scr[g, qr, :] = o
        m_scr[g, qr, :] = mm
        l_scr[g, qr, :] = ll


def _two_heads(x):
    head0 = lax.broadcasted_iota(jnp.int32, x.shape, 1) < DIL_DH
    zero = jnp.zeros_like(x)
    return jnp.concatenate([jnp.where(head0, x, zero), jnp.where(head0, zero, x)], axis=0)


def _attn_scores(q2, kb, bias, noprev):
    L = CHUNK
    s = _dot_nt(q2, _two_heads(kb)) + bias
    if noprev is not None:
        col = lax.broadcasted_iota(jnp.int32, (L, 4 * L), 1)
        prev_col = (col % (2 * L)) < L
        s = jnp.where(jnp.logical_and(prev_col, noprev), NEG_INF, s)
    return s


def _attn_softmax(s):
    L = CHUNK
    s0, s1 = s[:, 0:2 * L], s[:, 2 * L:4 * L]
    m0 = jnp.max(s0, -1, keepdims=True)
    m1 = jnp.max(s1, -1, keepdims=True)
    p0 = jnp.exp(s0 - m0)
    p1 = jnp.exp(s1 - m1)
    l0 = jnp.sum(p0, -1, keepdims=True)
    l1 = jnp.sum(p1, -1, keepdims=True)
    left = lax.broadcasted_iota(jnp.int32, (L, ATT_PW), 1) < DIL_DH
    p = jnp.concatenate([p0, p1], axis=1).astype(BF16)
    return p, jnp.where(left, m0, m1), jnp.where(left, l0, l1)


def _attn_values(p, vb):
    return jnp.dot(p, _two_heads(vb), preferred_element_type=F32)


def _attn_kernel(*refs):
    q_refs, kc_refs, vc_refs, kp_refs, vp_refs = (refs[i * N_DIL:(i + 1) * N_DIL] for i in range(5))
    bias_ref, y_ref = refs[5 * N_DIL:5 * N_DIL + 2]
    scr = refs[5 * N_DIL + 2:]
    kbufs, vbufs = scr[0:N_DIL], scr[N_DIL:2 * N_DIL]
    o_scr, m_scr, l_scr = scr[2 * N_DIL:]
    first = pl.program_id(2) == 0
    for g, (window, _) in enumerate(DIL_CONFIGS):
        kbufs[g][0:window, :] = kp_refs[g][...]
        vbufs[g][0:window, :] = vp_refs[g][...]
        kbufs[g][window:window + ATT_SPAN, :] = kc_refs[g][...]
        vbufs[g][window:window + ATT_SPAN, :] = vc_refs[g][...]

    for g, (window, dilation) in enumerate(DIL_CONFIGS):
        bias = bias_ref[0, g]
        nsub = ATT_SPAN // window

        def first_blocks(idx, g=g, bias=bias, dilation=dilation):
            _attn_units(q_refs[g], kbufs[g], vbufs[g], bias, g, idx, dilation, first, o_scr, m_scr, l_scr)
        _for_block_groups(dilation, first_blocks)

        def rest_blocks(idx, g=g, bias=bias, window=window, dilation=dilation):
            starts = [(1 + i // dilation) * window + i % dilation for i in idx]
            _attn_units(q_refs[g], kbufs[g], vbufs[g], bias, g, starts, dilation, None, o_scr, m_scr, l_scr)
        _for_block_groups((nsub - 1) * dilation, rest_blocks)

    rows = 256

    def merge(i, _):
        sl = pl.ds(pl.multiple_of(i * rows, rows), rows)
        ms = [m_scr[g, sl, :] for g in range(N_DIL)]
        mx = functools.reduce(jnp.maximum, ms)
        es = [jnp.exp(m - mx) for m in ms]
        num = sum(e * o_scr[g, sl, :] for g, e in enumerate(es))
        den = sum(e * l_scr[g, sl, :] for g, e in enumerate(es))
        y_ref[sl, :] = (num / den).astype(y_ref.dtype)
        return 0
    lax.fori_loop(0, ATT_SPAN // rows, merge, 0)


def _dilated_attention(p, bias, batch, seq):
    n = p.shape[0]
    nspan = seq // ATT_SPAN
    npair = DIL_HEADS // ATT_PAIR
    slab = DIL_W // ATT_PW

    def col(g, c, pair):
        return (g * 3 + c) * slab + pair

    def cur_spec(g, c):
        return pl.BlockSpec((ATT_SPAN, ATT_PW), lambda b, pr, s: (b * nspan + s, col(g, c, pr)))

    def prev_spec(g, c):
        window = DIL_CONFIGS[g][0]
        per_span = ATT_SPAN // window
        rows_per_batch = seq // window
        return pl.BlockSpec(
            (window, ATT_PW),
            lambda b, pr, s: (jnp.maximum(b * rows_per_batch + s * per_span - 1, 0), col(g, c, pr)))

    groups = range(N_DIL)
    in_specs = ([cur_spec(g, 0) for g in groups] + [cur_spec(g, 1) for g in groups] +
                [cur_spec(g, 2) for g in groups] + [prev_spec(g, 1) for g in groups] +
                [prev_spec(g, 2) for g in groups] +
                [pl.BlockSpec((1, N_DIL, CHUNK, 4 * CHUNK), lambda b, pr, s: (pr, 0, 0, 0))])
    return pl.pallas_call(
        _attn_kernel,
        grid=(batch, npair, nspan),
        in_specs=in_specs,
        out_specs=pl.BlockSpec((ATT_SPAN, ATT_PW), lambda b, pr, s: (b * nspan + s, pr)),
        out_shape=jax.ShapeDtypeStruct((n, DIL_W), BF16),
        scratch_shapes=[pltpu.VMEM((w + ATT_SPAN, ATT_PW), F32) for w, _ in DIL_CONFIGS] * 2 +
                       [pltpu.VMEM((N_DIL, ATT_SPAN, ATT_PW), F32) for _ in range(3)],
        compiler_params=_cparams(("parallel", "parallel", "arbitrary")),
        name="dilated_attention",
    )(*([p] * (5 * N_DIL)), bias)


def _attention_bias():
    L = CHUNK
    qi = jnp.arange(L)[:, None]
    kr = jnp.arange(2 * L)[None, :]
    steps = qi + L - kr
    band = (steps >= 0) & (steps <= L)
    slopes = jnp.exp2(-8.0 * jnp.arange(1, DIL_HEADS + 1, dtype=F32) / DIL_HEADS)
    per_group = []
    for _, dilation in DIL_CONFIGS:
        b = -slopes[:, None, None] * (steps * dilation).astype(F32)[None]
        b = jnp.where(band[None], b, NEG_INF)
        per_group.append(b.reshape(DIL_HEADS // ATT_PAIR, ATT_PAIR, L, 2 * L)
                         .transpose(0, 2, 1, 3).reshape(DIL_HEADS // ATT_PAIR, L, 4 * L))
    return jnp.stack(per_group, axis=1)


def _odd_weights(w):
    is_q = (jnp.arange(w.shape[1]) // DIL_W) % 3 == 0
    return (w * jnp.where(is_q, DIL_DH ** -0.5, 1.0)[None, :]).astype(BF16)


def _tile_copy(src_hbm, row, dst, i, sem):
    src = src_hbm.at[pl.ds(pl.multiple_of(row * TOK_TILES, TOK_TILES), TOK_TILES), :]
    return pltpu.make_async_copy(src, dst.at[pl.ds(i * TOK_TILES, TOK_TILES), :], sem)


META_PSTART, META_COUNT, META_NUSED, META_BLOCK_E = 0, 1, 2, 3
META_ROWS = SUBLANES


def _slot_plan_kernel(eidx_ref, rank_ref, cnt_ref, dest_ref, meta_ref, *, n_blocks):
    blk = float(MOE_BLK)
    hi = lax.Precision.HIGHEST
    cnt_b = cnt_ref[...].astype(F32)
    padded_b = jnp.floor((cnt_b + (blk - 1.0)) / blk) * blk
    er = lax.broadcasted_iota(jnp.int32, (LANES, LANES), 0)
    ec = lax.broadcasted_iota(jnp.int32, (LANES, LANES), 1)
    padded_full = jnp.concatenate([padded_b, jnp.zeros((LANES - N_EXPERTS, LANES), F32)], axis=0)
    pstart_b = jnp.dot((ec < er).astype(F32), padded_full, preferred_element_type=F32,
                       precision=hi)[0:N_EXPERTS, :]
    pend_b = pstart_b + padded_b

    sub = lax.broadcasted_iota(jnp.int32, (N_EXPERTS, eidx_ref.shape[1]), 0)
    pcol = pstart_b[:, 0:1]
    for k in range(TOP_K):
        start = jnp.sum(jnp.where(sub == eidx_ref[k:k + 1, :], pcol, 0.0), 0, keepdims=True)
        dest_ref[k:k + 1, :] = start.astype(jnp.int32) + rank_ref[k:k + 1, :]

    def as_row(col_b):
        full = jnp.concatenate([col_b, jnp.zeros((LANES - N_EXPERTS, LANES), F32)], axis=0)
        return full.T[0:1, :]
    meta_ref[...] = jnp.zeros_like(meta_ref)
    meta_ref[META_PSTART:META_PSTART + 1, :] = as_row(pstart_b).astype(jnp.int32)
    meta_ref[META_COUNT:META_COUNT + 1, :] = as_row(cnt_b).astype(jnp.int32)
    total = pend_b[N_EXPERTS - 1:N_EXPERTS, :]
    meta_ref[META_NUSED:META_NUSED + 1, :] = (total / blk).astype(jnp.int32)
    for r in range(-(-n_blocks // LANES)):
        first = (lax.broadcasted_iota(jnp.int32, (N_EXPERTS, LANES), 1) + r * LANES).astype(F32) * blk
        owner = jnp.sum((pend_b[:, 0:1] <= first).astype(jnp.int32), 0, keepdims=True)
        meta_ref[META_BLOCK_E + r:META_BLOCK_E + r + 1, :] = jnp.minimum(owner, N_EXPERTS - 1)


def _slot_plan(eidx, rank, counts, n_blocks):
    n = eidx.shape[1]
    assert META_BLOCK_E + -(-n_blocks // LANES) <= META_ROWS
    dest, meta = pl.pallas_call(
        functools.partial(_slot_plan_kernel, n_blocks=n_blocks),
        out_shape=[jax.ShapeDtypeStruct((TOP_K, n), jnp.int32),
                   jax.ShapeDtypeStruct((META_ROWS, LANES), jnp.int32)],
        compiler_params=pltpu.CompilerParams(vmem_limit_bytes=VMEM_LIMIT_BYTES),
        name="moe_slot_plan",
    )(eidx, rank, counts)
    pstart = meta[META_PSTART, :N_EXPERTS]
    cnt = meta[META_COUNT, :N_EXPERTS]
    nused = meta[META_NUSED, :1]
    block_e = meta[META_BLOCK_E:, :].reshape(-1)[:n_blocks]
    return dest.reshape(-1), pstart, cnt, block_e, nused


def _experts_kernel(dest_ref, pstart_ref, count_ref, block_e_ref, nused_ref,
                    h8_hbm, wg_ref, wu_ref, wd_ref, ys8_ref, slot_tok, xbuf, sems):
    j = pl.program_id(0)
    nused = nused_ref[0]
    n_tok = dest_ref.shape[0] // TOP_K

    @pl.when(j == 0)
    def _():
        def fill(t, _):
            for k in range(TOP_K):
                slot_tok[dest_ref[k * n_tok + t]] = t
            return 0
        lax.fori_loop(0, n_tok, fill, 0, unroll=8)

        def pad_expert(e, _):
            cnt = count_ref[e]
            padded = (cnt + MOE_BLK - 1) // MOE_BLK * MOE_BLK

            def pad(i, _):
                s = pstart_ref[e] + i
                slot_tok[s] = s % n_tok
                return 0
            lax.fori_loop(cnt, padded, pad, 0)
            return 0
        lax.fori_loop(0, N_EXPERTS, pad_expert, 0)

    def issue(blk, slot):
        for i in range(MOE_BLK):
            _tile_copy(h8_hbm, slot_tok[blk * MOE_BLK + i], xbuf.at[slot], i, sems.at[slot]).start()

    def wait(slot):
        for i in range(MOE_BLK):
            _tile_copy(h8_hbm, 0, xbuf.at[slot], i, sems.at[slot]).wait()

    ahead = MOE_NBUF - 1
    slot = j % MOE_NBUF

    @pl.when(jnp.logical_and(j == 0, nused > 0))
    def _():
        for a in range(ahead):
            issue(jnp.minimum(a, nused - 1), a)

    @pl.when(j < nused)
    def _():
        issue(jnp.minimum(j + ahead, nused - 1), (j + ahead) % MOE_NBUF)
        wait(slot)
        x = _load_token_tiles(xbuf.at[slot], MOE_BLK).astype(BF16)
        a = jnp.dot(x, wg_ref[0, 0].astype(BF16), preferred_element_type=F32)
        u = jnp.dot(x, wu_ref[0, 0].astype(BF16), preferred_element_type=F32)
        hm = (_silu(a) * u).astype(BF16)
        y = jnp.dot(hm, wd_ref[0, 0].astype(BF16), preferred_element_type=F32)
        _store_token_tiles(ys8_ref, y, MOE_BLK)

    @pl.when(j == nused - 1)
    def _():
        for a in range(1, MOE_NBUF):
            wait((slot + a) % MOE_NBUF)

    @pl.when(j >= nused)
    def _():
        ys8_ref[...] = jnp.zeros_like(ys8_ref)


def _experts(h8, dest, pstart, counts, block_e, nused, wg, wu, wd, layer, p_slots):
    n_blocks = p_slots // MOE_BLK
    wmap = lambda j, de, ps, ct, be, nu: (layer, be[j], 0, 0)
    grid_spec = pltpu.PrefetchScalarGridSpec(
        num_scalar_prefetch=5,
        grid=(n_blocks,),
        in_specs=[pl.BlockSpec(memory_space=pl.ANY),
                  pl.BlockSpec((1, 1, D_MODEL, D_EXPERT), wmap),
                  pl.BlockSpec((1, 1, D_MODEL, D_EXPERT), wmap),
                  pl.BlockSpec((1, 1, D_EXPERT, D_MODEL), wmap)],
        out_specs=pl.BlockSpec((MOE_BLK * TOK_TILES, LANES), lambda j, *_: (j, 0)),
        scratch_shapes=[pltpu.SMEM((p_slots,), jnp.int32),
                        pltpu.VMEM((MOE_NBUF, MOE_BLK * TOK_TILES, LANES), F32),
                        pltpu.SemaphoreType.DMA((MOE_NBUF,))],
    )
    return pl.pallas_call(
        _experts_kernel,
        grid_spec=grid_spec,
        out_shape=jax.ShapeDtypeStruct((p_slots * TOK_TILES, LANES), F32),
        compiler_params=_cparams(("arbitrary",)),
        name="moe_experts",
    )(dest, pstart, counts, block_e, nused, h8, wg, wu, wd)


def _combine_kernel(dest_ref, h8_ref, gcol_ref, ys8_hbm, lng_ref, lnb_ref, o_ref, buf, sems):
    i = pl.program_id(0)
    nt = pl.num_programs(0)
    n_tok = nt * COMB_TM

    def issue(tile, slot):
        for t in range(COMB_TM):
            for k in range(TOP_K):
                row = dest_ref[k * n_tok + tile * COMB_TM + t]
                _tile_copy(ys8_hbm, row, buf.at[slot, k], t, sems.at[slot]).start()

    def wait(slot):
        for t in range(COMB_TM):
            for k in range(TOP_K):
                _tile_copy(ys8_hbm, 0, buf.at[slot, k], t, sems.at[slot]).wait()

    slot = i % 2

    @pl.when(i == 0)
    def _():
        issue(0, 0)

    issue(jnp.minimum(i + 1, nt - 1), 1 - slot)
    wait(slot)
    g = gcol_ref[...]
    ffn = g[:, 0:1] * _load_token_tiles(buf.at[slot, 0], COMB_TM) + \
        g[:, 1:2] * _load_token_tiles(buf.at[slot, 1], COMB_TM)
    z = ALPHA * _load_token_tiles(h8_ref, COMB_TM) + ffn
    o_ref[...] = _layer_norm(z, lng_ref[...], lnb_ref[...])

    @pl.when(i == nt - 1)
    def _():
        wait(1 - slot)


def _combine(h8, gcol, ys8, dest, ln_g, ln_b):
    n = gcol.shape[0]
    const2 = lambda i, *_: (0, 0)
    grid_spec = pltpu.PrefetchScalarGridSpec(
        num_scalar_prefetch=1,
        grid=(n // COMB_TM,),
        in_specs=[pl.BlockSpec((COMB_TM * TOK_TILES, LANES), lambda i, *_: (i, 0)),
                  pl.BlockSpec((COMB_TM, LANES), lambda i, *_: (i, 0)),
                  pl.BlockSpec(memory_space=pl.ANY),
                  pl.BlockSpec((1, D_MODEL), const2),
                  pl.BlockSpec((1, D_MODEL), const2)],
        out_specs=pl.BlockSpec((COMB_TM, D_MODEL), lambda i, *_: (i, 0)),
        scratch_shapes=[pltpu.VMEM((2, TOP_K, COMB_TM * TOK_TILES, LANES), F32),
                        pltpu.SemaphoreType.DMA((2,))],
    )
    return pl.pallas_call(
        _combine_kernel,
        grid_spec=grid_spec,
        out_shape=jax.ShapeDtypeStruct((n, D_MODEL), F32),
        compiler_params=_cparams(("arbitrary",)),
        name="moe_combine",
    )(dest, h8, gcol, ys8, ln_g, ln_b)


def _moe_and_norm(h8, eidx, rank, gcol, counts, wg, wu, wd, layer, ln_g, ln_b):
    n = gcol.shape[0]
    p_slots = n * TOP_K + N_EXPERTS * MOE_BLK
    dest, pstart, cnt, block_e, nused = _slot_plan(eidx, rank, counts, p_slots // MOE_BLK)
    ys8 = _experts(h8, dest, pstart, cnt, block_e, nused, wg, wu, wd, layer, p_slots)
    return _combine(h8, gcol, ys8, dest, ln_g, ln_b)


def _retention_constants():
    log_gamma = jnp.log1p(-jnp.exp2(-5.0 - jnp.arange(RET_HEADS, dtype=F32)))
    idx = jnp.arange(CHUNK)
    rel = idx[:, None] - idx[None, :]
    decay_in = jnp.where(rel >= 0, jnp.exp(log_gamma[:, None, None] * jnp.maximum(rel, 0)), 0.0)
    xi = jnp.exp(log_gamma[:, None] * (idx + 1))
    zeta = jnp.exp(log_gamma[:, None] * (CHUNK - 1 - idx))
    g_chunk = jnp.exp(log_gamma * CHUNK)
    rc = jnp.zeros((CHUNK, LANES), F32)
    rc = rc.at[:, 0:RET_HEADS].set(xi.T)
    rc = rc.at[:, RET_HEADS:2 * RET_HEADS].set(zeta.T)
    rc = rc.at[:, 2 * RET_HEADS:3 * RET_HEADS].set(jnp.broadcast_to(g_chunk[None, :], (CHUNK, RET_HEADS)))
    return decay_in.astype(F32), rc


def kernel(x, w_in_even, i_bias, f_bias, conv_w, conv_b, mlstm_norm_g, ret_norm_g, w_out_even, w_in_odd,
           w_out_odd, router_w, router_bias, w_gate, w_up, w_down, ln_g, ln_b):
    batch, seq, _ = x.shape
    n = batch * seq
    assert all(w // d == CHUNK for w, d in DIL_CONFIGS) and seq % ATT_SPAN == 0
    h = x.reshape(n, D_MODEL)
    rw_t = router_w.T.astype(F32)
    rw_hi = rw_t.astype(BF16)
    router_wt = jnp.stack([rw_hi, (rw_t - rw_hi.astype(F32)).astype(BF16)])
    router_b = router_bias.astype(F32).reshape(N_EXPERTS, 1)
    decay_in, ret_consts = _retention_constants()
    tri = (jnp.arange(POST_TM)[:, None] <= jnp.arange(POST_TM)[None, :]).astype(BF16)
    att_bias = _attention_bias()
    gate_col0 = 4 * MLSTM_W
    gate_col1 = gate_col0 + 2 * MLSTM_HEADS

    for layer in range(DEPTH):
        j = layer // 2
        lg = lambda s: ln_g[layer, s].reshape(1, D_MODEL)
        lb = lambda s: ln_b[layer, s].reshape(1, D_MODEL)
        if layer % 2 == 0:
            w = w_in_even[j]
            w_re = jnp.concatenate(
                [w[:, :gate_col0], w[:, gate_col1:], w[:, gate_col0:gate_col1],
                 jnp.zeros((D_MODEL, LANES - 2 * MLSTM_HEADS), w.dtype)], axis=1).astype(BF16)
            gate_bias = jnp.zeros((1, LANES), F32)
            gate_bias = gate_bias.at[0, 0:MLSTM_HEADS].set(i_bias[j])
            gate_bias = gate_bias.at[0, MLSTM_HEADS:2 * MLSTM_HEADS].set(f_bias[j])
            norm_g = jnp.concatenate([mlstm_norm_g[j], ret_norm_g[j]]).reshape(1, MLSTM_W + RET_W)
            y = _even_mixer(h, w_re, batch, seq, conv_w[j], conv_b[j].reshape(1, 2 * MLSTM_W), gate_bias,
                            decay_in, ret_consts, norm_g.astype(F32))
            w_out_all = w_out_even
        else:
            p = _project(h, _odd_weights(w_in_odd[j]), "proj_odd")
            y = _dilated_attention(p, att_bias, batch, seq)
            w_out_all = w_out_odd
        h8, eidx, rank, gcol, counts = _post_mixer(h, y, w_out_all, j, lg(0), lb(0),
                                                   router_wt, router_b, tri, f"post_{layer}")
        h = _moe_and_norm(h8, eidx, rank, gcol, counts, w_gate, w_up, w_down, layer, lg(1), lb(1))
    return h.reshape(batch, seq, D_MODEL)
```

```python
import functools

import jax
import jax.numpy as jnp
from jax import lax
from jax.experimental import pallas as pl
from jax.experimental.pallas import tpu as pltpu

F32 = jnp.float32
BF16 = jnp.bfloat16

D_MODEL = 1024
DEPTH = 2
ALPHA = (2 * DEPTH) ** 0.25
LN_EPS = 1e-5
MLSTM_HEADS = 4
RET_HEADS = 4
HEAD_DIM = 128
MLSTM_W = MLSTM_HEADS * HEAD_DIM
RET_W = RET_HEADS * HEAD_DIM
CHUNK = 128
CONV_WIDTH = 4
DIL_HEADS = 8
DIL_DH = 64
DIL_W = DIL_HEADS * DIL_DH
DIL_CONFIGS = ((128, 1), (512, 4), (2048, 16))
N_EXPERTS = 32
N_GROUPS = 4
EXPERTS_PER_GROUP = N_EXPERTS // N_GROUPS
TOP_K = 2
D_EXPERT = 256

LANES = 128
SUBLANES = 8
VMEM_LIMIT_BYTES = 56 * 1024 * 1024

PROJ_TM = 512
POST_TM = 1024
MOE_BLK = 256
MOE_NBUF = 2
COMB_TM = 256
CONV_TAIL = SUBLANES

EV_QM, EV_KM, EV_VM, EV_OM = 0, MLSTM_W, 2 * MLSTM_W, 3 * MLSTM_W
EV_QR, EV_KR, EV_VR, EV_GR = (4 * MLSTM_W, 4 * MLSTM_W + RET_W, 4 * MLSTM_W + 2 * RET_W,
                              4 * MLSTM_W + 3 * RET_W)
EV_GATE = 4 * MLSTM_W + 4 * RET_W
EV_COLS = EV_GATE + LANES

NEG_INF = float("-inf")


def _cparams(sem):
    return pltpu.CompilerParams(dimension_semantics=sem, vmem_limit_bytes=VMEM_LIMIT_BYTES)


def _proj_kernel(x_ref, w_ref, o_ref):
    o_ref[...] = jnp.dot(x_ref[...].astype(BF16), w_ref[...], preferred_element_type=F32)


def _project(x, w, name):
    n, k = x.shape
    m = w.shape[1]
    return pl.pallas_call(
        _proj_kernel,
        grid=(n // PROJ_TM,),
        in_specs=[pl.BlockSpec((PROJ_TM, k), lambda i: (i, 0)),
                  pl.BlockSpec((k, m), lambda i: (0, 0))],
        out_specs=pl.BlockSpec((PROJ_TM, m), lambda i: (i, 0)),
        out_shape=jax.ShapeDtypeStruct((n, m), F32),
        compiler_params=_cparams(("parallel",)),
        name=name,
    )(x, w)


EV_STEP_CHUNKS = 2
EV_PROJ_PIECE = 2 * LANES


def _silu(x):
    return x * jax.nn.sigmoid(x)


def _head_norm(h):
    mu = jnp.mean(h, -1, keepdims=True)
    c = h - mu
    var = jnp.mean(c * c, -1, keepdims=True)
    return c * lax.rsqrt(var + LN_EPS)


def _dot_nt(a, b):
    return lax.dot_general(a, b, (((1,), (1,)), ((), ())), preferred_element_type=F32)


def _dot_tn(a_f32, b):
    return lax.dot_general(a_f32.astype(BF16), b, (((0,), (0,)), ((), ())), preferred_element_type=F32)


def _even_kernel(x0_ref, x_ref, w_ref, convw_ref, convb_ref, gbias_ref, dec_ref, rc_ref, ng_ref, y_ref,
                 p_scr, xb_scr, qkbuf, c_ref, n_ref, m_ref, r_ref, *, steps_per_batch):
    i = pl.program_id(0)

    @pl.when(i == 0)
    def _():
        p_scr[0] = jnp.dot(x0_ref[...].astype(BF16), w_ref[...], preferred_element_type=F32)

    @pl.when(i % steps_per_batch == 0)
    def _():
        qkbuf[0:CONV_TAIL, :] = jnp.zeros((CONV_TAIL, 2 * MLSTM_W), F32)
        c_ref[...] = jnp.zeros_like(c_ref)
        n_ref[...] = jnp.zeros_like(n_ref)
        m_ref[...] = jnp.zeros_like(m_ref)
        r_ref[...] = jnp.zeros_like(r_ref)

    xb_scr[...] = x_ref[...].astype(BF16)
    nxt = (i + 1) % 2
    pieces = iter(range(0, EV_COLS, EV_PROJ_PIECE))

    def project_piece():
        lo = next(pieces, None)
        if lo is not None:
            hi = min(lo + EV_PROJ_PIECE, EV_COLS)
            p_scr[nxt, :, lo:hi] = jnp.dot(xb_scr[...], w_ref[:, lo:hi], preferred_element_type=F32)

    cur = p_scr.at[i % 2]
    for c in range(EV_STEP_CHUNKS):
        rows = pl.ds(c * CHUNK, CHUNK)
        _mixer_chunk(cur.at[rows], y_ref.at[rows], convw_ref, convb_ref, gbias_ref, dec_ref, rc_ref, ng_ref,
                     qkbuf, c_ref, n_ref, m_ref, r_ref, project_piece)
    for _ in pieces:
        raise AssertionError("projection pieces left over: EV_PROJ_PIECE too small for the number of heads")


def _mixer_chunk(p_ref, y_ref, convw_ref, convb_ref, gbias_ref, dec_ref, rc_ref, ng_ref,
                 qkbuf, c_ref, n_ref, m_ref, r_ref, between):
    L = CHUNK
    scale = HEAD_DIM ** -0.5
    between()

    qkbuf[CONV_TAIL:CONV_TAIL + L, :] = p_ref[:, EV_QM:EV_QM + 2 * MLSTM_W]
    acc = jnp.broadcast_to(convb_ref[...], (L, 2 * MLSTM_W))
    for k in range(CONV_WIDTH):
        off = CONV_TAIL - (CONV_WIDTH - 1) + k
        acc = acc + convw_ref[k:k + 1, :] * qkbuf[off:off + L, :]
    qk = _silu(acc)
    qkbuf[0:CONV_TAIL, :] = qkbuf[L:L + CONV_TAIL, :]

    pre = p_ref[:, EV_GATE:EV_GATE + LANES] + gbias_ref[...]
    logf = jnp.minimum(pre, 0.0) - jnp.log1p(jnp.exp(-jnp.abs(pre)))
    row = lax.broadcasted_iota(jnp.int32, (L, L), 0)
    col = lax.broadcasted_iota(jnp.int32, (L, L), 1)
    causal = row >= col
    bcs = jnp.dot(causal.astype(F32), logf, preferred_element_type=F32,
                  precision=lax.Precision.HIGHEST)
    u_t = (pre - pltpu.roll(bcs, LANES - MLSTM_HEADS, 1)).T

    heads = range(MLSTM_HEADS)
    cols = [slice(h * HEAD_DIM, (h + 1) * HEAD_DIM) for h in heads]
    ph1 = []
    for h in heads:
        between()
        b_col = bcs[:, MLSTM_HEADS + h:MLSTM_HEADS + h + 1]
        i_col = pre[:, h:h + 1]
        r_row = u_t[h:h + 1, :]
        m_prev = m_ref[h:h + 1, 0:1]
        log_d = jnp.where(causal, b_col + r_row, NEG_INF)
        a = b_col + m_prev
        m_t = jnp.maximum(a, jnp.max(log_d, -1, keepdims=True))
        ph1.append((b_col, i_col, m_prev, m_t, jnp.exp(log_d - m_t), jnp.exp(a - m_t)))
    ph2 = []
    for h in heads:
        q = qk[:, cols[h]]
        k = qk[:, MLSTM_W + h * HEAD_DIM:MLSTM_W + (h + 1) * HEAD_DIM] * scale
        qb = q.astype(BF16)
        vb = p_ref[:, EV_VM + h * HEAD_DIM:EV_VM + (h + 1) * HEAD_DIM].astype(BF16)
        s = _dot_nt(qb, k.astype(BF16)) * ph1[h][4]
        ph2.append((q, k, qb, vb, s))
    ph3 = []
    for h in heads:
        _, _, _, m_t, _, inter = ph1[h]
        q, k, qb, vb, s = ph2[h]
        c_old = c_ref[h]
        n_old = n_ref[h:h + 1, :]
        num = jnp.dot(s.astype(BF16), vb, preferred_element_type=F32) + \
            inter * jnp.dot(qb, c_old.astype(BF16), preferred_element_type=F32)
        den = jnp.sum(s, -1, keepdims=True) + inter * jnp.sum(q * n_old, -1, keepdims=True)
        ph3.append((num / jnp.maximum(jnp.abs(den), jnp.exp(-m_t)), c_old, n_old))
    for h in heads:
        b_col, i_col, m_prev, _, _, _ = ph1[h]
        _, k, _, vb, _ = ph2[h]
        _, c_old, n_old = ph3[h]
        g = b_col[L - 1:L, :]
        log_w = g - b_col + i_col
        m_new = jnp.maximum(g + m_prev, jnp.max(log_w, 0, keepdims=True))
        kw = k * jnp.exp(log_w - m_new)
        decay = jnp.exp(g + m_prev - m_new)
        c_ref[h] = decay * c_old + _dot_tn(kw, vb)
        n_ref[h:h + 1, :] = decay * n_old + jnp.sum(kw, 0, keepdims=True)
        m_ref[h:h + 1, :] = jnp.broadcast_to(m_new, (1, LANES))
    for h in heads:
        o_gate = jax.nn.sigmoid(p_ref[:, EV_OM + h * HEAD_DIM:EV_OM + (h + 1) * HEAD_DIM])
        y_ref[:, cols[h]] = (_head_norm(o_gate * ph3[h][0]) * ng_ref[:, cols[h]]).astype(y_ref.dtype)

    rheads = range(RET_HEADS)
    rp1 = []
    for h in rheads:
        between()
        lo = h * HEAD_DIM
        qb = p_ref[:, EV_QR + lo:EV_QR + lo + HEAD_DIM].astype(BF16)
        k = p_ref[:, EV_KR + lo:EV_KR + lo + HEAD_DIM] * scale
        vb = p_ref[:, EV_VR + lo:EV_VR + lo + HEAD_DIM].astype(BF16)
        rp1.append((qb, k, vb, _dot_nt(qb, k.astype(BF16)) * dec_ref[h]))
    rp2 = []
    for h in rheads:
        qb, k, vb, inner = rp1[h]
        xi = rc_ref[:, h:h + 1]
        r_old = r_ref[h]
        y = jnp.dot(inner.astype(BF16), vb, preferred_element_type=F32) + \
            xi * jnp.dot(qb, r_old.astype(BF16), preferred_element_type=F32)
        rp2.append((y, r_old))
    for h in rheads:
        _, k, vb, _ = rp1[h]
        zeta = rc_ref[:, RET_HEADS + h:RET_HEADS + h + 1]
        g_chunk = rc_ref[0:1, 2 * RET_HEADS + h:2 * RET_HEADS + h + 1]
        r_ref[h] = g_chunk * rp2[h][1] + _dot_tn(k * zeta, vb)
    for h in rheads:
        lo = h * HEAD_DIM
        gr = p_ref[:, EV_GR + lo:EV_GR + lo + HEAD_DIM]
        out = _head_norm(rp2[h][0]) * ng_ref[:, MLSTM_W + lo:MLSTM_W + lo + HEAD_DIM] * _silu(gr)
        y_ref[:, MLSTM_W + lo:MLSTM_W + lo + HEAD_DIM] = out.astype(y_ref.dtype)


def _even_mixer(x, w, batch, seq, conv_w, conv_b, gate_bias, decay_in, ret_consts, norm_g):
    tm = EV_STEP_CHUNKS * CHUNK
    steps_per_batch = seq // tm
    nsteps = batch * steps_per_batch
    const2 = lambda i: (0, 0)
    return pl.pallas_call(
        functools.partial(_even_kernel, steps_per_batch=steps_per_batch),
        grid=(nsteps,),
        in_specs=[pl.BlockSpec((tm, D_MODEL), const2),
                  pl.BlockSpec((tm, D_MODEL), lambda i: (jnp.minimum(i + 1, nsteps - 1), 0)),
                  pl.BlockSpec((D_MODEL, EV_COLS), const2),
                  pl.BlockSpec((CONV_WIDTH, 2 * MLSTM_W), const2),
                  pl.BlockSpec((1, 2 * MLSTM_W), const2),
                  pl.BlockSpec((1, LANES), const2),
                  pl.BlockSpec((RET_HEADS, CHUNK, CHUNK), lambda i: (0, 0, 0)),
                  pl.BlockSpec((CHUNK, LANES), const2),
                  pl.BlockSpec((1, MLSTM_W + RET_W), const2)],
        out_specs=pl.BlockSpec((tm, MLSTM_W + RET_W), lambda i: (i, 0)),
        out_shape=jax.ShapeDtypeStruct((batch * seq, MLSTM_W + RET_W), BF16),
        scratch_shapes=[pltpu.VMEM((2, tm, EV_COLS), F32),
                        pltpu.VMEM((tm, D_MODEL), BF16),
                        pltpu.VMEM((CONV_TAIL + CHUNK, 2 * MLSTM_W), F32),
                        pltpu.VMEM((MLSTM_HEADS, HEAD_DIM, HEAD_DIM), F32),
                        pltpu.VMEM((SUBLANES, HEAD_DIM), F32),
                        pltpu.VMEM((SUBLANES, LANES), F32),
                        pltpu.VMEM((RET_HEADS, HEAD_DIM, HEAD_DIM), F32)],
        compiler_params=_cparams(("arbitrary",)),
        name="even_mixer",
    )(x, x, w, conv_w, conv_b, gate_bias, decay_in, ret_consts, norm_g)


TOK_TILES = D_MODEL // LANES


def _store_token_tiles(ref, val, rows):
    for c in range(TOK_TILES):
        ref[pl.ds(c, rows, stride=TOK_TILES), :] = val[:, c * LANES:(c + 1) * LANES]


def _load_token_tiles(ref, rows):
    return jnp.concatenate([ref[pl.ds(c, rows, stride=TOK_TILES), :] for c in range(TOK_TILES)], axis=1)


def _layer_norm(z, g, b):
    mu = jnp.mean(z, -1, keepdims=True)
    c = z - mu
    var = jnp.mean(c * c, -1, keepdims=True)
    return c * lax.rsqrt(var + LN_EPS) * g + b


def _route(h, rwt_ref, rb_ref):
    h_hi = h.astype(BF16)
    h_lo = (h - h_hi.astype(F32)).astype(BF16)
    logits = _dot_nt(rwt_ref[0], h_hi) + (_dot_nt(rwt_ref[0], h_lo) + _dot_nt(rwt_ref[1], h_hi))
    aff = jax.nn.sigmoid(logits)
    sel = aff + rb_ref[...]
    tm = h.shape[0]
    sub = lax.broadcasted_iota(jnp.int32, (EXPERTS_PER_GROUP, tm), 0)
    best = None
    for g in range(N_GROUPS):
        lo = g * EXPERTS_PER_GROUP
        sg = sel[lo:lo + EXPERTS_PER_GROUP, :]
        ag = aff[lo:lo + EXPERTS_PER_GROUP, :]
        v1 = jnp.max(sg, 0, keepdims=True)
        i1 = jnp.min(jnp.where(sg == v1, sub, EXPERTS_PER_GROUP), 0, keepdims=True)
        rest = jnp.where(sub == i1, NEG_INF, sg)
        v2 = jnp.max(rest, 0, keepdims=True)
        i2 = jnp.min(jnp.where(rest == v2, sub, EXPERTS_PER_GROUP), 0, keepdims=True)
        a1 = jnp.sum(jnp.where(sub == i1, ag, 0.0), 0, keepdims=True)
        a2 = jnp.sum(jnp.where(sub == i2, ag, 0.0), 0, keepdims=True)
        cand = (v1 + v2, i1 + lo, i2 + lo, a1, a2)
        if best is None:
            best = cand
        else:
            take = cand[0] > best[0]
            best = tuple(jnp.where(take, c, b) for c, b in zip(cand, best))
    _, e1, e2, a1, a2 = best
    tot = a1 + a2
    return e1, e2, a1 / tot, a2 / tot


def _post_kernel(x_ref, y_ref, wo_ref, lng_ref, lnb_ref, rwt_ref, rb_ref, tri_ref,
                 h8_ref, eidx_ref, rank_ref, gcol_ref, cnt_ref, run_ref, wo_scr):
    tm = x_ref.shape[0]

    @pl.when(pl.program_id(0) == 0)
    def _():
        run_ref[...] = jnp.zeros_like(run_ref)
        wo_scr[...] = wo_ref[0].astype(BF16)

    mix = jnp.dot(y_ref[...], wo_scr[...], preferred_element_type=F32)
    h = _layer_norm(ALPHA * x_ref[...] + mix, lng_ref[...], lnb_ref[...])
    _store_token_tiles(h8_ref, h, tm)

    e1, e2, g1, g2 = _route(h, rwt_ref, rb_ref)
    eidx_ref[0:1, :] = e1
    eidx_ref[1:2, :] = e2
    gates = jnp.concatenate([g1, g2, jnp.zeros((LANES - TOP_K, tm), F32)], axis=0)
    gcol_ref[...] = gates.T

    sub = lax.broadcasted_iota(jnp.int32, (N_EXPERTS, tm), 0)
    oh1 = sub == e1
    oh2 = sub == e2
    chosen = jnp.logical_or(oh1, oh2).astype(F32)
    incl = jnp.dot(chosen.astype(BF16), tri_ref[...], preferred_element_type=F32)
    before = run_ref[:, 0:1] + incl - chosen
    rank_ref[0:1, :] = jnp.sum(jnp.where(oh1, before, 0.0), 0, keepdims=True).astype(jnp.int32)
    rank_ref[1:2, :] = jnp.sum(jnp.where(oh2, before, 0.0), 0, keepdims=True).astype(jnp.int32)
    run_new = run_ref[...] + incl[:, tm - 1:tm]
    run_ref[...] = run_new
    cnt_ref[...] = run_new.astype(jnp.int32)


def _post_mixer(x, y, w_out_all, j, ln_g, ln_b, router_wt, router_b, tri, name):
    n = x.shape[0]
    tm = POST_TM
    kdim = w_out_all.shape[1]
    const2 = lambda i: (0, 0)
    return pl.pallas_call(
        _post_kernel,
        grid=(n // tm,),
        in_specs=[pl.BlockSpec((tm, D_MODEL), lambda i: (i, 0)),
                  pl.BlockSpec((tm, y.shape[1]), lambda i: (i, 0)),
                  pl.BlockSpec((1, kdim, D_MODEL), lambda i: (j, 0, 0)),
                  pl.BlockSpec((1, D_MODEL), const2),
                  pl.BlockSpec((1, D_MODEL), const2),
                  pl.BlockSpec((2, N_EXPERTS, D_MODEL), lambda i: (0, 0, 0)),
                  pl.BlockSpec((N_EXPERTS, 1), const2),
                  pl.BlockSpec((tm, tm), const2)],
        out_specs=[pl.BlockSpec((tm * TOK_TILES, LANES), lambda i: (i, 0)),
                   pl.BlockSpec((TOP_K, tm), lambda i: (0, i)),
                   pl.BlockSpec((TOP_K, tm), lambda i: (0, i)),
                   pl.BlockSpec((tm, LANES), lambda i: (i, 0)),
                   pl.BlockSpec((N_EXPERTS, LANES), const2)],
        out_shape=[jax.ShapeDtypeStruct((n * TOK_TILES, LANES), F32),
                   jax.ShapeDtypeStruct((TOP_K, n), jnp.int32),
                   jax.ShapeDtypeStruct((TOP_K, n), jnp.int32),
                   jax.ShapeDtypeStruct((n, LANES), F32),
                   jax.ShapeDtypeStruct((N_EXPERTS, LANES), jnp.int32)],
        scratch_shapes=[pltpu.VMEM((N_EXPERTS, LANES), F32),
                        pltpu.VMEM((kdim, D_MODEL), BF16)],
        compiler_params=_cparams(("arbitrary",)),
        name=name,
    )(x, y, w_out_all, ln_g, ln_b, router_wt, router_b, tri)


ATT_SPAN = max(w for w, _ in DIL_CONFIGS)
ATT_PAIR = 2
ATT_PW = ATT_PAIR * DIL_DH
N_DIL = len(DIL_CONFIGS)


ATT_GROUP = 3


def _for_block_groups(n, body):
    full = n // ATT_GROUP
    if full:
        def trip(t, _):
            body([t * ATT_GROUP + u for u in range(ATT_GROUP)])
            return 0
        lax.fori_loop(0, full, trip, 0)
    if n % ATT_GROUP:
        body(list(range(full * ATT_GROUP, n)))


def _attn_rows(start, stride):
    L = CHUNK
    if stride == 1:
        if not isinstance(start, int):
            start = pl.multiple_of(start, L)
        return pl.ds(start, L), pl.ds(start, 2 * L)
    return pl.ds(start, L, stride=stride), pl.ds(start, 2 * L, stride=stride)


def _attn_units(q_ref, kbuf, vbuf, bias, g, starts, stride, noprev, o_scr, m_scr, l_scr):
    rows = [_attn_rows(s, stride) for s in starts]
    loaded = [(q_ref[qr, :].astype(BF16),
               kbuf[kvr, :].astype(BF16),
               vbuf[kvr, :].astype(BF16)) for qr, kvr in rows]
    scores = [_attn_scores(q2, kb, bias, noprev) for q2, kb, _ in loaded]
    probs = [_attn_softmax(s) for s in scores]
    outs = [_attn_values(p, vb) for (p, _, _), (_, _, vb) in zip(probs, loaded)]
    for (qr, _), o, (_, mm, ll) in zip(rows, outs, probs):
        o_scr[g, qr, :] = o
        m_scr[g, qr, :] = mm
        l_scr[g, qr, :] = ll


def _two_heads(x):
    head0 = lax.broadcasted_iota(jnp.int32, x.shape, 1) < DIL_DH
    zero = jnp.zeros_like(x)
    return jnp.concatenate([jnp.where(head0, x, zero), jnp.where(head0, zero, x)], axis=0)


def _attn_scores(q2, kb, bias, noprev):
    L = CHUNK
    s = _dot_nt(q2, _two_heads(kb)) + bias
    if noprev is not None:
        col = lax.broadcasted_iota(jnp.int32, (L, 4 * L), 1)
        prev_col = (col % (2 * L)) < L
        s = jnp.where(jnp.logical_and(prev_col, noprev), NEG_INF, s)
    return s


def _attn_softmax(s):
    L = CHUNK
    s0, s1 = s[:, 0:2 * L], s[:, 2 * L:4 * L]
    m0 = jnp.max(s0, -1, keepdims=True)
    m1 = jnp.max(s1, -1, keepdims=True)
    p0 = jnp.exp(s0 - m0)
    p1 = jnp.exp(s1 - m1)
    l0 = jnp.sum(p0, -1, keepdims=True)
    l1 = jnp.sum(p1, -1, keepdims=True)
    left = lax.broadcasted_iota(jnp.int32, (L, ATT_PW), 1) < DIL_DH
    p = jnp.concatenate([p0, p1], axis=1).astype(BF16)
    return p, jnp.where(left, m0, m1), jnp.where(left, l0, l1)


def _attn_values(p, vb):
    return jnp.dot(p, _two_heads(vb), preferred_element_type=F32)


def _attn_kernel(*refs):
    q_refs, kc_refs, vc_refs, kp_refs, vp_refs = (refs[i * N_DIL:(i + 1) * N_DIL] for i in range(5))
    bias_ref, y_ref = refs[5 * N_DIL:5 * N_DIL + 2]
    scr = refs[5 * N_DIL + 2:]
    kbufs, vbufs = scr[0:N_DIL], scr[N_DIL:2 * N_DIL]
    o_scr, m_scr, l_scr = scr[2 * N_DIL:]
    first = pl.program_id(2) == 0
    for g, (window, _) in enumerate(DIL_CONFIGS):
        kbufs[g][0:window, :] = kp_refs[g][...]
        vbufs[g][0:window, :] = vp_refs[g][...]
        kbufs[g][window:window + ATT_SPAN, :] = kc_refs[g][...]
        vbufs[g][window:window + ATT_SPAN, :] = vc_refs[g][...]

    for g, (window, dilation) in enumerate(DIL_CONFIGS):
        bias = bias_ref[0, g]
        nsub = ATT_SPAN // window

        def first_blocks(idx, g=g, bias=bias, dilation=dilation):
            _attn_units(q_refs[g], kbufs[g], vbufs[g], bias, g, idx, dilation, first, o_scr, m_scr, l_scr)
        _for_block_groups(dilation, first_blocks)

        def rest_blocks(idx, g=g, bias=bias, window=window, dilation=dilation):
            starts = [(1 + i // dilation) * window + i % dilation for i in idx]
            _attn_units(q_refs[g], kbufs[g], vbufs[g], bias, g, starts, dilation, None, o_scr, m_scr, l_scr)
        _for_block_groups((nsub - 1) * dilation, rest_blocks)

    rows = 256

    def merge(i, _):
        sl = pl.ds(pl.multiple_of(i * rows, rows), rows)
        ms = [m_scr[g, sl, :] for g in range(N_DIL)]
        mx = functools.reduce(jnp.maximum, ms)
        es = [jnp.exp(m - mx) for m in ms]
        num = sum(e * o_scr[g, sl, :] for g, e in enumerate(es))
        den = sum(e * l_scr[g, sl, :] for g, e in enumerate(es))
        y_ref[sl, :] = (num / den).astype(y_ref.dtype)
        return 0
    lax.fori_loop(0, ATT_SPAN // rows, merge, 0)


def _dilated_attention(p, bias, batch, seq):
    n = p.shape[0]
    nspan = seq // ATT_SPAN
    npair = DIL_HEADS // ATT_PAIR
    slab = DIL_W // ATT_PW

    def col(g, c, pair):
        return (g * 3 + c) * slab + pair

    def cur_spec(g, c):
        return pl.BlockSpec((ATT_SPAN, ATT_PW), lambda b, pr, s: (b * nspan + s, col(g, c, pr)))

    def prev_spec(g, c):
        window = DIL_CONFIGS[g][0]
        per_span = ATT_SPAN // window
        rows_per_batch = seq // window
        return pl.BlockSpec(
            (window, ATT_PW),
            lambda b, pr, s: (jnp.maximum(b * rows_per_batch + s * per_span - 1, 0), col(g, c, pr)))

    groups = range(N_DIL)
    in_specs = ([cur_spec(g, 0) for g in groups] + [cur_spec(g, 1) for g in groups] +
                [cur_spec(g, 2) for g in groups] + [prev_spec(g, 1) for g in groups] +
                [prev_spec(g, 2) for g in groups] +
                [pl.BlockSpec((1, N_DIL, CHUNK, 4 * CHUNK), lambda b, pr, s: (pr, 0, 0, 0))])
    return pl.pallas_call(
        _attn_kernel,
        grid=(batch, npair, nspan),
        in_specs=in_specs,
        out_specs=pl.BlockSpec((ATT_SPAN, ATT_PW), lambda b, pr, s: (b * nspan + s, pr)),
        out_shape=jax.ShapeDtypeStruct((n, DIL_W), BF16),
        scratch_shapes=[pltpu.VMEM((w + ATT_SPAN, ATT_PW), F32) for w, _ in DIL_CONFIGS] * 2 +
                       [pltpu.VMEM((N_DIL, ATT_SPAN, ATT_PW), F32) for _ in range(3)],
        compiler_params=_cparams(("parallel", "parallel", "arbitrary")),
        name="dilated_attention",
    )(*([p] * (5 * N_DIL)), bias)


def _attention_bias():
    L = CHUNK
    qi = jnp.arange(L)[:, None]
    kr = jnp.arange(2 * L)[None, :]
    steps = qi + L - kr
    band = (steps >= 0) & (steps <= L)
    slopes = jnp.exp2(-8.0 * jnp.arange(1, DIL_HEADS + 1, dtype=F32) / DIL_HEADS)
    per_group = []
    for _, dilation in DIL_CONFIGS:
        b = -slopes[:, None, None] * (steps * dilation).astype(F32)[None]
        b = jnp.where(band[None], b, NEG_INF)
        per_group.append(b.reshape(DIL_HEADS // ATT_PAIR, ATT_PAIR, L, 2 * L)
                         .transpose(0, 2, 1, 3).reshape(DIL_HEADS // ATT_PAIR, L, 4 * L))
    return jnp.stack(per_group, axis=1)


def _odd_weights(w):
    is_q = (jnp.arange(w.shape[1]) // DIL_W) % 3 == 0
    return (w * jnp.where(is_q, DIL_DH ** -0.5, 1.0)[None, :]).astype(BF16)


def _tile_copy(src_hbm, row, dst, i, sem):
    src = src_hbm.at[pl.ds(pl.multiple_of(row * TOK_TILES, TOK_TILES), TOK_TILES), :]
    return pltpu.make_async_copy(src, dst.at[pl.ds(i * TOK_TILES, TOK_TILES), :], sem)


META_PSTART, META_COUNT, META_NUSED, META_BLOCK_E = 0, 1, 2, 3
META_ROWS = SUBLANES


def _slot_plan_kernel(eidx_ref, rank_ref, cnt_ref, dest_ref, meta_ref, *, n_blocks):
    blk = float(MOE_BLK)
    hi = lax.Precision.HIGHEST
    cnt_b = cnt_ref[...].astype(F32)
    padded_b = jnp.floor((cnt_b + (blk - 1.0)) / blk) * blk
    er = lax.broadcasted_iota(jnp.int32, (LANES, LANES), 0)
    ec = lax.broadcasted_iota(jnp.int32, (LANES, LANES), 1)
    padded_full = jnp.concatenate([padded_b, jnp.zeros((LANES - N_EXPERTS, LANES), F32)], axis=0)
    pstart_b = jnp.dot((ec < er).astype(F32), padded_full, preferred_element_type=F32,
                       precision=hi)[0:N_EXPERTS, :]
    pend_b = pstart_b + padded_b

    sub = lax.broadcasted_iota(jnp.int32, (N_EXPERTS, eidx_ref.shape[1]), 0)
    pcol = pstart_b[:, 0:1]
    for k in range(TOP_K):
        start = jnp.sum(jnp.where(sub == eidx_ref[k:k + 1, :], pcol, 0.0), 0, keepdims=True)
        dest_ref[k:k + 1, :] = start.astype(jnp.int32) + rank_ref[k:k + 1, :]

    def as_row(col_b):
        full = jnp.concatenate([col_b, jnp.zeros((LANES - N_EXPERTS, LANES), F32)], axis=0)
        return full.T[0:1, :]
    meta_ref[...] = jnp.zeros_like(meta_ref)
    meta_ref[META_PSTART:META_PSTART + 1, :] = as_row(pstart_b).astype(jnp.int32)
    meta_ref[META_COUNT:META_COUNT + 1, :] = as_row(cnt_b).astype(jnp.int32)
    total = pend_b[N_EXPERTS - 1:N_EXPERTS, :]
    meta_ref[META_NUSED:META_NUSED + 1, :] = (total / blk).astype(jnp.int32)
    for r in range(-(-n_blocks // LANES)):
        first = (lax.broadcasted_iota(jnp.int32, (N_EXPERTS, LANES), 1) + r * LANES).astype(F32) * blk
        owner = jnp.sum((pend_b[:, 0:1] <= first).astype(jnp.int32), 0, keepdims=True)
        meta_ref[META_BLOCK_E + r:META_BLOCK_E + r + 1, :] = jnp.minimum(owner, N_EXPERTS - 1)


def _slot_plan(eidx, rank, counts, n_blocks):
    n = eidx.shape[1]
    assert META_BLOCK_E + -(-n_blocks // LANES) <= META_ROWS
    dest, meta = pl.pallas_call(
        functools.partial(_slot_plan_kernel, n_blocks=n_blocks),
        out_shape=[jax.ShapeDtypeStruct((TOP_K, n), jnp.int32),
                   jax.ShapeDtypeStruct((META_ROWS, LANES), jnp.int32)],
        compiler_params=pltpu.CompilerParams(vmem_limit_bytes=VMEM_LIMIT_BYTES),
        name="moe_slot_plan",
    )(eidx, rank, counts)
    pstart = meta[META_PSTART, :N_EXPERTS]
    cnt = meta[META_COUNT, :N_EXPERTS]
    nused = meta[META_NUSED, :1]
    block_e = meta[META_BLOCK_E:, :].reshape(-1)[:n_blocks]
    return dest.reshape(-1), pstart, cnt, block_e, nused


def _experts_kernel(dest_ref, pstart_ref, count_ref, block_e_ref, nused_ref,
                    h8_hbm, wg_ref, wu_ref, wd_ref, ys8_ref, slot_tok, xbuf, sems):
    j = pl.program_id(0)
    nused = nused_ref[0]
    n_tok = dest_ref.shape[0] // TOP_K

    @pl.when(j == 0)
    def _():
        def fill(t, _):
            for k in range(TOP_K):
                slot_tok[dest_ref[k * n_tok + t]] = t
            return 0
        lax.fori_loop(0, n_tok, fill, 0, unroll=8)

        def pad_expert(e, _):
            cnt = count_ref[e]
            padded = (cnt + MOE_BLK - 1) // MOE_BLK * MOE_BLK

            def pad(i, _):
                s = pstart_ref[e] + i
                slot_tok[s] = s % n_tok
                return 0
            lax.fori_loop(cnt, padded, pad, 0)
            return 0
        lax.fori_loop(0, N_EXPERTS, pad_expert, 0)

    def issue(blk, slot):
        for i in range(MOE_BLK):
            _tile_copy(h8_hbm, slot_tok[blk * MOE_BLK + i], xbuf.at[slot], i, sems.at[slot]).start()

    def wait(slot):
        for i in range(MOE_BLK):
            _tile_copy(h8_hbm, 0, xbuf.at[slot], i, sems.at[slot]).wait()

    ahead = MOE_NBUF - 1
    slot = j % MOE_NBUF

    @pl.when(jnp.logical_and(j == 0, nused > 0))
    def _():
        for a in range(ahead):
            issue(jnp.minimum(a, nused - 1), a)

    @pl.when(j < nused)
    def _():
        issue(jnp.minimum(j + ahead, nused - 1), (j + ahead) % MOE_NBUF)
        wait(slot)
        x = _load_token_tiles(xbuf.at[slot], MOE_BLK).astype(BF16)
        a = jnp.dot(x, wg_ref[0, 0].astype(BF16), preferred_element_type=F32)
        u = jnp.dot(x, wu_ref[0, 0].astype(BF16), preferred_element_type=F32)
        hm = (_silu(a) * u).astype(BF16)
        y = jnp.dot(hm, wd_ref[0, 0].astype(BF16), preferred_element_type=F32)
        _store_token_tiles(ys8_ref, y, MOE_BLK)

    @pl.when(j == nused - 1)
    def _():
        for a in range(1, MOE_NBUF):
            wait((slot + a) % MOE_NBUF)

    @pl.when(j >= nused)
    def _():
        ys8_ref[...] = jnp.zeros_like(ys8_ref)


def _experts(h8, dest, pstart, counts, block_e, nused, wg, wu, wd, layer, p_slots):
    n_blocks = p_slots // MOE_BLK
    wmap = lambda j, de, ps, ct, be, nu: (layer, be[j], 0, 0)
    grid_spec = pltpu.PrefetchScalarGridSpec(
        num_scalar_prefetch=5,
        grid=(n_blocks,),
        in_specs=[pl.BlockSpec(memory_space=pl.ANY),
                  pl.BlockSpec((1, 1, D_MODEL, D_EXPERT), wmap),
                  pl.BlockSpec((1, 1, D_MODEL, D_EXPERT), wmap),
                  pl.BlockSpec((1, 1, D_EXPERT, D_MODEL), wmap)],
        out_specs=pl.BlockSpec((MOE_BLK * TOK_TILES, LANES), lambda j, *_: (j, 0)),
        scratch_shapes=[pltpu.SMEM((p_slots,), jnp.int32),
                        pltpu.VMEM((MOE_NBUF, MOE_BLK * TOK_TILES, LANES), F32),
                        pltpu.SemaphoreType.DMA((MOE_NBUF,))],
    )
    return pl.pallas_call(
        _experts_kernel,
        grid_spec=grid_spec,
        out_shape=jax.ShapeDtypeStruct((p_slots * TOK_TILES, LANES), F32),
        compiler_params=_cparams(("arbitrary",)),
        name="moe_experts",
    )(dest, pstart, counts, block_e, nused, h8, wg, wu, wd)


def _combine_kernel(dest_ref, h8_ref, gcol_ref, ys8_hbm, lng_ref, lnb_ref, o_ref, buf, sems):
    i = pl.program_id(0)
    nt = pl.num_programs(0)
    n_tok = nt * COMB_TM

    def issue(tile, slot):
        for t in range(COMB_TM):
            for k in range(TOP_K):
                row = dest_ref[k * n_tok + tile * COMB_TM + t]
                _tile_copy(ys8_hbm, row, buf.at[slot, k], t, sems.at[slot]).start()

    def wait(slot):
        for t in range(COMB_TM):
            for k in range(TOP_K):
                _tile_copy(ys8_hbm, 0, buf.at[slot, k], t, sems.at[slot]).wait()

    slot = i % 2

    @pl.when(i == 0)
    def _():
        issue(0, 0)

    issue(jnp.minimum(i + 1, nt - 1), 1 - slot)
    wait(slot)
    g = gcol_ref[...]
    ffn = g[:, 0:1] * _load_token_tiles(buf.at[slot, 0], COMB_TM) + \
        g[:, 1:2] * _load_token_tiles(buf.at[slot, 1], COMB_TM)
    z = ALPHA * _load_token_tiles(h8_ref, COMB_TM) + ffn
    o_ref[...] = _layer_norm(z, lng_ref[...], lnb_ref[...])

    @pl.when(i == nt - 1)
    def _():
        wait(1 - slot)


def _combine(h8, gcol, ys8, dest, ln_g, ln_b):
    n = gcol.shape[0]
    const2 = lambda i, *_: (0, 0)
    grid_spec = pltpu.PrefetchScalarGridSpec(
        num_scalar_prefetch=1,
        grid=(n // COMB_TM,),
        in_specs=[pl.BlockSpec((COMB_TM * TOK_TILES, LANES), lambda i, *_: (i, 0)),
                  pl.BlockSpec((COMB_TM, LANES), lambda i, *_: (i, 0)),
                  pl.BlockSpec(memory_space=pl.ANY),
                  pl.BlockSpec((1, D_MODEL), const2),
                  pl.BlockSpec((1, D_MODEL), const2)],
        out_specs=pl.BlockSpec((COMB_TM, D_MODEL), lambda i, *_: (i, 0)),
        scratch_shapes=[pltpu.VMEM((2, TOP_K, COMB_TM * TOK_TILES, LANES), F32),
                        pltpu.SemaphoreType.DMA((2,))],
    )
    return pl.pallas_call(
        _combine_kernel,
        grid_spec=grid_spec,
        out_shape=jax.ShapeDtypeStruct((n, D_MODEL), F32),
        compiler_params=_cparams(("arbitrary",)),
        name="moe_combine",
    )(dest, h8, gcol, ys8, ln_g, ln_b)


def _moe_and_norm(h8, eidx, rank, gcol, counts, wg, wu, wd, layer, ln_g, ln_b):
    n = gcol.shape[0]
    p_slots = n * TOP_K + N_EXPERTS * MOE_BLK
    dest, pstart, cnt, block_e, nused = _slot_plan(eidx, rank, counts, p_slots // MOE_BLK)
    ys8 = _experts(h8, dest, pstart, cnt, block_e, nused, wg, wu, wd, layer, p_slots)
    return _combine(h8, gcol, ys8, dest, ln_g, ln_b)


def _retention_constants():
    log_gamma = jnp.log1p(-jnp.exp2(-5.0 - jnp.arange(RET_HEADS, dtype=F32)))
    idx = jnp.arange(CHUNK)
    rel = idx[:, None] - idx[None, :]
    decay_in = jnp.where(rel >= 0, jnp.exp(log_gamma[:, None, None] * jnp.maximum(rel, 0)), 0.0)
    xi = jnp.exp(log_gamma[:, None] * (idx + 1))
    zeta = jnp.exp(log_gamma[:, None] * (CHUNK - 1 - idx))
    g_chunk = jnp.exp(log_gamma * CHUNK)
    rc = jnp.zeros((CHUNK, LANES), F32)
    rc = rc.at[:, 0:RET_HEADS].set(xi.T)
    rc = rc.at[:, RET_HEADS:2 * RET_HEADS].set(zeta.T)
    rc = rc.at[:, 2 * RET_HEADS:3 * RET_HEADS].set(jnp.broadcast_to(g_chunk[None, :], (CHUNK, RET_HEADS)))
    return decay_in.astype(F32), rc


def kernel(x, w_in_even, i_bias, f_bias, conv_w, conv_b, mlstm_norm_g, ret_norm_g, w_out_even, w_in_odd,
           w_out_odd, router_w, router_bias, w_gate, w_up, w_down, ln_g, ln_b):
    batch, seq, _ = x.shape
    n = batch * seq
    assert all(w // d == CHUNK for w, d in DIL_CONFIGS) and seq % ATT_SPAN == 0
    h = x.reshape(n, D_MODEL)
    rw_t = router_w.T.astype(F32)
    rw_hi = rw_t.astype(BF16)
    router_wt = jnp.stack([rw_hi, (rw_t - rw_hi.astype(F32)).astype(BF16)])
    router_b = router_bias.astype(F32).reshape(N_EXPERTS, 1)
    decay_in, ret_consts = _retention_constants()
    tri = (jnp.arange(POST_TM)[:, None] <= jnp.arange(POST_TM)[None, :]).astype(BF16)
    att_bias = _attention_bias()
    gate_col0 = 4 * MLSTM_W
    gate_col1 = gate_col0 + 2 * MLSTM_HEADS

    for layer in range(DEPTH):
        j = layer // 2
        lg = lambda s: ln_g[layer, s].reshape(1, D_MODEL)
        lb = lambda s: ln_b[layer, s].reshape(1, D_MODEL)
        if layer % 2 == 0:
            w = w_in_even[j]
            w_re = jnp.concatenate(
                [w[:, :gate_col0], w[:, gate_col1:], w[:, gate_col0:gate_col1],
                 jnp.zeros((D_MODEL, LANES - 2 * MLSTM_HEADS), w.dtype)], axis=1).astype(BF16)
            gate_bias = jnp.zeros((1, LANES), F32)
            gate_bias = gate_bias.at[0, 0:MLSTM_HEADS].set(i_bias[j])
            gate_bias = gate_bias.at[0, MLSTM_HEADS:2 * MLSTM_HEADS].set(f_bias[j])
            norm_g = jnp.concatenate([mlstm_norm_g[j], ret_norm_g[j]]).reshape(1, MLSTM_W + RET_W)
            y = _even_mixer(h, w_re, batch, seq, conv_w[j], conv_b[j].reshape(1, 2 * MLSTM_W), gate_bias,
                            decay_in, ret_consts, norm_g.astype(F32))
            w_out_all = w_out_even
        else:
            p = _project(h, _odd_weights(w_in_odd[j]), "proj_odd")
            y = _dilated_attention(p, att_bias, batch, seq)
            w_out_all = w_out_odd
        h8, eidx, rank, gcol, counts = _post_mixer(h, y, w_out_all, j, lg(0), lb(0),
                                                   router_wt, router_b, tri, f"post_{layer}")
        h = _moe_and_norm(h8, eidx, rank, gcol, counts, w_gate, w_up, w_down, layer, lg(1), lb(1))
    return h.reshape(batch, seq, D_MODEL)
```

```python
import functools

import jax
import jax.numpy as jnp
from jax import lax
from jax.experimental import pallas as pl
from jax.experimental.pallas import tpu as pltpu

F32 = jnp.float32
BF16 = jnp.bfloat16

D_MODEL = 1024
DEPTH = 2
ALPHA = (2 * DEPTH) ** 0.25
LN_EPS = 1e-5
MLSTM_HEADS = 4
RET_HEADS = 4
HEAD_DIM = 128
MLSTM_W = MLSTM_HEADS * HEAD_DIM
RET_W = RET_HEADS * HEAD_DIM
CHUNK = 128
CONV_WIDTH = 4
DIL_HEADS = 8
DIL_DH = 64
DIL_W = DIL_HEADS * DIL_DH
DIL_CONFIGS = ((128, 1), (512, 4), (2048, 16))
N_EXPERTS = 32
N_GROUPS = 4
EXPERTS_PER_GROUP = N_EXPERTS // N_GROUPS
TOP_K = 2
D_EXPERT = 256

LANES = 128
SUBLANES = 8
VMEM_LIMIT_BYTES = 56 * 1024 * 1024

PROJ_TM = 512
POST_TM = 1024
MOE_BLK = 256
MOE_NBUF = 2
COMB_TM = 256
CONV_TAIL = SUBLANES

EV_QM, EV_KM, EV_VM, EV_OM = 0, MLSTM_W, 2 * MLSTM_W, 3 * MLSTM_W
EV_QR, EV_KR, EV_VR, EV_GR = (4 * MLSTM_W, 4 * MLSTM_W + RET_W, 4 * MLSTM_W + 2 * RET_W,
                              4 * MLSTM_W + 3 * RET_W)
EV_GATE = 4 * MLSTM_W + 4 * RET_W
EV_COLS = EV_GATE + LANES

NEG_INF = float("-inf")


def _cparams(sem):
    return pltpu.CompilerParams(dimension_semantics=sem, vmem_limit_bytes=VMEM_LIMIT_BYTES)


def _proj_kernel(x_ref, w_ref, o_ref):
    o_ref[...] = jnp.dot(x_ref[...].astype(BF16), w_ref[...], preferred_element_type=F32)


def _project(x, w, name):
    n, k = x.shape
    m = w.shape[1]
    return pl.pallas_call(
        _proj_kernel,
        grid=(n // PROJ_TM,),
        in_specs=[pl.BlockSpec((PROJ_TM, k), lambda i: (i, 0)),
                  pl.BlockSpec((k, m), lambda i: (0, 0))],
        out_specs=pl.BlockSpec((PROJ_TM, m), lambda i: (i, 0)),
        out_shape=jax.ShapeDtypeStruct((n, m), F32),
        compiler_params=_cparams(("parallel",)),
        name=name,
    )(x, w)


EV_STEP_CHUNKS = 2
EV_PROJ_PIECE = 2 * LANES


def _silu(x):
    return x * jax.nn.sigmoid(x)


def _head_norm(h):
    mu = jnp.mean(h, -1, keepdims=True)
    c = h - mu
    var = jnp.mean(c * c, -1, keepdims=True)
    return c * lax.rsqrt(var + LN_EPS)


def _dot_nt(a, b):
    return lax.dot_general(a, b, (((1,), (1,)), ((), ())), preferred_element_type=F32)


def _dot_tn(a_f32, b):
    return lax.dot_general(a_f32.astype(BF16), b, (((0,), (0,)), ((), ())), preferred_element_type=F32)


def _even_kernel(x0_ref, x_ref, w_ref, convw_ref, convb_ref, gbias_ref, dec_ref, rc_ref, ng_ref, y_ref,
                 p_scr, xb_scr, qkbuf, c_ref, n_ref, m_ref, r_ref, *, steps_per_batch):
    i = pl.program_id(0)

    @pl.when(i == 0)
    def _():
        p_scr[0] = jnp.dot(x0_ref[...].astype(BF16), w_ref[...], preferred_element_type=F32)

    @pl.when(i % steps_per_batch == 0)
    def _():
        qkbuf[0:CONV_TAIL, :] = jnp.zeros((CONV_TAIL, 2 * MLSTM_W), F32)
        c_ref[...] = jnp.zeros_like(c_ref)
        n_ref[...] = jnp.zeros_like(n_ref)
        m_ref[...] = jnp.zeros_like(m_ref)
        r_ref[...] = jnp.zeros_like(r_ref)

    xb_scr[...] = x_ref[...].astype(BF16)
    nxt = (i + 1) % 2
    pieces = iter(range(0, EV_COLS, EV_PROJ_PIECE))

    def project_piece():
        lo = next(pieces, None)
        if lo is not None:
            hi = min(lo + EV_PROJ_PIECE, EV_COLS)
            p_scr[nxt, :, lo:hi] = jnp.dot(xb_scr[...], w_ref[:, lo:hi], preferred_element_type=F32)

    cur = p_scr.at[i % 2]
    for c in range(EV_STEP_CHUNKS):
        rows = pl.ds(c * CHUNK, CHUNK)
        _mixer_chunk(cur.at[rows], y_ref.at[rows], convw_ref, convb_ref, gbias_ref, dec_ref, rc_ref, ng_ref,
                     qkbuf, c_ref, n_ref, m_ref, r_ref, project_piece)
    for _ in pieces:
        raise AssertionError("projection pieces left over: EV_PROJ_PIECE too small for the number of heads")


def _mixer_chunk(p_ref, y_ref, convw_ref, convb_ref, gbias_ref, dec_ref, rc_ref, ng_ref,
                 qkbuf, c_ref, n_ref, m_ref, r_ref, between):
    L = CHUNK
    scale = HEAD_DIM ** -0.5
    between()

    qkbuf[CONV_TAIL:CONV_TAIL + L, :] = p_ref[:, EV_QM:EV_QM + 2 * MLSTM_W]
    acc = jnp.broadcast_to(convb_ref[...], (L, 2 * MLSTM_W))
    for k in range(CONV_WIDTH):
        off = CONV_TAIL - (CONV_WIDTH - 1) + k
        acc = acc + convw_ref[k:k + 1, :] * qkbuf[off:off + L, :]
    qk = _silu(acc)
    qkbuf[0:CONV_TAIL, :] = qkbuf[L:L + CONV_TAIL, :]

    pre = p_ref[:, EV_GATE:EV_GATE + LANES] + gbias_ref[...]
    logf = jnp.minimum(pre, 0.0) - jnp.log1p(jnp.exp(-jnp.abs(pre)))
    row = lax.broadcasted_iota(jnp.int32, (L, L), 0)
    col = lax.broadcasted_iota(jnp.int32, (L, L), 1)
    causal = row >= col
    bcs = jnp.dot(causal.astype(F32), logf, preferred_element_type=F32,
                  precision=lax.Precision.HIGHEST)
    u_t = (pre - pltpu.roll(bcs, LANES - MLSTM_HEADS, 1)).T

    heads = range(MLSTM_HEADS)
    cols = [slice(h * HEAD_DIM, (h + 1) * HEAD_DIM) for h in heads]
    ph1 = []
    for h in heads:
        between()
        b_col = bcs[:, MLSTM_HEADS + h:MLSTM_HEADS + h + 1]
        i_col = pre[:, h:h + 1]
        r_row = u_t[h:h + 1, :]
        m_prev = m_ref[h:h + 1, 0:1]
        log_d = jnp.where(causal, b_col + r_row, NEG_INF)
        a = b_col + m_prev
        m_t = jnp.maximum(a, jnp.max(log_d, -1, keepdims=True))
        ph1.append((b_col, i_col, m_prev, m_t, jnp.exp(log_d - m_t), jnp.exp(a - m_t)))
    ph2 = []
    for h in heads:
        q = qk[:, cols[h]]
        k = qk[:, MLSTM_W + h * HEAD_DIM:MLSTM_W + (h + 1) * HEAD_DIM] * scale
        qb = q.astype(BF16)
        vb = p_ref[:, EV_VM + h * HEAD_DIM:EV_VM + (h + 1) * HEAD_DIM].astype(BF16)
        s = _dot_nt(qb, k.astype(BF16)) * ph1[h][4]
        ph2.append((q, k, qb, vb, s))
    ph3 = []
    for h in heads:
        _, _, _, m_t, _, inter = ph1[h]
        q, k, qb, vb, s = ph2[h]
        c_old = c_ref[h]
        n_old = n_ref[h:h + 1, :]
        num = jnp.dot(s.astype(BF16), vb, preferred_element_type=F32) + \
            inter * jnp.dot(qb, c_old.astype(BF16), preferred_element_type=F32)
        den = jnp.sum(s, -1, keepdims=True) + inter * jnp.sum(q * n_old, -1, keepdims=True)
        ph3.append((num / jnp.maximum(jnp.abs(den), jnp.exp(-m_t)), c_old, n_old))
    for h in heads:
        b_col, i_col, m_prev, _, _, _ = ph1[h]
        _, k, _, vb, _ = ph2[h]
        _, c_old, n_old = ph3[h]
        g = b_col[L - 1:L, :]
        log_w = g - b_col + i_col
        m_new = jnp.maximum(g + m_prev, jnp.max(log_w, 0, keepdims=True))
        kw = k * jnp.exp(log_w - m_new)
        decay = jnp.exp(g + m_prev - m_new)
        c_ref[h] = decay * c_old + _dot_tn(kw, vb)
        n_ref[h:h + 1, :] = decay * n_old + jnp.sum(kw, 0, keepdims=True)
        m_ref[h:h + 1, :] = jnp.broadcast_to(m_new, (1, LANES))
    for h in heads:
        o_gate = jax.nn.sigmoid(p_ref[:, EV_OM + h * HEAD_DIM:EV_OM + (h + 1) * HEAD_DIM])
        y_ref[:, cols[h]] = (_head_norm(o_gate * ph3[h][0]) * ng_ref[:, cols[h]]).astype(y_ref.dtype)

    rheads = range(RET_HEADS)
    rp1 = []
    for h in rheads:
        between()
        lo = h * HEAD_DIM
        qb = p_ref[:, EV_QR + lo:EV_QR + lo + HEAD_DIM].astype(BF16)
        k = p_ref[:, EV_KR + lo:EV_KR + lo + HEAD_DIM] * scale
        vb = p_ref[:, EV_VR + lo:EV_VR + lo + HEAD_DIM].astype(BF16)
        rp1.append((qb, k, vb, _dot_nt(qb, k.astype(BF16)) * dec_ref[h]))
    rp2 = []
    for h in rheads:
        qb, k, vb, inner = rp1[h]
        xi = rc_ref[:, h:h + 1]
        r_old = r_ref[h]
        y = jnp.dot(inner.astype(BF16), vb, preferred_element_type=F32) + \
            xi * jnp.dot(qb, r_old.astype(BF16), preferred_element_type=F32)
        rp2.append((y, r_old))
    for h in rheads:
        _, k, vb, _ = rp1[h]
        zeta = rc_ref[:, RET_HEADS + h:RET_HEADS + h + 1]
        g_chunk = rc_ref[0:1, 2 * RET_HEADS + h:2 * RET_HEADS + h + 1]
        r_ref[h] = g_chunk * rp2[h][1] + _dot_tn(k * zeta, vb)
    for h in rheads:
        lo = h * HEAD_DIM
        gr = p_ref[:, EV_GR + lo:EV_GR + lo + HEAD_DIM]
        out = _head_norm(rp2[h][0]) * ng_ref[:, MLSTM_W + lo:MLSTM_W + lo + HEAD_DIM] * _silu(gr)
        y_ref[:, MLSTM_W + lo:MLSTM_W + lo + HEAD_DIM] = out.astype(y_ref.dtype)


def _even_mixer(x, w, batch, seq, conv_w, conv_b, gate_bias, decay_in, ret_consts, norm_g):
    tm = EV_STEP_CHUNKS * CHUNK
    steps_per_batch = seq // tm
    nsteps = batch * steps_per_batch
    const2 = lambda i: (0, 0)
    return pl.pallas_call(
        functools.partial(_even_kernel, steps_per_batch=steps_per_batch),
        grid=(nsteps,),
        in_specs=[pl.BlockSpec((tm, D_MODEL), const2),
                  pl.BlockSpec((tm, D_MODEL), lambda i: (jnp.minimum(i + 1, nsteps - 1), 0)),
                  pl.BlockSpec((D_MODEL, EV_COLS), const2),
                  pl.BlockSpec((CONV_WIDTH, 2 * MLSTM_W), const2),
                  pl.BlockSpec((1, 2 * MLSTM_W), const2),
                  pl.BlockSpec((1, LANES), const2),
                  pl.BlockSpec((RET_HEADS, CHUNK, CHUNK), lambda i: (0, 0, 0)),
                  pl.BlockSpec((CHUNK, LANES), const2),
                  pl.BlockSpec((1, MLSTM_W + RET_W), const2)],
        out_specs=pl.BlockSpec((tm, MLSTM_W + RET_W), lambda i: (i, 0)),
        out_shape=jax.ShapeDtypeStruct((batch * seq, MLSTM_W + RET_W), BF16),
        scratch_shapes=[pltpu.VMEM((2, tm, EV_COLS), F32),
                        pltpu.VMEM((tm, D_MODEL), BF16),
                        pltpu.VMEM((CONV_TAIL + CHUNK, 2 * MLSTM_W), F32),
                        pltpu.VMEM((MLSTM_HEADS, HEAD_DIM, HEAD_DIM), F32),
                        pltpu.VMEM((SUBLANES, HEAD_DIM), F32),
                        pltpu.VMEM((SUBLANES, LANES), F32),
                        pltpu.VMEM((RET_HEADS, HEAD_DIM, HEAD_DIM), F32)],
        compiler_params=_cparams(("arbitrary",)),
        name="even_mixer",
    )(x, x, w, conv_w, conv_b, gate_bias, decay_in, ret_consts, norm_g)


TOK_TILES = D_MODEL // LANES


def _store_token_tiles(ref, val, rows):
    for c in range(TOK_TILES):
        ref[pl.ds(c, rows, stride=TOK_TILES), :] = val[:, c * LANES:(c + 1) * LANES]


def _load_token_tiles(ref, rows):
    return jnp.concatenate([ref[pl.ds(c, rows, stride=TOK_TILES), :] for c in range(TOK_TILES)], axis=1)


def _layer_norm(z, g, b):
    mu = jnp.mean(z, -1, keepdims=True)
    c = z - mu
    var = jnp.mean(c * c, -1, keepdims=True)
    return c * lax.rsqrt(var + LN_EPS) * g + b


def _route(h, rwt_ref, rb_ref):
    h_hi = h.astype(BF16)
    h_lo = (h - h_hi.astype(F32)).astype(BF16)
    logits = _dot_nt(rwt_ref[0], h_hi) + (_dot_nt(rwt_ref[0], h_lo) + _dot_nt(rwt_ref[1], h_hi))
    aff = jax.nn.sigmoid(logits)
    sel = aff + rb_ref[...]
    tm = h.shape[0]
    sub = lax.broadcasted_iota(jnp.int32, (EXPERTS_PER_GROUP, tm), 0)
    best = None
    for g in range(N_GROUPS):
        lo = g * EXPERTS_PER_GROUP
        sg = sel[lo:lo + EXPERTS_PER_GROUP, :]
        ag = aff[lo:lo + EXPERTS_PER_GROUP, :]
        v1 = jnp.max(sg, 0, keepdims=True)
        i1 = jnp.min(jnp.where(sg == v1, sub, EXPERTS_PER_GROUP), 0, keepdims=True)
        rest = jnp.where(sub == i1, NEG_INF, sg)
        v2 = jnp.max(rest, 0, keepdims=True)
        i2 = jnp.min(jnp.where(rest == v2, sub, EXPERTS_PER_GROUP), 0, keepdims=True)
        a1 = jnp.sum(jnp.where(sub == i1, ag, 0.0), 0, keepdims=True)
        a2 = jnp.sum(jnp.where(sub == i2, ag, 0.0), 0, keepdims=True)
        cand = (v1 + v2, i1 + lo, i2 + lo, a1, a2)
        if best is None:
            best = cand
        else:
            take = cand[0] > best[0]
            best = tuple(jnp.where(take, c, b) for c, b in zip(cand, best))
    _, e1, e2, a1, a2 = best
    tot = a1 + a2
    return e1, e2, a1 / tot, a2 / tot


def _post_kernel(x_ref, y_ref, wo_ref, lng_ref, lnb_ref, rwt_ref, rb_ref, tri_ref,
                 h8_ref, eidx_ref, rank_ref, gcol_ref, cnt_ref, run_ref, wo_scr):
    tm = x_ref.shape[0]

    @pl.when(pl.program_id(0) == 0)
    def _():
        run_ref[...] = jnp.zeros_like(run_ref)
        wo_scr[...] = wo_ref[0].astype(BF16)

    mix = jnp.dot(y_ref[...], wo_scr[...], preferred_element_type=F32)
    h = _layer_norm(ALPHA * x_ref[...] + mix, lng_ref[...], lnb_ref[...])
    _store_token_tiles(h8_ref, h, tm)

    e1, e2, g1, g2 = _route(h, rwt_ref, rb_ref)
    eidx_ref[0:1, :] = e1
    eidx_ref[1:2, :] = e2
    gates = jnp.concatenate([g1, g2, jnp.zeros((LANES - TOP_K, tm), F32)], axis=0)
    gcol_ref[...] = gates.T

    sub = lax.broadcasted_iota(jnp.int32, (N_EXPERTS, tm), 0)
    oh1 = sub == e1
    oh2 = sub == e2
    chosen = jnp.logical_or(oh1, oh2).astype(F32)
    incl = jnp.dot(chosen.astype(BF16), tri_ref[...], preferred_element_type=F32)
    before = run_ref[:, 0:1] + incl - chosen
    rank_ref[0:1, :] = jnp.sum(jnp.where(oh1, before, 0.0), 0, keepdims=True).astype(jnp.int32)
    rank_ref[1:2, :] = jnp.sum(jnp.where(oh2, before, 0.0), 0, keepdims=True).astype(jnp.int32)
    run_new = run_ref[...] + incl[:, tm - 1:tm]
    run_ref[...] = run_new
    cnt_ref[...] = run_new.astype(jnp.int32)


def _post_mixer(x, y, w_out_all, j, ln_g, ln_b, router_wt, router_b, tri, name):
    n = x.shape[0]
    tm = POST_TM
    kdim = w_out_all.shape[1]
    const2 = lambda i: (0, 0)
    return pl.pallas_call(
        _post_kernel,
        grid=(n // tm,),
        in_specs=[pl.BlockSpec((tm, D_MODEL), lambda i: (i, 0)),
                  pl.BlockSpec((tm, y.shape[1]), lambda i: (i, 0)),
                  pl.BlockSpec((1, kdim, D_MODEL), lambda i: (j, 0, 0)),
                  pl.BlockSpec((1, D_MODEL), const2),
                  pl.BlockSpec((1, D_MODEL), const2),
                  pl.BlockSpec((2, N_EXPERTS, D_MODEL), lambda i: (0, 0, 0)),
                  pl.BlockSpec((N_EXPERTS, 1), const2),
                  pl.BlockSpec((tm, tm), const2)],
        out_specs=[pl.BlockSpec((tm * TOK_TILES, LANES), lambda i: (i, 0)),
                   pl.BlockSpec((TOP_K, tm), lambda i: (0, i)),
                   pl.BlockSpec((TOP_K, tm), lambda i: (0, i)),
                   pl.BlockSpec((tm, LANES), lambda i: (i, 0)),
                   pl.BlockSpec((N_EXPERTS, LANES), const2)],
        out_shape=[jax.ShapeDtypeStruct((n * TOK_TILES, LANES), F32),
                   jax.ShapeDtypeStruct((TOP_K, n), jnp.int32),
                   jax.ShapeDtypeStruct((TOP_K, n), jnp.int32),
                   jax.ShapeDtypeStruct((n, LANES), F32),
                   jax.ShapeDtypeStruct((N_EXPERTS, LANES), jnp.int32)],
        scratch_shapes=[pltpu.VMEM((N_EXPERTS, LANES), F32),
                        pltpu.VMEM((kdim, D_MODEL), BF16)],
        compiler_params=_cparams(("arbitrary",)),
        name=name,
    )(x, y, w_out_all, ln_g, ln_b, router_wt, router_b, tri)


ATT_SPAN = max(w for w, _ in DIL_CONFIGS)
ATT_PAIR = 2
ATT_PW = ATT_PAIR * DIL_DH
N_DIL = len(DIL_CONFIGS)


ATT_GROUP = 4


def _for_block_groups(n, body):
    full = n // ATT_GROUP
    if full:
        def trip(t, _):
            body([t * ATT_GROUP + u for u in range(ATT_GROUP)])
            return 0
        lax.fori_loop(0, full, trip, 0)
    if n % ATT_GROUP:
        body(list(range(full * ATT_GROUP, n)))


def _attn_rows(start, stride):
    L = CHUNK
    if stride == 1:
        if not isinstance(start, int):
            start = pl.multiple_of(start, L)
        return pl.ds(start, L), pl.ds(start, 2 * L)
    return pl.ds(start, L, stride=stride), pl.ds(start, 2 * L, stride=stride)


def _attn_units(q_ref, kbuf, vbuf, bias, g, starts, stride, noprev, o_scr, m_scr, l_scr):
    rows = [_attn_rows(s, stride) for s in starts]
    loaded = [(q_ref[qr, :].astype(BF16),
               kbuf[kvr, :].astype(BF16),
               vbuf[kvr, :].astype(BF16)) for qr, kvr in rows]
    scores = [_attn_scores(q2, kb, bias, noprev) for q2, kb, _ in loaded]
    probs = [_attn_softmax(s) for s in scores]
    outs = [_attn_values(p, vb) for (p, _, _), (_, _, vb) in zip(probs, loaded)]
    for (qr, _), o, (_, mm, ll) in zip(rows, outs, probs):
        o_scr[g, qr, :] = o
        m_scr[g, qr, :] = mm
        l_scr[g, qr, :] = ll


def _two_heads(x):
    head0 = lax.broadcasted_iota(jnp.int32, x.shape, 1) < DIL_DH
    zero = jnp.zeros_like(x)
    return jnp.concatenate([jnp.where(head0, x, zero), jnp.where(head0, zero, x)], axis=0)


def _attn_scores(q2, kb, bias, noprev):
    L = CHUNK
    s = _dot_nt(q2, _two_heads(kb)) + bias
    if noprev is not None:
        col = lax.broadcasted_iota(jnp.int32, (L, 4 * L), 1)
        prev_col = (col % (2 * L)) < L
        s = jnp.where(jnp.logical_and(prev_col, noprev), NEG_INF, s)
    return s


def _attn_softmax(s):
    L = CHUNK
    s0, s1 = s[:, 0:2 * L], s[:, 2 * L:4 * L]
    m0 = jnp.max(s0, -1, keepdims=True)
    m1 = jnp.max(s1, -1, keepdims=True)
    p0 = jnp.exp(s0 - m0)
    p1 = jnp.exp(s1 - m1)
    l0 = jnp.sum(p0, -1, keepdims=True)
    l1 = jnp.sum(p1, -1, keepdims=True)
    left = lax.broadcasted_iota(jnp.int32, (L, ATT_PW), 1) < DIL_DH
    p = jnp.concatenate([p0, p1], axis=1).astype(BF16)
    return p, jnp.where(left, m0, m1), jnp.where(left, l0, l1)


def _attn_values(p, vb):
    return jnp.dot(p, _two_heads(vb), preferred_element_type=F32)


def _attn_kernel(*refs):
    q_refs, kc_refs, vc_refs, kp_refs, vp_refs = (refs[i * N_DIL:(i + 1) * N_DIL] for i in range(5))
    bias_ref, y_ref = refs[5 * N_DIL:5 * N_DIL + 2]
    scr = refs[5 * N_DIL + 2:]
    kbufs, vbufs = scr[0:N_DIL], scr[N_DIL:2 * N_DIL]
    o_scr, m_scr, l_scr = scr[2 * N_DIL:]
    first = pl.program_id(2) == 0
    for g, (window, _) in enumerate(DIL_CONFIGS):
        kbufs[g][0:window, :] = kp_refs[g][...]
        vbufs[g][0:window, :] = vp_refs[g][...]
        kbufs[g][window:window + ATT_SPAN, :] = kc_refs[g][...]
        vbufs[g][window:window + ATT_SPAN, :] = vc_refs[g][...]

    for g, (window, dilation) in enumerate(DIL_CONFIGS):
        bias = bias_ref[0, g]
        nsub = ATT_SPAN // window

        def first_blocks(idx, g=g, bias=bias, dilation=dilation):
            _attn_units(q_refs[g], kbufs[g], vbufs[g], bias, g, idx, dilation, first, o_scr, m_scr, l_scr)
        _for_block_groups(dilation, first_blocks)

        def rest_blocks(idx, g=g, bias=bias, window=window, dilation=dilation):
            starts = [(1 + i // dilation) * window + i % dilation for i in idx]
            _attn_units(q_refs[g], kbufs[g], vbufs[g], bias, g, starts, dilation, None, o_scr, m_scr, l_scr)
        _for_block_groups((nsub - 1) * dilation, rest_blocks)

    rows = 256

    def merge(i, _):
        sl = pl.ds(pl.multiple_of(i * rows, rows), rows)
        ms = [m_scr[g, sl, :] for g in range(N_DIL)]
        mx = functools.reduce(jnp.maximum, ms)
        es = [jnp.exp(m - mx) for m in ms]
        num = sum(e * o_scr[g, sl, :] for g, e in enumerate(es))
        den = sum(e * l_scr[g, sl, :] for g, e in enumerate(es))
        y_ref[sl, :] = (num / den).astype(y_ref.dtype)
        return 0
    lax.fori_loop(0, ATT_SPAN // rows, merge, 0)


def _dilated_attention(p, bias, batch, seq):
    n = p.shape[0]
    nspan = seq // ATT_SPAN
    npair = DIL_HEADS // ATT_PAIR
    slab = DIL_W // ATT_PW

    def col(g, c, pair):
        return (g * 3 + c) * slab + pair

    def cur_spec(g, c):
        return pl.BlockSpec((ATT_SPAN, ATT_PW), lambda b, pr, s: (b * nspan + s, col(g, c, pr)))

    def prev_spec(g, c):
        window = DIL_CONFIGS[g][0]
        per_span = ATT_SPAN // window
        rows_per_batch = seq // window
        return pl.BlockSpec(
            (window, ATT_PW),
            lambda b, pr, s: (jnp.maximum(b * rows_per_batch + s * per_span - 1, 0), col(g, c, pr)))

    groups = range(N_DIL)
    in_specs = ([cur_spec(g, 0) for g in groups] + [cur_spec(g, 1) for g in groups] +
                [cur_spec(g, 2) for g in groups] + [prev_spec(g, 1) for g in groups] +
                [prev_spec(g, 2) for g in groups] +
                [pl.BlockSpec((1, N_DIL, CHUNK, 4 * CHUNK), lambda b, pr, s: (pr, 0, 0, 0))])
    return pl.pallas_call(
        _attn_kernel,
        grid=(batch, npair, nspan),
        in_specs=in_specs,
        out_specs=pl.BlockSpec((ATT_SPAN, ATT_PW), lambda b, pr, s: (b * nspan + s, pr)),
        out_shape=jax.ShapeDtypeStruct((n, DIL_W), BF16),
        scratch_shapes=[pltpu.VMEM((w + ATT_SPAN, ATT_PW), F32) for w, _ in DIL_CONFIGS] * 2 +
                       [pltpu.VMEM((N_DIL, ATT_SPAN, ATT_PW), F32) for _ in range(3)],
        compiler_params=_cparams(("parallel", "parallel", "arbitrary")),
        name="dilated_attention",
    )(*([p] * (5 * N_DIL)), bias)


def _attention_bias():
    L = CHUNK
    qi = jnp.arange(L)[:, None]
    kr = jnp.arange(2 * L)[None, :]
    steps = qi + L - kr
    band = (steps >= 0) & (steps <= L)
    slopes = jnp.exp2(-8.0 * jnp.arange(1, DIL_HEADS + 1, dtype=F32) / DIL_HEADS)
    per_group = []
    for _, dilation in DIL_CONFIGS:
        b = -slopes[:, None, None] * (steps * dilation).astype(F32)[None]
        b = jnp.where(band[None], b, NEG_INF)
        per_group.append(b.reshape(DIL_HEADS // ATT_PAIR, ATT_PAIR, L, 2 * L)
                         .transpose(0, 2, 1, 3).reshape(DIL_HEADS // ATT_PAIR, L, 4 * L))
    return jnp.stack(per_group, axis=1)


def _odd_weights(w):
    is_q = (jnp.arange(w.shape[1]) // DIL_W) % 3 == 0
    return (w * jnp.where(is_q, DIL_DH ** -0.5, 1.0)[None, :]).astype(BF16)


def _tile_copy(src_hbm, row, dst, i, sem):
    src = src_hbm.at[pl.ds(pl.multiple_of(row * TOK_TILES, TOK_TILES), TOK_TILES), :]
    return pltpu.make_async_copy(src, dst.at[pl.ds(i * TOK_TILES, TOK_TILES), :], sem)


META_PSTART, META_COUNT, META_NUSED, META_BLOCK_E = 0, 1, 2, 3
META_ROWS = SUBLANES


def _slot_plan_kernel(eidx_ref, rank_ref, cnt_ref, dest_ref, meta_ref, *, n_blocks):
    blk = float(MOE_BLK)
    hi = lax.Precision.HIGHEST
    cnt_b = cnt_ref[...].astype(F32)
    padded_b = jnp.floor((cnt_b + (blk - 1.0)) / blk) * blk
    er = lax.broadcasted_iota(jnp.int32, (LANES, LANES), 0)
    ec = lax.broadcasted_iota(jnp.int32, (LANES, LANES), 1)
    padded_full = jnp.concatenate([padded_b, jnp.zeros((LANES - N_EXPERTS, LANES), F32)], axis=0)
    pstart_b = jnp.dot((ec < er).astype(F32), padded_full, preferred_element_type=F32,
                       precision=hi)[0:N_EXPERTS, :]
    pend_b = pstart_b + padded_b

    sub = lax.broadcasted_iota(jnp.int32, (N_EXPERTS, eidx_ref.shape[1]), 0)
    pcol = pstart_b[:, 0:1]
    for k in range(TOP_K):
        start = jnp.sum(jnp.where(sub == eidx_ref[k:k + 1, :], pcol, 0.0), 0, keepdims=True)
        dest_ref[k:k + 1, :] = start.astype(jnp.int32) + rank_ref[k:k + 1, :]

    def as_row(col_b):
        full = jnp.concatenate([col_b, jnp.zeros((LANES - N_EXPERTS, LANES), F32)], axis=0)
        return full.T[0:1, :]
    meta_ref[...] = jnp.zeros_like(meta_ref)
    meta_ref[META_PSTART:META_PSTART + 1, :] = as_row(pstart_b).astype(jnp.int32)
    meta_ref[META_COUNT:META_COUNT + 1, :] = as_row(cnt_b).astype(jnp.int32)
    total = pend_b[N_EXPERTS - 1:N_EXPERTS, :]
    meta_ref[META_NUSED:META_NUSED + 1, :] = (total / blk).astype(jnp.int32)
    for r in range(-(-n_blocks // LANES)):
        first = (lax.broadcasted_iota(jnp.int32, (N_EXPERTS, LANES), 1) + r * LANES).astype(F32) * blk
        owner = jnp.sum((pend_b[:, 0:1] <= first).astype(jnp.int32), 0, keepdims=True)
        meta_ref[META_BLOCK_E + r:META_BLOCK_E + r + 1, :] = jnp.minimum(owner, N_EXPERTS - 1)


def _slot_plan(eidx, rank, counts, n_blocks):
    n = eidx.shape[1]
    assert META_BLOCK_E + -(-n_blocks // LANES) <= META_ROWS
    dest, meta = pl.pallas_call(
        functools.partial(_slot_plan_kernel, n_blocks=n_blocks),
        out_shape=[jax.ShapeDtypeStruct((TOP_K, n), jnp.int32),
                   jax.ShapeDtypeStruct((META_ROWS, LANES), jnp.int32)],
        compiler_params=pltpu.CompilerParams(vmem_limit_bytes=VMEM_LIMIT_BYTES),
        name="moe_slot_plan",
    )(eidx, rank, counts)
    pstart = meta[META_PSTART, :N_EXPERTS]
    cnt = meta[META_COUNT, :N_EXPERTS]
    nused = meta[META_NUSED, :1]
    block_e = meta[META_BLOCK_E:, :].reshape(-1)[:n_blocks]
    return dest.reshape(-1), pstart, cnt, block_e, nused


def _experts_kernel(dest_ref, pstart_ref, count_ref, block_e_ref, nused_ref,
                    h8_hbm, wg_ref, wu_ref, wd_ref, ys8_ref, slot_tok, xbuf, sems):
    j = pl.program_id(0)
    nused = nused_ref[0]
    n_tok = dest_ref.shape[0] // TOP_K

    @pl.when(j == 0)
    def _():
        def fill(t, _):
            for k in range(TOP_K):
                slot_tok[dest_ref[k * n_tok + t]] = t
            return 0
        lax.fori_loop(0, n_tok, fill, 0, unroll=8)

        def pad_expert(e, _):
            cnt = count_ref[e]
            padded = (cnt + MOE_BLK - 1) // MOE_BLK * MOE_BLK

            def pad(i, _):
                s = pstart_ref[e] + i
                slot_tok[s] = s % n_tok
                return 0
            lax.fori_loop(cnt, padded, pad, 0)
            return 0
        lax.fori_loop(0, N_EXPERTS, pad_expert, 0)

    def issue(blk, slot):
        for i in range(MOE_BLK):
            _tile_copy(h8_hbm, slot_tok[blk * MOE_BLK + i], xbuf.at[slot], i, sems.at[slot]).start()

    def wait(slot):
        for i in range(MOE_BLK):
            _tile_copy(h8_hbm, 0, xbuf.at[slot], i, sems.at[slot]).wait()

    ahead = MOE_NBUF - 1
    slot = j % MOE_NBUF

    @pl.when(jnp.logical_and(j == 0, nused > 0))
    def _():
        for a in range(ahead):
            issue(jnp.minimum(a, nused - 1), a)

    @pl.when(j < nused)
    def _():
        issue(jnp.minimum(j + ahead, nused - 1), (j + ahead) % MOE_NBUF)
        wait(slot)
        x = _load_token_tiles(xbuf.at[slot], MOE_BLK).astype(BF16)
        a = jnp.dot(x, wg_ref[0, 0].astype(BF16), preferred_element_type=F32)
        u = jnp.dot(x, wu_ref[0, 0].astype(BF16), preferred_element_type=F32)
        hm = (_silu(a) * u).astype(BF16)
        y = jnp.dot(hm, wd_ref[0, 0].astype(BF16), preferred_element_type=F32)
        _store_token_tiles(ys8_ref, y, MOE_BLK)

    @pl.when(j == nused - 1)
    def _():
        for a in range(1, MOE_NBUF):
            wait((slot + a) % MOE_NBUF)

    @pl.when(j >= nused)
    def _():
        ys8_ref[...] = jnp.zeros_like(ys8_ref)


def _experts(h8, dest, pstart, counts, block_e, nused, wg, wu, wd, layer, p_slots):
    n_blocks = p_slots // MOE_BLK
    wmap = lambda j, de, ps, ct, be, nu: (layer, be[j], 0, 0)
    grid_spec = pltpu.PrefetchScalarGridSpec(
        num_scalar_prefetch=5,
        grid=(n_blocks,),
        in_specs=[pl.BlockSpec(memory_space=pl.ANY),
                  pl.BlockSpec((1, 1, D_MODEL, D_EXPERT), wmap),
                  pl.BlockSpec((1, 1, D_MODEL, D_EXPERT), wmap),
                  pl.BlockSpec((1, 1, D_EXPERT, D_MODEL), wmap)],
        out_specs=pl.BlockSpec((MOE_BLK * TOK_TILES, LANES), lambda j, *_: (j, 0)),
        scratch_shapes=[pltpu.SMEM((p_slots,), jnp.int32),
                        pltpu.VMEM((MOE_NBUF, MOE_BLK * TOK_TILES, LANES), F32),
                        pltpu.SemaphoreType.DMA((MOE_NBUF,))],
    )
    return pl.pallas_call(
        _experts_kernel,
        grid_spec=grid_spec,
        out_shape=jax.ShapeDtypeStruct((p_slots * TOK_TILES, LANES), F32),
        compiler_params=_cparams(("arbitrary",)),
        name="moe_experts",
    )(dest, pstart, counts, block_e, nused, h8, wg, wu, wd)


def _combine_kernel(dest_ref, h8_ref, gcol_ref, ys8_hbm, lng_ref, lnb_ref, o_ref, buf, sems):
    i = pl.program_id(0)
    nt = pl.num_programs(0)
    n_tok = nt * COMB_TM

    def issue(tile, slot):
        for t in range(COMB_TM):
            for k in range(TOP_K):
                row = dest_ref[k * n_tok + tile * COMB_TM + t]
                _tile_copy(ys8_hbm, row, buf.at[slot, k], t, sems.at[slot]).start()

    def wait(slot):
        for t in range(COMB_TM):
            for k in range(TOP_K):
                _tile_copy(ys8_hbm, 0, buf.at[slot, k], t, sems.at[slot]).wait()

    slot = i % 2

    @pl.when(i == 0)
    def _():
        issue(0, 0)

    issue(jnp.minimum(i + 1, nt - 1), 1 - slot)
    wait(slot)
    g = gcol_ref[...]
    ffn = g[:, 0:1] * _load_token_tiles(buf.at[slot, 0], COMB_TM) + \
        g[:, 1:2] * _load_token_tiles(buf.at[slot, 1], COMB_TM)
    z = ALPHA * _load_token_tiles(h8_ref, COMB_TM) + ffn
    o_ref[...] = _layer_norm(z, lng_ref[...], lnb_ref[...])

    @pl.when(i == nt - 1)
    def _():
        wait(1 - slot)


def _combine(h8, gcol, ys8, dest, ln_g, ln_b):
    n = gcol.shape[0]
    const2 = lambda i, *_: (0, 0)
    grid_spec = pltpu.PrefetchScalarGridSpec(
        num_scalar_prefetch=1,
        grid=(n // COMB_TM,),
        in_specs=[pl.BlockSpec((COMB_TM * TOK_TILES, LANES), lambda i, *_: (i, 0)),
                  pl.BlockSpec((COMB_TM, LANES), lambda i, *_: (i, 0)),
                  pl.BlockSpec(memory_space=pl.ANY),
                  pl.BlockSpec((1, D_MODEL), const2),
                  pl.BlockSpec((1, D_MODEL), const2)],
        out_specs=pl.BlockSpec((COMB_TM, D_MODEL), lambda i, *_: (i, 0)),
        scratch_shapes=[pltpu.VMEM((2, TOP_K, COMB_TM * TOK_TILES, LANES), F32),
                        pltpu.SemaphoreType.DMA((2,))],
    )
    return pl.pallas_call(
        _combine_kernel,
        grid_spec=grid_spec,
        out_shape=jax.ShapeDtypeStruct((n, D_MODEL), F32),
        compiler_params=_cparams(("arbitrary",)),
        name="moe_combine",
    )(dest, h8, gcol, ys8, ln_g, ln_b)


def _moe_and_norm(h8, eidx, rank, gcol, counts, wg, wu, wd, layer, ln_g, ln_b):
    n = gcol.shape[0]
    p_slots = n * TOP_K + N_EXPERTS * MOE_BLK
    dest, pstart, cnt, block_e, nused = _slot_plan(eidx, rank, counts, p_slots // MOE_BLK)
    ys8 = _experts(h8, dest, pstart, cnt, block_e, nused, wg, wu, wd, layer, p_slots)
    return _combine(h8, gcol, ys8, dest, ln_g, ln_b)


def _retention_constants():
    log_gamma = jnp.log1p(-jnp.exp2(-5.0 - jnp.arange(RET_HEADS, dtype=F32)))
    idx = jnp.arange(CHUNK)
    rel = idx[:, None] - idx[None, :]
    decay_in = jnp.where(rel >= 0, jnp.exp(log_gamma[:, None, None] * jnp.maximum(rel, 0)), 0.0)
    xi = jnp.exp(log_gamma[:, None] * (idx + 1))
    zeta = jnp.exp(log_gamma[:, None] * (CHUNK - 1 - idx))
    g_chunk = jnp.exp(log_gamma * CHUNK)
    rc = jnp.zeros((CHUNK, LANES), F32)
    rc = rc.at[:, 0:RET_HEADS].set(xi.T)
    rc = rc.at[:, RET_HEADS:2 * RET_HEADS].set(zeta.T)
    rc = rc.at[:, 2 * RET_HEADS:3 * RET_HEADS].set(jnp.broadcast_to(g_chunk[None, :], (CHUNK, RET_HEADS)))
    return decay_in.astype(F32), rc


def kernel(x, w_in_even, i_bias, f_bias, conv_w, conv_b, mlstm_norm_g, ret_norm_g, w_out_even, w_in_odd,
           w_out_odd, router_w, router_bias, w_gate, w_up, w_down, ln_g, ln_b):
    batch, seq, _ = x.shape
    n = batch * seq
    assert all(w // d == CHUNK for w, d in DIL_CONFIGS) and seq % ATT_SPAN == 0
    h = x.reshape(n, D_MODEL)
    rw_t = router_w.T.astype(F32)
    rw_hi = rw_t.astype(BF16)
    router_wt = jnp.stack([rw_hi, (rw_t - rw_hi.astype(F32)).astype(BF16)])
    router_b = router_bias.astype(F32).reshape(N_EXPERTS, 1)
    decay_in, ret_consts = _retention_constants()
    tri = (jnp.arange(POST_TM)[:, None] <= jnp.arange(POST_TM)[None, :]).astype(BF16)
    att_bias = _attention_bias()
    gate_col0 = 4 * MLSTM_W
    gate_col1 = gate_col0 + 2 * MLSTM_HEADS

    for layer in range(DEPTH):
        j = layer // 2
        lg = lambda s: ln_g[layer, s].reshape(1, D_MODEL)
        lb = lambda s: ln_b[layer, s].reshape(1, D_MODEL)
        if layer % 2 == 0:
            w = w_in_even[j]
            w_re = jnp.concatenate(
                [w[:, :gate_col0], w[:, gate_col1:], w[:, gate_col0:gate_col1],
                 jnp.zeros((D_MODEL, LANES - 2 * MLSTM_HEADS), w.dtype)], axis=1).astype(BF16)
            gate_bias = jnp.zeros((1, LANES), F32)
            gate_bias = gate_bias.at[0, 0:MLSTM_HEADS].set(i_bias[j])
            gate_bias = gate_bias.at[0, MLSTM_HEADS:2 * MLSTM_HEADS].set(f_bias[j])
            norm_g = jnp.concatenate([mlstm_norm_g[j], ret_norm_g[j]]).reshape(1, MLSTM_W + RET_W)
            y = _even_mixer(h, w_re, batch, seq, conv_w[j], conv_b[j].reshape(1, 2 * MLSTM_W), gate_bias,
                            decay_in, ret_consts, norm_g.astype(F32))
            w_out_all = w_out_even
        else:
            p = _project(h, _odd_weights(w_in_odd[j]), "proj_odd")
            y = _dilated_attention(p, att_bias, batch, seq)
            w_out_all = w_out_odd
        h8, eidx, rank, gcol, counts = _post_mixer(h, y, w_out_all, j, lg(0), lb(0),
                                                   router_wt, router_b, tri, f"post_{layer}")
        h = _moe_and_norm(h8, eidx, rank, gcol, counts, w_gate, w_up, w_down, layer, lg(1), lb(1))
    return h.reshape(batch, seq, D_MODEL)
```

```python
import functools

import jax
import jax.numpy as jnp
from jax import lax
from jax.experimental import pallas as pl
from jax.experimental.pallas import tpu as pltpu

F32 = jnp.float32
BF16 = jnp.bfloat16

D_MODEL = 1024
DEPTH = 2
ALPHA = (2 * DEPTH) ** 0.25
LN_EPS = 1e-5
MLSTM_HEADS = 4
RET_HEADS = 4
HEAD_DIM = 128
MLSTM_W = MLSTM_HEADS * HEAD_DIM
RET_W = RET_HEADS * HEAD_DIM
CHUNK = 128
CONV_WIDTH = 4
DIL_HEADS = 8
DIL_DH = 64
DIL_W = DIL_HEADS * DIL_DH
DIL_CONFIGS = ((128, 1), (512, 4), (2048, 16))
N_EXPERTS = 32
N_GROUPS = 4
EXPERTS_PER_GROUP = N_EXPERTS // N_GROUPS
TOP_K = 2
D_EXPERT = 256

LANES = 128
SUBLANES = 8
VMEM_LIMIT_BYTES = 56 * 1024 * 1024

PROJ_TM = 512
POST_TM = 1024
MOE_BLK = 256
MOE_NBUF = 2
COMB_TM = 256
CONV_TAIL = SUBLANES

EV_QM, EV_KM, EV_VM, EV_OM = 0, MLSTM_W, 2 * MLSTM_W, 3 * MLSTM_W
EV_QR, EV_KR, EV_VR, EV_GR = (4 * MLSTM_W, 4 * MLSTM_W + RET_W, 4 * MLSTM_W + 2 * RET_W,
                              4 * MLSTM_W + 3 * RET_W)
EV_GATE = 4 * MLSTM_W + 4 * RET_W
EV_COLS = EV_GATE + LANES

NEG_INF = float("-inf")


def _cparams(sem):
    return pltpu.CompilerParams(dimension_semantics=sem, vmem_limit_bytes=VMEM_LIMIT_BYTES)


def _proj_kernel(x_ref, w_ref, o_ref):
    o_ref[...] = jnp.dot(x_ref[...].astype(BF16), w_ref[...], preferred_element_type=F32)


def _project(x, w, name):
    n, k = x.shape
    m = w.shape[1]
    return pl.pallas_call(
        _proj_kernel,
        grid=(n // PROJ_TM,),
        in_specs=[pl.BlockSpec((PROJ_TM, k), lambda i: (i, 0)),
                  pl.BlockSpec((k, m), lambda i: (0, 0))],
        out_specs=pl.BlockSpec((PROJ_TM, m), lambda i: (i, 0)),
        out_shape=jax.ShapeDtypeStruct((n, m), F32),
        compiler_params=_cparams(("parallel",)),
        name=name,
    )(x, w)


EV_STEP_CHUNKS = 2
EV_PROJ_PIECE = 2 * LANES


def _silu(x):
    return x * jax.nn.sigmoid(x)


def _head_norm(h):
    mu = jnp.mean(h, -1, keepdims=True)
    c = h - mu
    var = jnp.mean(c * c, -1, keepdims=True)
    return c * lax.rsqrt(var + LN_EPS)


def _dot_nt(a, b):
    return lax.dot_general(a, b, (((1,), (1,)), ((), ())), preferred_element_type=F32)


def _dot_tn(a_f32, b):
    return lax.dot_general(a_f32.astype(BF16), b, (((0,), (0,)), ((), ())), preferred_element_type=F32)


def _even_kernel(x0_ref, x_ref, w_ref, convw_ref, convb_ref, gbias_ref, dec_ref, rc_ref, ng_ref, y_ref,
                 p_scr, xb_scr, qkbuf, c_ref, n_ref, m_ref, r_ref, *, steps_per_batch):
    i = pl.program_id(0)

    @pl.when(i == 0)
    def _():
        p_scr[0] = jnp.dot(x0_ref[...].astype(BF16), w_ref[...], preferred_element_type=F32)

    @pl.when(i % steps_per_batch == 0)
    def _():
        qkbuf[0:CONV_TAIL, :] = jnp.zeros((CONV_TAIL, 2 * MLSTM_W), F32)
        c_ref[...] = jnp.zeros_like(c_ref)
        n_ref[...] = jnp.zeros_like(n_ref)
        m_ref[...] = jnp.zeros_like(m_ref)
        r_ref[...] = jnp.zeros_like(r_ref)

    xb_scr[...] = x_ref[...].astype(BF16)
    nxt = (i + 1) % 2
    pieces = iter(range(0, EV_COLS, EV_PROJ_PIECE))

    def project_piece():
        lo = next(pieces, None)
        if lo is not None:
            hi = min(lo + EV_PROJ_PIECE, EV_COLS)
            p_scr[nxt, :, lo:hi] = jnp.dot(xb_scr[...], w_ref[:, lo:hi], preferred_element_type=F32)

    cur = p_scr.at[i % 2]
    for c in range(EV_STEP_CHUNKS):
        rows = pl.ds(c * CHUNK, CHUNK)
        _mixer_chunk(cur.at[rows], y_ref.at[rows], convw_ref, convb_ref, gbias_ref, dec_ref, rc_ref, ng_ref,
                     qkbuf, c_ref, n_ref, m_ref, r_ref, project_piece)
    for _ in pieces:
        raise AssertionError("projection pieces left over: EV_PROJ_PIECE too small for the number of heads")


def _mixer_chunk(p_ref, y_ref, convw_ref, convb_ref, gbias_ref, dec_ref, rc_ref, ng_ref,
                 qkbuf, c_ref, n_ref, m_ref, r_ref, between):
    L = CHUNK
    scale = HEAD_DIM ** -0.5
    between()

    qkbuf[CONV_TAIL:CONV_TAIL + L, :] = p_ref[:, EV_QM:EV_QM + 2 * MLSTM_W]
    acc = jnp.broadcast_to(convb_ref[...], (L, 2 * MLSTM_W))
    for k in range(CONV_WIDTH):
        off = CONV_TAIL - (CONV_WIDTH - 1) + k
        acc = acc + convw_ref[k:k + 1, :] * qkbuf[off:off + L, :]
    qk = _silu(acc)
    qkbuf[0:CONV_TAIL, :] = qkbuf[L:L + CONV_TAIL, :]

    pre = p_ref[:, EV_GATE:EV_GATE + LANES] + gbias_ref[...]
    logf = jnp.minimum(pre, 0.0) - jnp.log1p(jnp.exp(-jnp.abs(pre)))
    row = lax.broadcasted_iota(jnp.int32, (L, L), 0)
    col = lax.broadcasted_iota(jnp.int32, (L, L), 1)
    causal = row >= col
    bcs = jnp.dot(causal.astype(F32), logf, preferred_element_type=F32,
                  precision=lax.Precision.HIGHEST)
    u_t = (pre - pltpu.roll(bcs, LANES - MLSTM_HEADS, 1)).T

    heads = range(MLSTM_HEADS)
    cols = [slice(h * HEAD_DIM, (h + 1) * HEAD_DIM) for h in heads]
    ph1 = []
    for h in heads:
        between()
        b_col = bcs[:, MLSTM_HEADS + h:MLSTM_HEADS + h + 1]
        i_col = pre[:, h:h + 1]
        r_row = u_t[h:h + 1, :]
        m_prev = m_ref[h:h + 1, 0:1]
        log_d = jnp.where(causal, b_col + r_row, NEG_INF)
        a = b_col + m_prev
        m_t = jnp.maximum(a, jnp.max(log_d, -1, keepdims=True))
        ph1.append((b_col, i_col, m_prev, m_t, jnp.exp(log_d - m_t), jnp.exp(a - m_t)))
    ph2 = []
    for h in heads:
        q = qk[:, cols[h]]
        k = qk[:, MLSTM_W + h * HEAD_DIM:MLSTM_W + (h + 1) * HEAD_DIM] * scale
        qb = q.astype(BF16)
        vb = p_ref[:, EV_VM + h * HEAD_DIM:EV_VM + (h + 1) * HEAD_DIM].astype(BF16)
        s = _dot_nt(qb, k.astype(BF16)) * ph1[h][4]
        ph2.append((q, k, qb, vb, s))
    ph3 = []
    for h in heads:
        _, _, _, m_t, _, inter = ph1[h]
        q, k, qb, vb, s = ph2[h]
        c_old = c_ref[h]
        n_old = n_ref[h:h + 1, :]
        num = jnp.dot(s.astype(BF16), vb, preferred_element_type=F32) + \
            inter * jnp.dot(qb, c_old.astype(BF16), preferred_element_type=F32)
        den = jnp.sum(s, -1, keepdims=True) + inter * jnp.sum(q * n_old, -1, keepdims=True)
        ph3.append((num / jnp.maximum(jnp.abs(den), jnp.exp(-m_t)), c_old, n_old))
    for h in heads:
        b_col, i_col, m_prev, _, _, _ = ph1[h]
        _, k, _, vb, _ = ph2[h]
        _, c_old, n_old = ph3[h]
        g = b_col[L - 1:L, :]
        log_w = g - b_col + i_col
        m_new = jnp.maximum(g + m_prev, jnp.max(log_w, 0, keepdims=True))
        kw = k * jnp.exp(log_w - m_new)
        decay = jnp.exp(g + m_prev - m_new)
        c_ref[h] = decay * c_old + _dot_tn(kw, vb)
        n_ref[h:h + 1, :] = decay * n_old + jnp.sum(kw, 0, keepdims=True)
        m_ref[h:h + 1, :] = jnp.broadcast_to(m_new, (1, LANES))
    for h in heads:
        o_gate = jax.nn.sigmoid(p_ref[:, EV_OM + h * HEAD_DIM:EV_OM + (h + 1) * HEAD_DIM])
        y_ref[:, cols[h]] = (_head_norm(o_gate * ph3[h][0]) * ng_ref[:, cols[h]]).astype(y_ref.dtype)

    rheads = range(RET_HEADS)
    rp1 = []
    for h in rheads:
        between()
        lo = h * HEAD_DIM
        qb = p_ref[:, EV_QR + lo:EV_QR + lo + HEAD_DIM].astype(BF16)
        k = p_ref[:, EV_KR + lo:EV_KR + lo + HEAD_DIM] * scale
        vb = p_ref[:, EV_VR + lo:EV_VR + lo + HEAD_DIM].astype(BF16)
        rp1.append((qb, k, vb, _dot_nt(qb, k.astype(BF16)) * dec_ref[h]))
    rp2 = []
    for h in rheads:
        qb, k, vb, inner = rp1[h]
        xi = rc_ref[:, h:h + 1]
        r_old = r_ref[h]
        y = jnp.dot(inner.astype(BF16), vb, preferred_element_type=F32) + \
            xi * jnp.dot(qb, r_old.astype(BF16), preferred_element_type=F32)
        rp2.append((y, r_old))
    for h in rheads:
        _, k, vb, _ = rp1[h]
        zeta = rc_ref[:, RET_HEADS + h:RET_HEADS + h + 1]
        g_chunk = rc_ref[0:1, 2 * RET_HEADS + h:2 * RET_HEADS + h + 1]
        r_ref[h] = g_chunk * rp2[h][1] + _dot_tn(k * zeta, vb)
    for h in rheads:
        lo = h * HEAD_DIM
        gr = p_ref[:, EV_GR + lo:EV_GR + lo + HEAD_DIM]
        out = _head_norm(rp2[h][0]) * ng_ref[:, MLSTM_W + lo:MLSTM_W + lo + HEAD_DIM] * _silu(gr)
        y_ref[:, MLSTM_W + lo:MLSTM_W + lo + HEAD_DIM] = out.astype(y_ref.dtype)


def _even_mixer(x, w, batch, seq, conv_w, conv_b, gate_bias, decay_in, ret_consts, norm_g):
    tm = EV_STEP_CHUNKS * CHUNK
    steps_per_batch = seq // tm
    nsteps = batch * steps_per_batch
    const2 = lambda i: (0, 0)
    return pl.pallas_call(
        functools.partial(_even_kernel, steps_per_batch=steps_per_batch),
        grid=(nsteps,),
        in_specs=[pl.BlockSpec((tm, D_MODEL), const2),
                  pl.BlockSpec((tm, D_MODEL), lambda i: (jnp.minimum(i + 1, nsteps - 1), 0)),
                  pl.BlockSpec((D_MODEL, EV_COLS), const2),
                  pl.BlockSpec((CONV_WIDTH, 2 * MLSTM_W), const2),
                  pl.BlockSpec((1, 2 * MLSTM_W), const2),
                  pl.BlockSpec((1, LANES), const2),
                  pl.BlockSpec((RET_HEADS, CHUNK, CHUNK), lambda i: (0, 0, 0)),
                  pl.BlockSpec((CHUNK, LANES), const2),
                  pl.BlockSpec((1, MLSTM_W + RET_W), const2)],
        out_specs=pl.BlockSpec((tm, MLSTM_W + RET_W), lambda i: (i, 0)),
        out_shape=jax.ShapeDtypeStruct((batch * seq, MLSTM_W + RET_W), BF16),
        scratch_shapes=[pltpu.VMEM((2, tm, EV_COLS), F32),
                        pltpu.VMEM((tm, D_MODEL), BF16),
                        pltpu.VMEM((CONV_TAIL + CHUNK, 2 * MLSTM_W), F32),
                        pltpu.VMEM((MLSTM_HEADS, HEAD_DIM, HEAD_DIM), F32),
                        pltpu.VMEM((SUBLANES, HEAD_DIM), F32),
                        pltpu.VMEM((SUBLANES, LANES), F32),
                        pltpu.VMEM((RET_HEADS, HEAD_DIM, HEAD_DIM), F32)],
        compiler_params=_cparams(("arbitrary",)),
        name="even_mixer",
    )(x, x, w, conv_w, conv_b, gate_bias, decay_in, ret_consts, norm_g)


TOK_TILES = D_MODEL // LANES


def _store_token_tiles(ref, val, rows):
    for c in range(TOK_TILES):
        ref[pl.ds(c, rows, stride=TOK_TILES), :] = val[:, c * LANES:(c + 1) * LANES]


def _load_token_tiles(ref, rows):
    return jnp.concatenate([ref[pl.ds(c, rows, stride=TOK_TILES), :] for c in range(TOK_TILES)], axis=1)


def _layer_norm(z, g, b):
    mu = jnp.mean(z, -1, keepdims=True)
    c = z - mu
    var = jnp.mean(c * c, -1, keepdims=True)
    return c * lax.rsqrt(var + LN_EPS) * g + b


def _route(h, rwt_ref, rb_ref):
    h_hi = h.astype(BF16)
    h_lo = (h - h_hi.astype(F32)).astype(BF16)
    logits = _dot_nt(rwt_ref[0], h_hi) + (_dot_nt(rwt_ref[0], h_lo) + _dot_nt(rwt_ref[1], h_hi))
    aff = jax.nn.sigmoid(logits)
    sel = aff + rb_ref[...]
    tm = h.shape[0]
    sub = lax.broadcasted_iota(jnp.int32, (EXPERTS_PER_GROUP, tm), 0)
    best = None
    for g in range(N_GROUPS):
        lo = g * EXPERTS_PER_GROUP
        sg = sel[lo:lo + EXPERTS_PER_GROUP, :]
        ag = aff[lo:lo + EXPERTS_PER_GROUP, :]
        v1 = jnp.max(sg, 0, keepdims=True)
        i1 = jnp.min(jnp.where(sg == v1, sub, EXPERTS_PER_GROUP), 0, keepdims=True)
        rest = jnp.where(sub == i1, NEG_INF, sg)
        v2 = jnp.max(rest, 0, keepdims=True)
        i2 = jnp.min(jnp.where(rest == v2, sub, EXPERTS_PER_GROUP), 0, keepdims=True)
        a1 = jnp.sum(jnp.where(sub == i1, ag, 0.0), 0, keepdims=True)
        a2 = jnp.sum(jnp.where(sub == i2, ag, 0.0), 0, keepdims=True)
        cand = (v1 + v2, i1 + lo, i2 + lo, a1, a2)
        if best is None:
            best = cand
        else:
            take = cand[0] > best[0]
            best = tuple(jnp.where(take, c, b) for c, b in zip(cand, best))
    _, e1, e2, a1, a2 = best
    tot = a1 + a2
    return e1, e2, a1 / tot, a2 / tot


def _post_kernel(x_ref, y_ref, wo_ref, lng_ref, lnb_ref, rwt_ref, rb_ref, tri_ref,
                 h8_ref, eidx_ref, rank_ref, gcol_ref, cnt_ref, run_ref, wo_scr):
    tm = x_ref.shape[0]

    @pl.when(pl.program_id(0) == 0)
    def _():
        run_ref[...] = jnp.zeros_like(run_ref)
        wo_scr[...] = wo_ref[0].astype(BF16)

    mix = jnp.dot(y_ref[...], wo_scr[...], preferred_element_type=F32)
    h = _layer_norm(ALPHA * x_ref[...] + mix, lng_ref[...], lnb_ref[...])
    _store_token_tiles(h8_ref, h, tm)

    e1, e2, g1, g2 = _route(h, rwt_ref, rb_ref)
    eidx_ref[0:1, :] = e1
    eidx_ref[1:2, :] = e2
    gates = jnp.concatenate([g1, g2, jnp.zeros((LANES - TOP_K, tm), F32)], axis=0)
    gcol_ref[...] = gates.T

    sub = lax.broadcasted_iota(jnp.int32, (N_EXPERTS, tm), 0)
    oh1 = sub == e1
    oh2 = sub == e2
    chosen = jnp.logical_or(oh1, oh2).astype(F32)
    incl = jnp.dot(chosen.astype(BF16), tri_ref[...], preferred_element_type=F32)
    before = run_ref[:, 0:1] + incl - chosen
    rank_ref[0:1, :] = jnp.sum(jnp.where(oh1, before, 0.0), 0, keepdims=True).astype(jnp.int32)
    rank_ref[1:2, :] = jnp.sum(jnp.where(oh2, before, 0.0), 0, keepdims=True).astype(jnp.int32)
    run_new = run_ref[...] + incl[:, tm - 1:tm]
    run_ref[...] = run_new
    cnt_ref[...] = run_new.astype(jnp.int32)


def _post_mixer(x, y, w_out_all, j, ln_g, ln_b, router_wt, router_b, tri, name):
    n = x.shape[0]
    tm = POST_TM
    kdim = w_out_all.shape[1]
    const2 = lambda i: (0, 0)
    return pl.pallas_call(
        _post_kernel,
        grid=(n // tm,),
        in_specs=[pl.BlockSpec((tm, D_MODEL), lambda i: (i, 0)),
                  pl.BlockSpec((tm, y.shape[1]), lambda i: (i, 0)),
                  pl.BlockSpec((1, kdim, D_MODEL), lambda i: (j, 0, 0)),
                  pl.BlockSpec((1, D_MODEL), const2),
                  pl.BlockSpec((1, D_MODEL), const2),
                  pl.BlockSpec((2, N_EXPERTS, D_MODEL), lambda i: (0, 0, 0)),
                  pl.BlockSpec((N_EXPERTS, 1), const2),
                  pl.BlockSpec((tm, tm), const2)],
        out_specs=[pl.BlockSpec((tm * TOK_TILES, LANES), lambda i: (i, 0)),
                   pl.BlockSpec((TOP_K, tm), lambda i: (0, i)),
                   pl.BlockSpec((TOP_K, tm), lambda i: (0, i)),
                   pl.BlockSpec((tm, LANES), lambda i: (i, 0)),
                   pl.BlockSpec((N_EXPERTS, LANES), const2)],
        out_shape=[jax.ShapeDtypeStruct((n * TOK_TILES, LANES), F32),
                   jax.ShapeDtypeStruct((TOP_K, n), jnp.int32),
                   jax.ShapeDtypeStruct((TOP_K, n), jnp.int32),
                   jax.ShapeDtypeStruct((n, LANES), F32),
                   jax.ShapeDtypeStruct((N_EXPERTS, LANES), jnp.int32)],
        scratch_shapes=[pltpu.VMEM((N_EXPERTS, LANES), F32),
                        pltpu.VMEM((kdim, D_MODEL), BF16)],
        compiler_params=_cparams(("arbitrary",)),
        name=name,
    )(x, y, w_out_all, ln_g, ln_b, router_wt, router_b, tri)


ATT_SPAN = max(w for w, _ in DIL_CONFIGS)
ATT_PAIR = 2
ATT_PW = ATT_PAIR * DIL_DH
N_DIL = len(DIL_CONFIGS)


ATT_GROUP = 4


def _for_block_groups(n, body):
    full = n // ATT_GROUP
    if full:
        def trip(t, _):
            body([t * ATT_GROUP + u for u in range(ATT_GROUP)])
            return 0
        lax.fori_loop(0, full, trip, 0)
    if n % ATT_GROUP:
        body(list(range(full * ATT_GROUP, n)))


def _attn_rows(start, stride):
    L = CHUNK
    if stride == 1:
        if not isinstance(start, int):
            start = pl.multiple_of(start, L)
        return pl.ds(start, L), pl.ds(start, 2 * L)
    return pl.ds(start, L, stride=stride), pl.ds(start, 2 * L, stride=stride)


def _attn_units(q_ref, kbuf, vbuf, bias, g, starts, stride, noprev, o_scr, m_scr, l_scr):
    rows = [_attn_rows(s, stride) for s in starts]
    loaded = [(q_ref[qr, :].astype(BF16),
               kbuf[kvr, :].astype(BF16),
               vbuf[kvr, :].astype(BF16)) for qr, kvr in rows]
    scores = [_attn_scores(q2, kb, bias, noprev) for q2, kb, _ in loaded]
    probs = [_attn_softmax(s) for s in scores]
    outs = [_attn_values(p, vb) for (p, _, _), (_, _, vb) in zip(probs, loaded)]
    for (qr, _), o, (_, mm, ll) in zip(rows, outs, probs):
        o_scr[g, qr, :] = o
        m_scr[g, qr, :] = mm
        l_scr[g, qr, :] = ll


def _two_heads(x):
    head0 = lax.broadcasted_iota(jnp.int32, x.shape, 1) < DIL_DH
    zero = jnp.zeros_like(x)
    return jnp.concatenate([jnp.where(head0, x, zero), jnp.where(head0, zero, x)], axis=0)


def _attn_scores(q2, kb, bias, noprev):
    L = CHUNK
    s = _dot_nt(q2, _two_heads(kb)) + bias
    if noprev is not None:
        col = lax.broadcasted_iota(jnp.int32, (L, 4 * L), 1)
        prev_col = (col % (2 * L)) < L
        s = jnp.where(jnp.logical_and(prev_col, noprev), NEG_INF, s)
    return s


def _attn_softmax(s):
    L = CHUNK
    s0, s1 = s[:, 0:2 * L], s[:, 2 * L:4 * L]
    m0 = jnp.max(s0, -1, keepdims=True)
    m1 = jnp.max(s1, -1, keepdims=True)
    p0 = jnp.exp(s0 - m0)
    p1 = jnp.exp(s1 - m1)
    l0 = jnp.sum(p0, -1, keepdims=True)
    l1 = jnp.sum(p1, -1, keepdims=True)
    left = lax.broadcasted_iota(jnp.int32, (L, ATT_PW), 1) < DIL_DH
    p = jnp.concatenate([p0, p1], axis=1).astype(BF16)
    return p, jnp.where(left, m0, m1), jnp.where(left, l0, l1)


def _attn_values(p, vb):
    return jnp.dot(p, _two_heads(vb), preferred_element_type=F32)


def _attn_kernel(*refs):
    q_refs, kc_refs, vc_refs, kp_refs, vp_refs = (refs[i * N_DIL:(i + 1) * N_DIL] for i in range(5))
    bias_ref, y_ref = refs[5 * N_DIL:5 * N_DIL + 2]
    scr = refs[5 * N_DIL + 2:]
    kbufs, vbufs = scr[0:N_DIL], scr[N_DIL:2 * N_DIL]
    o_scr, m_scr, l_scr = scr[2 * N_DIL:]
    first = pl.program_id(2) == 0
    for g, (window, _) in enumerate(DIL_CONFIGS):
        kbufs[g][0:window, :] = kp_refs[g][...]
        vbufs[g][0:window, :] = vp_refs[g][...]
        kbufs[g][window:window + ATT_SPAN, :] = kc_refs[g][...]
        vbufs[g][window:window + ATT_SPAN, :] = vc_refs[g][...]

    for g, (window, dilation) in enumerate(DIL_CONFIGS):
        bias = bias_ref[0, g]
        nsub = ATT_SPAN // window

        def first_blocks(idx, g=g, bias=bias, dilation=dilation):
            _attn_units(q_refs[g], kbufs[g], vbufs[g], bias, g, idx, dilation, first, o_scr, m_scr, l_scr)
        _for_block_groups(dilation, first_blocks)

        def rest_blocks(idx, g=g, bias=bias, window=window, dilation=dilation):
            starts = [(1 + i // dilation) * window + i % dilation for i in idx]
            _attn_units(q_refs[g], kbufs[g], vbufs[g], bias, g, starts, dilation, None, o_scr, m_scr, l_scr)
        _for_block_groups((nsub - 1) * dilation, rest_blocks)

    rows = 256

    def merge(i, _):
        sl = pl.ds(pl.multiple_of(i * rows, rows), rows)
        ms = [m_scr[g, sl, :] for g in range(N_DIL)]
        mx = functools.reduce(jnp.maximum, ms)
        es = [jnp.exp(m - mx) for m in ms]
        num = sum(e * o_scr[g, sl, :] for g, e in enumerate(es))
        den = sum(e * l_scr[g, sl, :] for g, e in enumerate(es))
        y_ref[sl, :] = (num / den).astype(y_ref.dtype)
        return 0
    lax.fori_loop(0, ATT_SPAN // rows, merge, 0)


def _dilated_attention(p, bias, batch, seq):
    n = p.shape[0]
    nspan = seq // ATT_SPAN
    npair = DIL_HEADS // ATT_PAIR
    slab = DIL_W // ATT_PW

    def col(g, c, pair):
        return (g * 3 + c) * slab + pair

    def cur_spec(g, c):
        return pl.BlockSpec((ATT_SPAN, ATT_PW), lambda b, pr, s: (b * nspan + s, col(g, c, pr)))

    def prev_spec(g, c):
        window = DIL_CONFIGS[g][0]
        per_span = ATT_SPAN // window
        rows_per_batch = seq // window
        return pl.BlockSpec(
            (window, ATT_PW),
            lambda b, pr, s: (jnp.maximum(b * rows_per_batch + s * per_span - 1, 0), col(g, c, pr)))

    groups = range(N_DIL)
    in_specs = ([cur_spec(g, 0) for g in groups] + [cur_spec(g, 1) for g in groups] +
                [cur_spec(g, 2) for g in groups] + [prev_spec(g, 1) for g in groups] +
                [prev_spec(g, 2) for g in groups] +
                [pl.BlockSpec((1, N_DIL, CHUNK, 4 * CHUNK), lambda b, pr, s: (pr, 0, 0, 0))])
    return pl.pallas_call(
        _attn_kernel,
        grid=(batch, npair, nspan),
        in_specs=in_specs,
        out_specs=pl.BlockSpec((ATT_SPAN, ATT_PW), lambda b, pr, s: (b * nspan + s, pr)),
        out_shape=jax.ShapeDtypeStruct((n, DIL_W), BF16),
        scratch_shapes=[pltpu.VMEM((w + ATT_SPAN, ATT_PW), F32) for w, _ in DIL_CONFIGS] * 2 +
                       [pltpu.VMEM((N_DIL, ATT_SPAN, ATT_PW), F32) for _ in range(3)],
        compiler_params=_cparams(("parallel", "parallel", "arbitrary")),
        name="dilated_attention",
    )(*([p] * (5 * N_DIL)), bias)


def _attention_bias():
    L = CHUNK
    qi = jnp.arange(L)[:, None]
    kr = jnp.arange(2 * L)[None, :]
    steps = qi + L - kr
    band = (steps >= 0) & (steps <= L)
    slopes = jnp.exp2(-8.0 * jnp.arange(1, DIL_HEADS + 1, dtype=F32) / DIL_HEADS)
    per_group = []
    for _, dilation in DIL_CONFIGS:
        b = -slopes[:, None, None] * (steps * dilation).astype(F32)[None]
        b = jnp.where(band[None], b, NEG_INF)
        per_group.append(b.reshape(DIL_HEADS // ATT_PAIR, ATT_PAIR, L, 2 * L)
                         .transpose(0, 2, 1, 3).reshape(DIL_HEADS // ATT_PAIR, L, 4 * L))
    return jnp.stack(per_group, axis=1)


def _odd_weights(w):
    is_q = (jnp.arange(w.shape[1]) // DIL_W) % 3 == 0
    return (w * jnp.where(is_q, DIL_DH ** -0.5, 1.0)[None, :]).astype(BF16)


def _tile_copy(src_hbm, row, dst, i, sem):
    src = src_hbm.at[pl.ds(pl.multiple_of(row * TOK_TILES, TOK_TILES), TOK_TILES), :]
    return pltpu.make_async_copy(src, dst.at[pl.ds(i * TOK_TILES, TOK_TILES), :], sem)


META_PSTART, META_COUNT, META_NUSED, META_BLOCK_E = 0, 1, 2, 3
META_ROWS = SUBLANES


def _slot_plan_kernel(eidx_ref, rank_ref, cnt_ref, dest_ref, meta_ref, *, n_blocks):
    blk = float(MOE_BLK)
    hi = lax.Precision.HIGHEST
    cnt_b = cnt_ref[...].astype(F32)
    padded_b = jnp.floor((cnt_b + (blk - 1.0)) / blk) * blk
    er = lax.broadcasted_iota(jnp.int32, (LANES, LANES), 0)
    ec = lax.broadcasted_iota(jnp.int32, (LANES, LANES), 1)
    padded_full = jnp.concatenate([padded_b, jnp.zeros((LANES - N_EXPERTS, LANES), F32)], axis=0)
    pstart_b = jnp.dot((ec < er).astype(F32), padded_full, preferred_element_type=F32,
                       precision=hi)[0:N_EXPERTS, :]
    pend_b = pstart_b + padded_b

    sub = lax.broadcasted_iota(jnp.int32, (N_EXPERTS, eidx_ref.shape[1]), 0)
    pcol = pstart_b[:, 0:1]
    for k in range(TOP_K):
        start = jnp.sum(jnp.where(sub == eidx_ref[k:k + 1, :], pcol, 0.0), 0, keepdims=True)
        dest_ref[k:k + 1, :] = start.astype(jnp.int32) + rank_ref[k:k + 1, :]

    def as_row(col_b):
        full = jnp.concatenate([col_b, jnp.zeros((LANES - N_EXPERTS, LANES), F32)], axis=0)
        return full.T[0:1, :]
    meta_ref[...] = jnp.zeros_like(meta_ref)
    meta_ref[META_PSTART:META_PSTART + 1, :] = as_row(pstart_b).astype(jnp.int32)
    meta_ref[META_COUNT:META_COUNT + 1, :] = as_row(cnt_b).astype(jnp.int32)
    total = pend_b[N_EXPERTS - 1:N_EXPERTS, :]
    meta_ref[META_NUSED:META_NUSED + 1, :] = (total / blk).astype(jnp.int32)
    for r in range(-(-n_blocks // LANES)):
        first = (lax.broadcasted_iota(jnp.int32, (N_EXPERTS, LANES), 1) + r * LANES).astype(F32) * blk
        owner = jnp.sum((pend_b[:, 0:1] <= first).astype(jnp.int32), 0, keepdims=True)
        meta_ref[META_BLOCK_E + r:META_BLOCK_E + r + 1, :] = jnp.minimum(owner, N_EXPERTS - 1)


def _slot_plan(eidx, rank, counts, n_blocks):
    n = eidx.shape[1]
    assert META_BLOCK_E + -(-n_blocks // LANES) <= META_ROWS
    dest, meta = pl.pallas_call(
        functools.partial(_slot_plan_kernel, n_blocks=n_blocks),
        out_shape=[jax.ShapeDtypeStruct((TOP_K, n), jnp.int32),
                   jax.ShapeDtypeStruct((META_ROWS, LANES), jnp.int32)],
        compiler_params=pltpu.CompilerParams(vmem_limit_bytes=VMEM_LIMIT_BYTES),
        name="moe_slot_plan",
    )(eidx, rank, counts)
    pstart = meta[META_PSTART, :N_EXPERTS]
    cnt = meta[META_COUNT, :N_EXPERTS]
    nused = meta[META_NUSED, :1]
    block_e = meta[META_BLOCK_E:, :].reshape(-1)[:n_blocks]
    return dest.reshape(-1), pstart, cnt, block_e, nused


def _experts_kernel(dest_ref, pstart_ref, count_ref, block_e_ref, nused_ref,
                    h8_hbm, wg_ref, wu_ref, wd_ref, ys8_ref, slot_tok, xbuf, sems):
    j = pl.program_id(0)
    nused = nused_ref[0]
    n_tok = dest_ref.shape[0] // TOP_K

    @pl.when(j == 0)
    def _():
        def fill(t, _):
            for k in range(TOP_K):
                slot_tok[dest_ref[k * n_tok + t]] = t
            return 0
        lax.fori_loop(0, n_tok, fill, 0, unroll=8)

        def pad_expert(e, _):
            cnt = count_ref[e]
            padded = (cnt + MOE_BLK - 1) // MOE_BLK * MOE_BLK

            def pad(i, _):
                s = pstart_ref[e] + i
                slot_tok[s] = s % n_tok
                return 0
            lax.fori_loop(cnt, padded, pad, 0)
            return 0
        lax.fori_loop(0, N_EXPERTS, pad_expert, 0)

    def issue(blk, slot):
        for i in range(MOE_BLK):
            _tile_copy(h8_hbm, slot_tok[blk * MOE_BLK + i], xbuf.at[slot], i, sems.at[slot]).start()

    def wait(slot):
        for i in range(MOE_BLK):
            _tile_copy(h8_hbm, 0, xbuf.at[slot], i, sems.at[slot]).wait()

    ahead = MOE_NBUF - 1
    slot = j % MOE_NBUF

    @pl.when(jnp.logical_and(j == 0, nused > 0))
    def _():
        for a in range(ahead):
            issue(jnp.minimum(a, nused - 1), a)

    @pl.when(j < nused)
    def _():
        issue(jnp.minimum(j + ahead, nused - 1), (j + ahead) % MOE_NBUF)
        wait(slot)
        x = _load_token_tiles(xbuf.at[slot], MOE_BLK).astype(BF16)
        a = jnp.dot(x, wg_ref[0, 0].astype(BF16), preferred_element_type=F32)
        u = jnp.dot(x, wu_ref[0, 0].astype(BF16), preferred_element_type=F32)
        hm = (_silu(a) * u).astype(BF16)
        y = jnp.dot(hm, wd_ref[0, 0].astype(BF16), preferred_element_type=F32)
        _store_token_tiles(ys8_ref, y, MOE_BLK)

    @pl.when(j == nused - 1)
    def _():
        for a in range(1, MOE_NBUF):
            wait((slot + a) % MOE_NBUF)

    @pl.when(j >= nused)
    def _():
        ys8_ref[...] = jnp.zeros_like(ys8_ref)


def _experts(h8, dest, pstart, counts, block_e, nused, wg, wu, wd, layer, p_slots):
    n_blocks = p_slots // MOE_BLK
    wmap = lambda j, de, ps, ct, be, nu: (layer, be[j], 0, 0)
    grid_spec = pltpu.PrefetchScalarGridSpec(
        num_scalar_prefetch=5,
        grid=(n_blocks,),
        in_specs=[pl.BlockSpec(memory_space=pl.ANY),
                  pl.BlockSpec((1, 1, D_MODEL, D_EXPERT), wmap),
                  pl.BlockSpec((1, 1, D_MODEL, D_EXPERT), wmap),
                  pl.BlockSpec((1, 1, D_EXPERT, D_MODEL), wmap)],
        out_specs=pl.BlockSpec((MOE_BLK * TOK_TILES, LANES), lambda j, *_: (j, 0)),
        scratch_shapes=[pltpu.SMEM((p_slots,), jnp.int32),
                        pltpu.VMEM((MOE_NBUF, MOE_BLK * TOK_TILES, LANES), F32),
                        pltpu.SemaphoreType.DMA((MOE_NBUF,))],
    )
    return pl.pallas_call(
        _experts_kernel,
        grid_spec=grid_spec,
        out_shape=jax.ShapeDtypeStruct((p_slots * TOK_TILES, LANES), F32),
        compiler_params=_cparams(("arbitrary",)),
        name="moe_experts",
    )(dest, pstart, counts, block_e, nused, h8, wg, wu, wd)


def _combine_kernel(dest_ref, h8_ref, gcol_ref, ys8_hbm, lng_ref, lnb_ref, o_ref, buf, sems):
    i = pl.program_id(0)
    nt = pl.num_programs(0)
    n_tok = nt * COMB_TM

    def issue(tile, slot):
        for t in range(COMB_TM):
            for k in range(TOP_K):
                row = dest_ref[k * n_tok + tile * COMB_TM + t]
                _tile_copy(ys8_hbm, row, buf.at[slot, k], t, sems.at[slot]).start(priority=k)

    def wait(slot):
        for t in range(COMB_TM):
            for k in range(TOP_K):
                _tile_copy(ys8_hbm, 0, buf.at[slot, k], t, sems.at[slot]).wait()

    slot = i % 2

    @pl.when(i == 0)
    def _():
        issue(0, 0)

    issue(jnp.minimum(i + 1, nt - 1), 1 - slot)
    wait(slot)
    g = gcol_ref[...]
    ffn = g[:, 0:1] * _load_token_tiles(buf.at[slot, 0], COMB_TM) + \
        g[:, 1:2] * _load_token_tiles(buf.at[slot, 1], COMB_TM)
    z = ALPHA * _load_token_tiles(h8_ref, COMB_TM) + ffn
    o_ref[...] = _layer_norm(z, lng_ref[...], lnb_ref[...])

    @pl.when(i == nt - 1)
    def _():
        wait(1 - slot)


def _combine(h8, gcol, ys8, dest, ln_g, ln_b):
    n = gcol.shape[0]
    const2 = lambda i, *_: (0, 0)
    grid_spec = pltpu.PrefetchScalarGridSpec(
        num_scalar_prefetch=1,
        grid=(n // COMB_TM,),
        in_specs=[pl.BlockSpec((COMB_TM * TOK_TILES, LANES), lambda i, *_: (i, 0)),
                  pl.BlockSpec((COMB_TM, LANES), lambda i, *_: (i, 0)),
                  pl.BlockSpec(memory_space=pl.ANY),
                  pl.BlockSpec((1, D_MODEL), const2),
                  pl.BlockSpec((1, D_MODEL), const2)],
        out_specs=pl.BlockSpec((COMB_TM, D_MODEL), lambda i, *_: (i, 0)),
        scratch_shapes=[pltpu.VMEM((2, TOP_K, COMB_TM * TOK_TILES, LANES), F32),
                        pltpu.SemaphoreType.DMA((2,))],
    )
    return pl.pallas_call(
        _combine_kernel,
        grid_spec=grid_spec,
        out_shape=jax.ShapeDtypeStruct((n, D_MODEL), F32),
        compiler_params=_cparams(("arbitrary",)),
        name="moe_combine",
    )(dest, h8, gcol, ys8, ln_g, ln_b)


def _moe_and_norm(h8, eidx, rank, gcol, counts, wg, wu, wd, layer, ln_g, ln_b):
    n = gcol.shape[0]
    p_slots = n * TOP_K + N_EXPERTS * MOE_BLK
    dest, pstart, cnt, block_e, nused = _slot_plan(eidx, rank, counts, p_slots // MOE_BLK)
    ys8 = _experts(h8, dest, pstart, cnt, block_e, nused, wg, wu, wd, layer, p_slots)
    return _combine(h8, gcol, ys8, dest, ln_g, ln_b)


def _retention_constants():
    log_gamma = jnp.log1p(-jnp.exp2(-5.0 - jnp.arange(RET_HEADS, dtype=F32)))
    idx = jnp.arange(CHUNK)
    rel = idx[:, None] - idx[None, :]
    decay_in = jnp.where(rel >= 0, jnp.exp(log_gamma[:, None, None] * jnp.maximum(rel, 0)), 0.0)
    xi = jnp.exp(log_gamma[:, None] * (idx + 1))
    zeta = jnp.exp(log_gamma[:, None] * (CHUNK - 1 - idx))
    g_chunk = jnp.exp(log_gamma * CHUNK)
    rc = jnp.zeros((CHUNK, LANES), F32)
    rc = rc.at[:, 0:RET_HEADS].set(xi.T)
    rc = rc.at[:, RET_HEADS:2 * RET_HEADS].set(zeta.T)
    rc = rc.at[:, 2 * RET_HEADS:3 * RET_HEADS].set(jnp.broadcast_to(g_chunk[None, :], (CHUNK, RET_HEADS)))
    return decay_in.astype(F32), rc


def kernel(x, w_in_even, i_bias, f_bias, conv_w, conv_b, mlstm_norm_g, ret_norm_g, w_out_even, w_in_odd,
           w_out_odd, router_w, router_bias, w_gate, w_up, w_down, ln_g, ln_b):
    batch, seq, _ = x.shape
    n = batch * seq
    assert all(w // d == CHUNK for w, d in DIL_CONFIGS) and seq % ATT_SPAN == 0
    h = x.reshape(n, D_MODEL)
    rw_t = router_w.T.astype(F32)
    rw_hi = rw_t.astype(BF16)
    router_wt = jnp.stack([rw_hi, (rw_t - rw_hi.astype(F32)).astype(BF16)])
    router_b = router_bias.astype(F32).reshape(N_EXPERTS, 1)
    decay_in, ret_consts = _retention_constants()
    tri = (jnp.arange(POST_TM)[:, None] <= jnp.arange(POST_TM)[None, :]).astype(BF16)
    att_bias = _attention_bias()
    gate_col0 = 4 * MLSTM_W
    gate_col1 = gate_col0 + 2 * MLSTM_HEADS

    for layer in range(DEPTH):
        j = layer // 2
        lg = lambda s: ln_g[layer, s].reshape(1, D_MODEL)
        lb = lambda s: ln_b[layer, s].reshape(1, D_MODEL)
        if layer % 2 == 0:
            w = w_in_even[j]
            w_re = jnp.concatenate(
                [w[:, :gate_col0], w[:, gate_col1:], w[:, gate_col0:gate_col1],
                 jnp.zeros((D_MODEL, LANES - 2 * MLSTM_HEADS), w.dtype)], axis=1).astype(BF16)
            gate_bias = jnp.zeros((1, LANES), F32)
            gate_bias = gate_bias.at[0, 0:MLSTM_HEADS].set(i_bias[j])
            gate_bias = gate_bias.at[0, MLSTM_HEADS:2 * MLSTM_HEADS].set(f_bias[j])
            norm_g = jnp.concatenate([mlstm_norm_g[j], ret_norm_g[j]]).reshape(1, MLSTM_W + RET_W)
            y = _even_mixer(h, w_re, batch, seq, conv_w[j], conv_b[j].reshape(1, 2 * MLSTM_W), gate_bias,
                            decay_in, ret_consts, norm_g.astype(F32))
            w_out_all = w_out_even
        else:
            p = _project(h, _odd_weights(w_in_odd[j]), "proj_odd")
            y = _dilated_attention(p, att_bias, batch, seq)
            w_out_all = w_out_odd
        h8, eidx, rank, gcol, counts = _post_mixer(h, y, w_out_all, j, lg(0), lb(0),
                                                   router_wt, router_b, tri, f"post_{layer}")
        h = _moe_and_norm(h8, eidx, rank, gcol, counts, w_gate, w_up, w_down, layer, lg(1), lb(1))
    return h.reshape(batch, seq, D_MODEL)
```
